```python
import math
import jax, jax.numpy as jnp
from jax import lax
import numpy as np

D_MODEL = 4096
BATCH = 32
SEQ = 256
DEPTH = 2
DEC_BATCH = 4
DEC_SEQ = 4096
PAST_LEN = 512

GRID_W = 64
D_MIX = D_MODEL
GROUP_W = D_MIX // 4

POOL_WINDOWS = (2, 4, 8, 16)
POOL_GW = GROUP_W // 4

GLA_HEADS = 8
GLA_DV = GROUP_W // GLA_HEADS
GLA_DK = GLA_DV // 2
GLA_RANK = 16
GLA_GATE_NORM = 16.0
GLA_CHUNK = 64

DIFF_HEADS = 8
DIFF_VD = GROUP_W // DIFF_HEADS
DIFF_HD = DIFF_VD // 2
ROPE_BASE = 10000.0

NAT_HEADS = 8
NAT_HD = GROUP_W // NAT_HEADS
NAT_WH = 8
NAT_WW = 16

MOE_GROUPS = 4
MOE_PER_GROUP = 8
MOE_EXPERTS = MOE_GROUPS * MOE_PER_GROUP
MOE_TOPK = 2
MOE_FF = 512

Q_BLOCK = 128
EPS = 1e-6

IN_SIZES = (GROUP_W, GLA_HEADS * GLA_DK, GLA_HEADS * GLA_DK, GROUP_W, GROUP_W, 2 * GLA_RANK,
            GROUP_W, GROUP_W, GROUP_W, GROUP_W, GROUP_W, GROUP_W)
D_IN = 9 * GROUP_W + 2 * GLA_HEADS * GLA_DK + 2 * GLA_RANK

kernel_name = 'hybrid_pool_gla_diff_nat_hmoe_dit_step'


def rmsnorm(x, g):
    x32 = x.astype(jnp.float32)
    y = x32 * lax.rsqrt(jnp.mean(x32 * x32, axis=-1, keepdims=True) + EPS)
    return (y * g.astype(jnp.float32)).astype(x.dtype)


def head_rms(o, g):
    return o * lax.rsqrt(jnp.mean(o * o, axis=-1, keepdims=True) + EPS) * g.astype(jnp.float32)


def split_in(u):
    cuts = [int(i) for i in np.cumsum(IN_SIZES)[:-1]]
    return jnp.split(u, cuts, axis=-1)


def mix_inputs(x, g_norm, shift, scale, w_in_l):
    h = rmsnorm(x, g_norm) * (1 + scale) + shift
    return split_in((h @ w_in_l).astype(jnp.float32))


def pool_mixer(u, w, s):
    B, L, _ = u.shape
    cs = jnp.concatenate([jnp.zeros((B, 1, GROUP_W), u.dtype), jnp.cumsum(u, axis=1)], axis=1)
    t = jnp.arange(L)
    parts = []
    for i, win in enumerate(POOL_WINDOWS):
        lo = jnp.clip(t - win // 2, 0, L)
        hi = jnp.clip(t + win // 2, 0, L)
        cg = cs[..., i * POOL_GW:(i + 1) * POOL_GW]
        mean = (cg[:, hi] - cg[:, lo]) / (hi - lo).astype(u.dtype)[None, :, None]
        parts.append(mean - u[..., i * POOL_GW:(i + 1) * POOL_GW])
    p = jnp.stack(parts, axis=2)
    y = jnp.einsum('blgc,gcd->blgd', p, w.astype(u.dtype)).reshape(B, L, GROUP_W)
    return y * s.astype(u.dtype)


def gla_scan(q, k, v, la, s0):
    B, L, H, _ = q.shape
    C = GLA_CHUNK
    n = L // C
    mask = jnp.tril(jnp.ones((C, C), dtype=bool))

    def chunks(a):
        return jnp.moveaxis(a.reshape(B, n, C, H, a.shape[-1]), 1, 0)

    def step(S, inp):
        qc, kc, vc, lac = inp
        b = jnp.cumsum(lac, axis=1)
        ref = b[:, C // 2:C // 2 + 1]
        a = jnp.einsum('bihd,bjhd->bhij', qc * jnp.exp(b - ref), kc * jnp.exp(ref - b))
        a = jnp.where(mask, a, 0.0)
        o = (jnp.einsum('bhij,bjhv->bihv', a, vc)
             + jnp.einsum('bihd,bhdv->bihv', qc * jnp.exp(b), S))
        bl = b[:, -1]
        S = (jnp.exp(bl)[..., None] * S
             + jnp.einsum('bjhd,bjhv->bhdv', kc * jnp.exp(bl[:, None] - b), vc))
        return S, o

    S, o = lax.scan(step, s0, (chunks(q), chunks(k), chunks(v), chunks(la)))
    return jnp.moveaxis(o, 0, 1).reshape(B, L, H, v.shape[-1]), S


def gla_mixer(q_in, k_in, v_in, g_in, gd_in, w_up, b_up, g_norm, s0_f, s0_b):
    B, L, _ = q_in.shape
    q = q_in.reshape(B, L, GLA_HEADS, GLA_DK) * GLA_DK ** -0.5
    k = k_in.reshape(B, L, GLA_HEADS, GLA_DK)
    v = v_in.reshape(B, L, GLA_HEADS, GLA_DV)
    gd = gd_in.reshape(B, L, 2, GLA_RANK)
    logit = jnp.einsum('blzr,zre->blze', gd, w_up.astype(jnp.float32)) + b_up.astype(jnp.float32)
    la = (jax.nn.log_sigmoid(logit) / GLA_GATE_NORM).reshape(B, L, 2, GLA_HEADS, GLA_DK)
    o_f, s_f = gla_scan(q, k, v, la[:, :, 0], s0_f)
    rev = lambda a: jnp.flip(a, axis=1)
    o_b, s_b = gla_scan(rev(q), rev(k), rev(v), rev(la[:, :, 1]), s0_b)
    o = head_rms(o_f + rev(o_b), g_norm).reshape(B, L, GROUP_W) * jax.nn.silu(g_in)
    return o, jnp.stack([s_f, s_b], axis=1)


def axial_rope(x):
    L = x.shape[1]
    t = jnp.arange(L)
    row = (t // GRID_W).astype(jnp.float32)
    col = (t % GRID_W).astype(jnp.float32)
    half = DIFF_HD // 2
    nf = half // 2
    inv = ROPE_BASE ** (-jnp.arange(nf, dtype=jnp.float32) / nf)

    def rot(xh, pos):
        ang = pos[:, None] * inv
        cos = jnp.cos(ang)[None, :, None, None, :]
        sin = jnp.sin(ang)[None, :, None, None, :]
        x1, x2 = xh[..., :nf], xh[..., nf:]
        return jnp.concatenate([x1 * cos - x2 * sin, x2 * cos + x1 * sin], axis=-1)

    return jnp.concatenate([rot(x[..., :half], row), rot(x[..., half:], col)], axis=-1)


def diff_attention(q, k, v, lam, lam_init, g_norm):
    B, Lq = q.shape[:2]
    nb = Lq // Q_BLOCK
    qb = jnp.moveaxis(q.reshape(B, nb, Q_BLOCK, DIFF_HEADS, 2, DIFF_HD), 1, 0)

    def blk(qi):
        s = jnp.einsum('bqhcd,bkhcd->bchqk', qi, k) * DIFF_HD ** -0.5
        p = jax.nn.softmax(s, axis=-1)
        a = p[:, 0] - lam * p[:, 1]
        return jnp.einsum('bhqk,bkhv->bqhv', a, v)

    o = jnp.moveaxis(lax.map(blk, qb), 0, 1).reshape(B, Lq, DIFF_HEADS, DIFF_VD)
    return (head_rms(o, g_norm) * (1.0 - lam_init)).reshape(B, Lq, GROUP_W)


def dense_attention(q, k, v):
    B, Lq, H, D = q.shape
    nb = Lq // Q_BLOCK
    qb = jnp.moveaxis(q.reshape(B, nb, Q_BLOCK, H, D), 1, 0)

    def blk(qi):
        p = jax.nn.softmax(jnp.einsum('bqhd,bkhd->bhqk', qi, k) * D ** -0.5, axis=-1)
        return jnp.einsum('bhqk,bkhd->bqhd', p, v)

    return jnp.moveaxis(lax.map(blk, qb), 0, 1).reshape(B, Lq, H * D)


def nat_latent(q, k, v, kc, vc, rpb):
    B, L, H, D = q.shape
    rows = L // GRID_W
    wh = min(NAT_WH, rows)
    qg = q.reshape(B, rows, GRID_W, H, D)
    kg = k.reshape(B, rows, GRID_W, H, D)
    vg = v.reshape(B, rows, GRID_W, H, D)
    colv = np.arange(GRID_W)
    cstart = np.clip(colv - NAT_WW // 2, 0, GRID_W - NAT_WW)
    col_mask = (colv[None, :] >= cstart[:, None]) & (colv[None, :] < cstart[:, None] + NAT_WW)
    col_idx = np.clip(colv[None, :] - colv[:, None] + NAT_WW - 1, 0, 2 * NAT_WW - 2)
    rpb_c = rpb.astype(jnp.float32)[:, :, col_idx]
    scale = D ** -0.5
    nkey = wh * GRID_W

    def row(r):
        rs = jnp.clip(r - wh // 2, 0, rows - wh)
        kr = lax.dynamic_slice_in_dim(kg, rs, wh, axis=1)
        vr = lax.dynamic_slice_in_dim(vg, rs, wh, axis=1)
        qr = lax.dynamic_index_in_dim(qg, r, axis=1, keepdims=False)
        row_idx = rs + jnp.arange(wh) - r + (NAT_WH - 1)
        bias = jnp.take(rpb_c, row_idx, axis=1).transpose(0, 2, 1, 3)
        s = jnp.einsum('bqhd,bwkhd->bhqwk', qr, kr) * scale + bias[None]
        s = jnp.where(col_mask[None, None, :, None, :], s, -jnp.inf).reshape(B, H, GRID_W, nkey)
        sc = jnp.einsum('bqhd,bkhd->bhqk', qr, kc) * scale
        p = jax.nn.softmax(jnp.concatenate([s, sc], axis=-1), axis=-1)
        return (jnp.einsum('bhqk,bkhd->bqhd', p[..., :nkey], vr.reshape(B, nkey, H, D))
                + jnp.einsum('bhqk,bkhd->bqhd', p[..., nkey:], vc))

    o = lax.map(row, jnp.arange(rows))
    return jnp.moveaxis(o, 0, 1).reshape(B, L, H * D)


def hier_moe(h, rgw, rgb, rew, reb, wg, wu, wd):
    N = h.shape[0]
    lg = (h @ rgw).astype(jnp.float32) + rgb.astype(jnp.float32)
    pg = jax.nn.softmax(lg, axis=-1)
    gi = jnp.argmax(lg, axis=-1)
    p_top = jnp.max(pg, axis=-1, keepdims=True)
    le = ((h @ rew).astype(jnp.float32) + reb.astype(jnp.float32)).reshape(N, MOE_GROUPS, MOE_PER_GROUP)
    le_g = jnp.einsum('nge,ng->ne', le, jax.nn.one_hot(gi, MOE_GROUPS, dtype=jnp.float32))
    top_v, top_i = lax.top_k(le_g, MOE_TOPK)
    w = jax.nn.softmax(top_v, axis=-1) * p_top
    eid = gi[:, None] * MOE_PER_GROUP + top_i
    comb = jnp.sum(jax.nn.one_hot(eid, MOE_EXPERTS, dtype=jnp.float32) * w[..., None], axis=1)
    y = jnp.zeros(h.shape, jnp.float32)
    for e in range(MOE_EXPERTS):
        a = jax.nn.silu(h @ wg[e]) * (h @ wu[e])
        y = y + comb[:, e:e + 1] * (a @ wd[e]).astype(jnp.float32)
    return y.astype(h.dtype)


def moe_sublayer(x, shift, scale, gate, g_norm, rgw, rgb, rew, reb, wg, wu, wd):
    B, L, D = x.shape
    h = rmsnorm(x, g_norm) * (1 + scale) + shift
    y = hier_moe(h.reshape(B * L, D), rgw, rgb, rew, reb, wg, wu, wd).reshape(B, L, D)
    return x + gate * y


def setup_inputs(seed: int = 0) -> dict:
    key = jax.random.key(seed)
    ks = jax.random.split(key, 31)
    f32 = jnp.float32

    def nrm(k, shape, s):
        return jax.random.normal(k, shape, f32) * s

    D = D_MODEL
    return {
        'x_prompt': nrm(ks[0], (BATCH, SEQ, D), 1.0),
        'x_sample': nrm(ks[1], (DEC_BATCH, DEC_SEQ, D), 1.0),
        'cache_diff_k': nrm(ks[2], (DEC_BATCH, DEPTH, PAST_LEN, DIFF_HEADS, 2 * DIFF_HD), 1.0),
        'cache_diff_v': nrm(ks[3], (DEC_BATCH, DEPTH, PAST_LEN, DIFF_HEADS, DIFF_VD), 1.0),
        'cache_nat_k': nrm(ks[4], (DEC_BATCH, DEPTH, PAST_LEN, NAT_HEADS, NAT_HD), 1.0),
        'cache_nat_v': nrm(ks[5], (DEC_BATCH, DEPTH, PAST_LEN, NAT_HEADS, NAT_HD), 1.0),
        'state_gla': nrm(ks[6], (DEC_BATCH, DEPTH, 2, GLA_HEADS, GLA_DK, GLA_DV), 0.5),
        'c': nrm(ks[7], (DEC_BATCH, D), 1.0),
        'c_ctx': nrm(ks[8], (D,), 1.0),
        'w_ada': nrm(ks[9], (DEPTH, D, 6 * D), 0.5 * D ** -0.5),
        'b_ada': nrm(ks[10], (DEPTH, 6 * D), 0.01),
        'norm1': 1.0 + nrm(ks[11], (DEPTH, D), 0.1),
        'w_in': nrm(ks[12], (DEPTH, D, D_IN), D ** -0.5),
        'pool_w': nrm(ks[13], (DEPTH, 4, POOL_GW, POOL_GW), POOL_GW ** -0.5),
        'pool_scale': 1.0 + nrm(ks[14], (DEPTH, GROUP_W), 0.1),
        'gla_w_up': nrm(ks[15], (DEPTH, 2, GLA_RANK, GLA_HEADS * GLA_DK), GLA_RANK ** -0.5),
        'gla_b_up': nrm(ks[16], (DEPTH, 2, GLA_HEADS * GLA_DK), 0.5),
        'gla_norm': 1.0 + nrm(ks[17], (DEPTH, GLA_DV), 0.1),
        'diff_lambda': nrm(ks[18], (DEPTH, 4, DIFF_HD), 0.1),
        'diff_norm': 1.0 + nrm(ks[19], (DEPTH, DIFF_VD), 0.1),
        'nat_rpb': nrm(ks[20], (DEPTH, NAT_HEADS, 2 * NAT_WH - 1, 2 * NAT_WW - 1), 0.1),
        'w_out': nrm(ks[21], (DEPTH, D_MIX, D), D_MIX ** -0.5),
        'norm2': 1.0 + nrm(ks[22], (DEPTH, D), 0.1),
        'router_group_w': nrm(ks[23], (DEPTH, D, MOE_GROUPS), D ** -0.5),
        'router_group_b': nrm(ks[24], (DEPTH, MOE_GROUPS), 0.01),
        'router_expert_w': nrm(ks[25], (DEPTH, D, MOE_EXPERTS), D ** -0.5),
        'router_expert_b': nrm(ks[26], (DEPTH, MOE_EXPERTS), 0.01),
        'expert_w_gate': nrm(ks[27], (DEPTH, MOE_EXPERTS, D, MOE_FF), D ** -0.5),
        'expert_w_up': nrm(ks[28], (DEPTH, MOE_EXPERTS, D, MOE_FF), D ** -0.5),
        'expert_w_down': nrm(ks[29], (DEPTH, MOE_EXPERTS, MOE_FF, D), MOE_FF ** -0.5),
        'norm_final': 1.0 + nrm(ks[30], (D,), 0.1),
    }


def reference(x_prompt, x_sample, cache_diff_k, cache_diff_v, cache_nat_k, cache_nat_v, state_gla,
              c, c_ctx, w_ada, b_ada, norm1, w_in, pool_w, pool_scale, gla_w_up, gla_b_up, gla_norm,
              diff_lambda, diff_norm, nat_rpb, w_out, norm2, router_group_w, router_group_b,
              router_expert_w, router_expert_b, expert_w_gate, expert_w_up, expert_w_down, norm_final):
    f32 = jnp.float32
    xp, xs = x_prompt, x_sample
    Bp, Lp = xp.shape[:2]
    Bs, Ls = xs.shape[:2]
    Lc = cache_diff_k.shape[2]
    new_dk, new_dv, new_nk, new_nv, new_gs = [], [], [], [], []
    for l in range(DEPTH):
        lam_init = 0.8 - 0.6 * math.exp(-0.3 * l)
        lv = diff_lambda[l].astype(f32)
        lam = jnp.exp(jnp.sum(lv[0] * lv[1])) - jnp.exp(jnp.sum(lv[2] * lv[3])) + lam_init
        moe_w = (router_group_w[l], router_group_b[l], router_expert_w[l], router_expert_b[l],
                 expert_w_gate[l], expert_w_up[l], expert_w_down[l])

        mod_ctx = (jax.nn.silu(c_ctx) @ w_ada[l] + b_ada[l])[None, None, :]
        sh1, sc1, g1, sh2, sc2, g2 = jnp.split(mod_ctx, 6, axis=-1)
        pin, gq, gk, gv, gg, gd, dq, dk, dv, nq, nk, nv = mix_inputs(xp, norm1[l], sh1, sc1, w_in[l])
        zero = jnp.zeros((Bp, GLA_HEADS, GLA_DK, GLA_DV), f32)
        o_gla, gs = gla_mixer(gq, gk, gv, gg, gd, gla_w_up[l], gla_b_up[l], gla_norm[l], zero, zero)
        o_diff = diff_attention(dq.reshape(Bp, Lp, DIFF_HEADS, 2, DIFF_HD),
                                dk.reshape(Bp, Lp, DIFF_HEADS, 2, DIFF_HD),
                                dv.reshape(Bp, Lp, DIFF_HEADS, DIFF_VD), lam, lam_init, diff_norm[l])
        nk4 = nk.reshape(Bp, Lp, NAT_HEADS, NAT_HD)
        nv4 = nv.reshape(Bp, Lp, NAT_HEADS, NAT_HD)
        o_nat = dense_attention(nq.reshape(Bp, Lp, NAT_HEADS, NAT_HD), nk4, nv4)
        o = jnp.concatenate([pool_mixer(pin, pool_w[l], pool_scale[l]), o_gla, o_diff, o_nat],
                            axis=-1).astype(xp.dtype)
        xp = xp + g1 * (o @ w_out[l])
        xp = moe_sublayer(xp, sh2, sc2, g2, norm2[l], *moe_w)
        new_dk.append(dk.reshape(Bp, Lp, DIFF_HEADS, 2 * DIFF_HD).astype(xp.dtype))
        new_dv.append(dv.reshape(Bp, Lp, DIFF_HEADS, DIFF_VD).astype(xp.dtype))
        new_nk.append(nk4.astype(xp.dtype))
        new_nv.append(nv4.astype(xp.dtype))
        new_gs.append(gs.astype(xp.dtype))

        mod_lat = (jax.nn.silu(c) @ w_ada[l] + b_ada[l])[:, None, :]
        sh1, sc1, g1, sh2, sc2, g2 = jnp.split(mod_lat, 6, axis=-1)
        pin, gq, gk, gv, gg, gd, dq, dk, dv, nq, nk, nv = mix_inputs(xs, norm1[l], sh1, sc1, w_in[l])
        st = state_gla[:, l].astype(f32)
        o_gla, _ = gla_mixer(gq, gk, gv, gg, gd, gla_w_up[l], gla_b_up[l], gla_norm[l], st[:, 0], st[:, 1])
        dq5 = axial_rope(dq.reshape(Bs, Ls, DIFF_HEADS, 2, DIFF_HD))
        dk5 = axial_rope(dk.reshape(Bs, Ls, DIFF_HEADS, 2, DIFF_HD))
        ck = cache_diff_k[:, l].astype(f32).reshape(Bs, Lc, DIFF_HEADS, 2, DIFF_HD)
        cv = cache_diff_v[:, l].astype(f32)
        o_diff = diff_attention(dq5, jnp.concatenate([dk5, ck], axis=1),
                                jnp.concatenate([dv.reshape(Bs, Ls, DIFF_HEADS, DIFF_VD), cv], axis=1),
                                lam, lam_init, diff_norm[l])
        o_nat = nat_latent(nq.reshape(Bs, Ls, NAT_HEADS, NAT_HD), nk.reshape(Bs, Ls, NAT_HEADS, NAT_HD),
                           nv.reshape(Bs, Ls, NAT_HEADS, NAT_HD), cache_nat_k[:, l].astype(f32),
                           cache_nat_v[:, l].astype(f32), nat_rpb[l])
        o = jnp.concatenate([pool_mixer(pin, pool_w[l], pool_scale[l]), o_gla, o_diff, o_nat],
                            axis=-1).astype(xs.dtype)
        xs = xs + g1 * (o @ w_out[l])
        xs = moe_sublayer(xs, sh2, sc2, g2, norm2[l], *moe_w)

    y_prompt = rmsnorm(xp, norm_final)
    y_sample = rmsnorm(xs, norm_final)
    new_diff_k = jnp.stack(new_dk, axis=1)
    new_diff_v = jnp.stack(new_dv, axis=1)
    new_nat_k = jnp.stack(new_nk, axis=1)
    new_nat_v = jnp.stack(new_nv, axis=1)
    new_state_gla = jnp.stack(new_gs, axis=1)
    return (y_prompt, y_sample, new_diff_k, new_diff_v, new_nat_k, new_nat_v, new_state_gla)
```

```python
import functools
import math

import jax
import jax.numpy as jnp
import numpy as np
from jax import lax
from jax.experimental import pallas as pl
from jax.experimental.pallas import tpu as pltpu

f32 = jnp.float32
bf16 = jnp.bfloat16

D_MODEL = 4096
BATCH = 32
SEQ = 256
DEPTH = 2
DEC_BATCH = 4
DEC_SEQ = 4096
PAST_LEN = 512
GRID_W = 64
GROUP_W = D_MODEL // 4
POOL_WINDOWS = (2, 4, 8, 16)
POOL_GW = GROUP_W // 4
GLA_HEADS = 8
GLA_DV = GROUP_W // GLA_HEADS
GLA_DK = GLA_DV // 2
GLA_RANK = 16
GLA_GATE_NORM = 16.0
GLA_CHUNK = 64
DIFF_HEADS = 8
DIFF_VD = GROUP_W // DIFF_HEADS
DIFF_HD = DIFF_VD // 2
ROPE_BASE = 10000.0
NAT_HEADS = 8
NAT_HD = GROUP_W // NAT_HEADS
NAT_WH = 8
NAT_WW = 16
MOE_GROUPS = 4
MOE_PER_GROUP = 8
MOE_EXPERTS = MOE_GROUPS * MOE_PER_GROUP
MOE_TOPK = 2
MOE_FF = 512
Q_BLOCK = 128
EPS = 1e-6

N_PROMPT = BATCH * SEQ
N_SAMPLE = DEC_BATCH * DEC_SEQ
N_TOK = N_PROMPT + N_SAMPLE
LANES = 128
MOD_ROWS = 8

C_POOL, C_GQ, C_GK, C_GV, C_GG = 0, 1024, 1536, 2048, 3072
C_DQ, C_DK, C_DV, C_NQ, C_NK, C_NV, C_GD = 4096, 5120, 6144, 7168, 8192, 9216, 10240
D_IN_PAD = 10752
ORIG_GD = 3 * GROUP_W + 2 * GLA_HEADS * GLA_DK

TM = 512
TN_IN = 768
TN_OUT = 1024
TM_E = 256
N_ASSIGN = N_TOK * MOE_TOPK
N_ETILES = N_ASSIGN // TM_E + MOE_EXPERTS
VMEM_LIMIT = 56 * 1024 * 1024


def _cparams(sem):
    return pltpu.CompilerParams(dimension_semantics=sem, vmem_limit_bytes=VMEM_LIMIT)


def _mod_row(i):
    npt = N_PROMPT // TM
    return jnp.where(i < npt, 0, 1 + (i - npt) // (DEC_SEQ // TM))


def _ada_kernel(c_ref, w_ref, b_ref, o_ref):
    @pl.when(pl.program_id(2) == 0)
    def _():
        o_ref[0] = jnp.broadcast_to(b_ref[0], o_ref.shape[1:])

    c = c_ref[...]
    a = (c * jax.nn.sigmoid(c)).astype(bf16)
    o_ref[0] += jnp.dot(a, w_ref[0].astype(bf16), preferred_element_type=f32)


def ada_modulation(cvec, w_ada, b_ada):
    tk, tn = 2048, 1024
    n6 = 6 * D_MODEL
    return pl.pallas_call(
        _ada_kernel,
        grid=(DEPTH, n6 // tn, D_MODEL // tk),
        in_specs=[pl.BlockSpec((MOD_ROWS, tk), lambda l, j, k: (0, k)),
                  pl.BlockSpec((1, tk, tn), lambda l, j, k: (l, k, j)),
                  pl.BlockSpec((1, 1, tn), lambda l, j, k: (l, 0, j))],
        out_specs=pl.BlockSpec((1, MOD_ROWS, tn), lambda l, j, k: (l, 0, j)),
        out_shape=jax.ShapeDtypeStruct((DEPTH, MOD_ROWS, n6), f32),
        compiler_params=_cparams(("parallel", "parallel", "arbitrary")),
        name="ada_modulation",
    )(cvec, w_ada, b_ada.reshape(DEPTH, 1, n6))


def _modulated_norm(x, g, shift, scale):
    y = x * lax.rsqrt(jnp.mean(x * x, axis=-1, keepdims=True) + EPS) * g
    return y * (1.0 + scale) + shift


def _win_kernel(x_ref, g_ref, mod_ref, w_ref, o_ref, h_ref):
    @pl.when(pl.program_id(1) == 0)
    def _():
        h = _modulated_norm(x_ref[...], g_ref[...], mod_ref[0, 0:1, :], mod_ref[0, 1:2, :])
        h_ref[...] = h.astype(bf16)

    o_ref[...] = jnp.dot(h_ref[...], w_ref[...], preferred_element_type=f32)


def input_projection(x, g_norm, mod, w):
    return pl.pallas_call(
        _win_kernel,
        grid=(N_TOK // TM, D_IN_PAD // TN_IN),
        in_specs=[pl.BlockSpec((TM, D_MODEL), lambda i, j: (i, 0)),
                  pl.BlockSpec((1, D_MODEL), lambda i, j: (0, 0)),
                  pl.BlockSpec((1, 6, D_MODEL), lambda i, j: (_mod_row(i), 0, 0)),
                  pl.BlockSpec((D_MODEL, TN_IN), lambda i, j: (0, j))],
        out_specs=pl.BlockSpec((TM, TN_IN), lambda i, j: (i, j)),
        out_shape=jax.ShapeDtypeStruct((N_TOK, D_IN_PAD), f32),
        scratch_shapes=[pltpu.VMEM((TM, D_MODEL), bf16)],
        compiler_params=_cparams(("parallel", "arbitrary")),
        name="input_projection",
    )(x, g_norm.reshape(1, D_MODEL), mod, w)


def _wout_kernel(a0, a1, a2, a3, w0, w1, w2, w3, x_ref, mod_ref, o_ref):
    acc = jnp.dot(a0[...], w0[...], preferred_element_type=f32)
    acc += jnp.dot(a1[...], w1[...], preferred_element_type=f32)
    acc += jnp.dot(a2[...], w2[...], preferred_element_type=f32)
    acc += jnp.dot(a3[...], w3[...], preferred_element_type=f32)
    o_ref[...] = x_ref[...] + mod_ref[0, 2:3, :] * acc


def output_projection(parts, w, x, mod):
    a_specs = [pl.BlockSpec((TM, GROUP_W), lambda i, j: (i, 0)) for _ in range(4)]
    w_specs = [pl.BlockSpec((GROUP_W, TN_OUT), functools.partial(lambda i, j, c: (c, j), c=c))
               for c in range(4)]
    return pl.pallas_call(
        _wout_kernel,
        grid=(N_TOK // TM, D_MODEL // TN_OUT),
        in_specs=a_specs + w_specs + [
            pl.BlockSpec((TM, TN_OUT), lambda i, j: (i, j)),
            pl.BlockSpec((1, 6, TN_OUT), lambda i, j: (_mod_row(i), 0, j))],
        out_specs=pl.BlockSpec((TM, TN_OUT), lambda i, j: (i, j)),
        out_shape=jax.ShapeDtypeStruct((N_TOK, D_MODEL), f32),
        compiler_params=_cparams(("parallel", "arbitrary")),
        name="output_projection",
    )(*parts, w, w, w, w, x, mod)


def _split_bf16(a):
    hi = a.astype(bf16)
    lo = (a - hi.astype(f32)).astype(bf16)
    return hi, lo


def _router_kernel(x_ref, g_ref, mod_ref, wr_ref, br_ref, h_ref, r_ref):
    h = _modulated_norm(x_ref[...], g_ref[...], mod_ref[0, 3:4, :], mod_ref[0, 4:5, :])
    h_ref[...] = h.astype(bf16)
    h_hi, h_lo = _split_bf16(h)
    w_hi, w_lo = _split_bf16(wr_ref[...])
    lg = (jnp.dot(h_hi, w_hi, preferred_element_type=f32)
          + jnp.dot(h_lo, w_hi, preferred_element_type=f32)
          + jnp.dot(h_hi, w_lo, preferred_element_type=f32)) + br_ref[...]
    lane = lax.broadcasted_iota(jnp.int32, lg.shape, 1).astype(f32)
    ninf = -jnp.inf

    def first_max(v):
        m = jnp.max(v, axis=-1, keepdims=True)
        idx = jnp.min(jnp.where(v == m, lane, float(LANES)), axis=-1, keepdims=True)
        return m, idx

    gmask = lane < MOE_GROUPS
    mg, gi = first_max(jnp.where(gmask, lg, ninf))
    p_top = 1.0 / jnp.sum(jnp.where(gmask, jnp.exp(lg - mg), 0.0), axis=-1, keepdims=True)
    lo = MOE_GROUPS + gi * MOE_PER_GROUP
    le = jnp.where((lane >= lo) & (lane < lo + MOE_PER_GROUP), lg, ninf)
    v1, i1 = first_max(le)
    v2, i2 = first_max(jnp.where(lane == i1, ninf, le))
    t = jnp.exp(v2 - v1)
    w1 = p_top / (1.0 + t)
    w2 = w1 * t
    r_ref[...] = jnp.where(lane == 0.0, i1 - MOE_GROUPS,
                           jnp.where(lane == 1.0, i2 - MOE_GROUPS,
                                     jnp.where(lane == 2.0, w1, jnp.where(lane == 3.0, w2, 0.0))))


def moe_router(x, g_norm, mod, wr, br):
    tm = 256
    return pl.pallas_call(
        _router_kernel,
        grid=(N_TOK // tm,),
        in_specs=[pl.BlockSpec((tm, D_MODEL), lambda i: (i, 0)),
                  pl.BlockSpec((1, D_MODEL), lambda i: (0, 0)),
                  pl.BlockSpec((1, 6, D_MODEL), lambda i: (_mod_row(i // (TM // tm)), 0, 0)),
                  pl.BlockSpec((D_MODEL, LANES), lambda i: (0, 0)),
                  pl.BlockSpec((1, LANES), lambda i: (0, 0))],
        out_specs=[pl.BlockSpec((tm, D_MODEL), lambda i: (i, 0)),
                   pl.BlockSpec((tm, LANES), lambda i: (i, 0))],
        out_shape=[jax.ShapeDtypeStruct((N_TOK, D_MODEL), bf16),
                   jax.ShapeDtypeStruct((N_TOK, LANES), f32)],
        compiler_params=_cparams(("parallel",)),
        name="moe_router",
    )(x, g_norm.reshape(1, D_MODEL), mod, wr, br)


def _ffn_kernel(te_ref, nu_ref, x_ref, wg_ref, wu_ref, wd_ref, o_ref):
    @pl.when(pl.program_id(0) < nu_ref[0])
    def _():
        x = x_ref[...]
        a = jnp.dot(x, wg_ref[0], preferred_element_type=f32)
        u = jnp.dot(x, wu_ref[0], preferred_element_type=f32)
        mid = (a * jax.nn.sigmoid(a) * u).astype(bf16)
        o_ref[...] = jnp.dot(mid, wd_ref[0], preferred_element_type=f32)


def expert_ffn(tile_expert, n_used, xs, wg, wu, wd):
    row_blk = lambda t, te, nu: (jnp.minimum(t, nu[0] - 1), 0)
    return pl.pallas_call(
        _ffn_kernel,
        grid_spec=pltpu.PrefetchScalarGridSpec(
            num_scalar_prefetch=2,
            grid=(N_ETILES,),
            in_specs=[pl.BlockSpec((TM_E, D_MODEL), row_blk),
                      pl.BlockSpec((1, D_MODEL, MOE_FF), lambda t, te, nu: (te[t], 0, 0)),
                      pl.BlockSpec((1, D_MODEL, MOE_FF), lambda t, te, nu: (te[t], 0, 0)),
                      pl.BlockSpec((1, MOE_FF, D_MODEL), lambda t, te, nu: (te[t], 0, 0))],
            out_specs=pl.BlockSpec((TM_E, D_MODEL), row_blk)),
        out_shape=jax.ShapeDtypeStruct((N_ETILES * TM_E, D_MODEL), f32),
        compiler_params=_cparams(("arbitrary",)),
        name="expert_ffn",
    )(tile_expert, n_used, xs, wg, wu, wd)


def route_layout(route):
    e = route[:, :MOE_TOPK].astype(jnp.int32).reshape(-1)
    onehot = (e[:, None] == jnp.arange(MOE_EXPERTS, dtype=jnp.int32)[None, :]).astype(jnp.int32)
    csum = jnp.cumsum(onehot, axis=0)
    rank = jnp.sum(onehot * (csum - 1), axis=1)
    cnt = csum[-1]
    tiles_e = (cnt + TM_E - 1) // TM_E
    tile_end = jnp.cumsum(tiles_e)
    tile_start = tile_end - tiles_e
    pos = tile_start[e] * TM_E + rank
    slot_token = jnp.zeros((N_ETILES * TM_E,), jnp.int32).at[pos].set(
        jnp.arange(N_ASSIGN, dtype=jnp.int32) // MOE_TOPK)
    n_used = tile_end[-1]
    t = jnp.arange(N_ETILES, dtype=jnp.int32)
    tile_expert = jnp.searchsorted(tile_end, jnp.minimum(t, n_used - 1), side="right").astype(jnp.int32)
    tile_expert = jnp.minimum(tile_expert, MOE_EXPERTS - 1)
    return pos.reshape(N_TOK, MOE_TOPK), slot_token, tile_expert, n_used.reshape(1).astype(jnp.int32)


def _combine_kernel(x_ref, y0_ref, y1_ref, r_ref, mod_ref, o_ref):
    y = r_ref[:, 2:3] * y0_ref[...] + r_ref[:, 3:4] * y1_ref[...]
    o_ref[...] = x_ref[...] + mod_ref[0, 5:6, :] * y


def _combine_norm_kernel(x_ref, y0_ref, y1_ref, r_ref, mod_ref, g_ref, o_ref):
    y = r_ref[:, 2:3] * y0_ref[...] + r_ref[:, 3:4] * y1_ref[...]
    x = x_ref[...] + mod_ref[0, 5:6, :] * y
    o_ref[...] = x * lax.rsqrt(jnp.mean(x * x, axis=-1, keepdims=True) + EPS) * g_ref[...]


def moe_combine(x, y0, y1, route, mod, g_final=None):
    tm = 256
    row = pl.BlockSpec((tm, D_MODEL), lambda i: (i, 0))
    in_specs = [row, row, row, pl.BlockSpec((tm, LANES), lambda i: (i, 0)),
                pl.BlockSpec((1, 6, D_MODEL), lambda i: (_mod_row(i // (TM // tm)), 0, 0))]
    args = [x, y0, y1, route, mod]
    body = _combine_kernel
    if g_final is not None:
        in_specs.append(pl.BlockSpec((1, D_MODEL), lambda i: (0, 0)))
        args.append(g_final.reshape(1, D_MODEL))
        body = _combine_norm_kernel
    return pl.pallas_call(
        body,
        grid=(N_TOK // tm,),
        in_specs=in_specs,
        out_specs=row,
        out_shape=jax.ShapeDtypeStruct((N_TOK, D_MODEL), f32),
        compiler_params=_cparams(("parallel",)),
        name="moe_combine",
    )(*args)


def _head_rms(o, g):
    return o * lax.rsqrt(jnp.mean(o * o, axis=-1, keepdims=True) + EPS) * g.astype(f32)


def _pool_mixer(u, w, s):
    B, L, _ = u.shape
    cs = jnp.concatenate([jnp.zeros((B, 1, GROUP_W), u.dtype), jnp.cumsum(u, axis=1)], axis=1)
    t = jnp.arange(L)
    parts = []
    for i, win in enumerate(POOL_WINDOWS):
        lo = jnp.clip(t - win // 2, 0, L)
        hi = jnp.clip(t + win // 2, 0, L)
        cg = cs[..., i * POOL_GW:(i + 1) * POOL_GW]
        mean = (cg[:, hi] - cg[:, lo]) / (hi - lo).astype(u.dtype)[None, :, None]
        parts.append(mean - u[..., i * POOL_GW:(i + 1) * POOL_GW])
    p = jnp.stack(parts, axis=2)
    y = jnp.einsum('blgc,gcd->blgd', p, w.astype(u.dtype)).reshape(B, L, GROUP_W)
    return y * s.astype(u.dtype)


def _gla_scan(q, k, v, la, s0):
    B, L, H, _ = q.shape
    C = GLA_CHUNK
    n = L // C
    mask = jnp.tril(jnp.ones((C, C), dtype=bool))

    def chunks(a):
        return jnp.moveaxis(a.reshape(B, n, C, H, a.shape[-1]), 1, 0)

    def step(S, inp):
        qc, kc, vc, lac = inp
        b = jnp.cumsum(lac, axis=1)
        ref = b[:, C // 2:C // 2 + 1]
        a = jnp.einsum('bihd,bjhd->bhij', qc * jnp.exp(b - ref), kc * jnp.exp(ref - b))
        a = jnp.where(mask, a, 0.0)
        o = (jnp.einsum('bhij,bjhv->bihv', a, vc)
             + jnp.einsum('bihd,bhdv->bihv', qc * jnp.exp(b), S))
        bl = b[:, -1]
        S = (jnp.exp(bl)[..., None] * S
             + jnp.einsum('bjhd,bjhv->bhdv', kc * jnp.exp(bl[:, None] - b), vc))
        return S, o

    S, o = lax.scan(step, s0, (chunks(q), chunks(k), chunks(v), chunks(la)))
    return jnp.moveaxis(o, 0, 1).reshape(B, L, H, v.shape[-1]), S


def _gla_mixer(q_in, k_in, v_in, g_in, gd_in, w_up, b_up, g_norm, s0_f, s0_b):
    B, L, _ = q_in.shape
    q = q_in.reshape(B, L, GLA_HEADS, GLA_DK) * GLA_DK ** -0.5
    k = k_in.reshape(B, L, GLA_HEADS, GLA_DK)
    v = v_in.reshape(B, L, GLA_HEADS, GLA_DV)
    gd = gd_in.reshape(B, L, 2, GLA_RANK)
    logit = jnp.einsum('blzr,zre->blze', gd, w_up.astype(f32)) + b_up.astype(f32)
    la = (jax.nn.log_sigmoid(logit) / GLA_GATE_NORM).reshape(B, L, 2, GLA_HEADS, GLA_DK)
    o_f, s_f = _gla_scan(q, k, v, la[:, :, 0], s0_f)
    rev = lambda a: jnp.flip(a, axis=1)
    o_b, s_b = _gla_scan(rev(q), rev(k), rev(v), rev(la[:, :, 1]), s0_b)
    o = _head_rms(o_f + rev(o_b), g_norm).reshape(B, L, GROUP_W) * jax.nn.silu(g_in)
    return o, jnp.stack([s_f, s_b], axis=1)


def _axial_rope(x):
    L = x.shape[1]
    t = jnp.arange(L)
    row = (t // GRID_W).astype(f32)
    col = (t % GRID_W).astype(f32)
    half = DIFF_HD // 2
    nf = half // 2
    inv = ROPE_BASE ** (-jnp.arange(nf, dtype=f32) / nf)

    def rot(xh, pos):
        ang = pos[:, None] * inv
        cos = jnp.cos(ang)[None, :, None, None, :]
        sin = jnp.sin(ang)[None, :, None, None, :]
        x1, x2 = xh[..., :nf], xh[..., nf:]
        return jnp.concatenate([x1 * cos - x2 * sin, x2 * cos + x1 * sin], axis=-1)

    return jnp.concatenate([rot(x[..., :half], row), rot(x[..., half:], col)], axis=-1)


def _diff_attention(q, k, v, lam, lam_init, g_norm):
    B, Lq = q.shape[:2]
    nb = Lq // Q_BLOCK
    qb = jnp.moveaxis(q.reshape(B, nb, Q_BLOCK, DIFF_HEADS, 2, DIFF_HD), 1, 0)

    def blk(qi):
        s = jnp.einsum('bqhcd,bkhcd->bchqk', qi, k) * DIFF_HD ** -0.5
        p = jax.nn.softmax(s, axis=-1)
        a = p[:, 0] - lam * p[:, 1]
        return jnp.einsum('bhqk,bkhv->bqhv', a, v)

    o = jnp.moveaxis(lax.map(blk, qb), 0, 1).reshape(B, Lq, DIFF_HEADS, DIFF_VD)
    return (_head_rms(o, g_norm) * (1.0 - lam_init)).reshape(B, Lq, GROUP_W)


def _dense_attention(q, k, v):
    B, Lq, H, D = q.shape
    nb = Lq // Q_BLOCK
    qb = jnp.moveaxis(q.reshape(B, nb, Q_BLOCK, H, D), 1, 0)

    def blk(qi):
        p = jax.nn.softmax(jnp.einsum('bqhd,bkhd->bhqk', qi, k) * D ** -0.5, axis=-1)
        return jnp.einsum('bhqk,bkhd->bqhd', p, v)

    return jnp.moveaxis(lax.map(blk, qb), 0, 1).reshape(B, Lq, H * D)


def _nat_latent(q, k, v, kc, vc, rpb):
    B, L, H, D = q.shape
    rows = L // GRID_W
    wh = min(NAT_WH, rows)
    qg = q.reshape(B, rows, GRID_W, H, D)
    kg = k.reshape(B, rows, GRID_W, H, D)
    vg = v.reshape(B, rows, GRID_W, H, D)
    colv = np.arange(GRID_W)
    cstart = np.clip(colv - NAT_WW // 2, 0, GRID_W - NAT_WW)
    col_mask = (colv[None, :] >= cstart[:, None]) & (colv[None, :] < cstart[:, None] + NAT_WW)
    col_idx = np.clip(colv[None, :] - colv[:, None] + NAT_WW - 1, 0, 2 * NAT_WW - 2)
    rpb_c = rpb.astype(f32)[:, :, col_idx]
    scale = D ** -0.5
    nkey = wh * GRID_W

    def row(r):
        rs = jnp.clip(r - wh // 2, 0, rows - wh)
        kr = lax.dynamic_slice_in_dim(kg, rs, wh, axis=1)
        vr = lax.dynamic_slice_in_dim(vg, rs, wh, axis=1)
        qr = lax.dynamic_index_in_dim(qg, r, axis=1, keepdims=False)
        row_idx = rs + jnp.arange(wh) - r + (NAT_WH - 1)
        bias = jnp.take(rpb_c, row_idx, axis=1).transpose(0, 2, 1, 3)
        s = jnp.einsum('bqhd,bwkhd->bhqwk', qr, kr) * scale + bias[None]
        s = jnp.where(col_mask[None, None, :, None, :], s, -jnp.inf).reshape(B, H, GRID_W, nkey)
        sc = jnp.einsum('bqhd,bkhd->bhqk', qr, kc) * scale
        p = jax.nn.softmax(jnp.concatenate([s, sc], axis=-1), axis=-1)
        return (jnp.einsum('bhqk,bkhd->bqhd', p[..., :nkey], vr.reshape(B, nkey, H, D))
                + jnp.einsum('bhqk,bkhd->bqhd', p[..., nkey:], vc))

    o = lax.map(row, jnp.arange(rows))
    return jnp.moveaxis(o, 0, 1).reshape(B, L, H * D)


def kernel(x_prompt, x_sample, cache_diff_k, cache_diff_v, cache_nat_k, cache_nat_v, state_gla, c, c_ctx, w_ada, b_ada, norm1, w_in, pool_w, pool_scale, gla_w_up, gla_b_up, gla_norm, diff_lambda, diff_norm, nat_rpb, w_out, norm2, router_group_w, router_group_b, router_expert_w, router_expert_b, expert_w_gate, expert_w_up, expert_w_down, norm_final):
    Bp, Lp, Bs, Ls = BATCH, SEQ, DEC_BATCH, DEC_SEQ
    x = jnp.concatenate([x_prompt.reshape(N_PROMPT, D_MODEL), x_sample.reshape(N_SAMPLE, D_MODEL)], axis=0)
    cvec = jnp.zeros((MOD_ROWS, D_MODEL), f32).at[0].set(c_ctx).at[1:1 + DEC_BATCH].set(c)
    mod_all = ada_modulation(cvec, w_ada, b_ada).reshape(DEPTH, MOD_ROWS, 6, D_MODEL)

    new_dk, new_dv, new_nk, new_nv, new_gs = [], [], [], [], []
    for l in range(DEPTH):
        mod = mod_all[l]
        lam_init = 0.8 - 0.6 * math.exp(-0.3 * l)
        lv = diff_lambda[l].astype(f32)
        lam = jnp.exp(jnp.sum(lv[0] * lv[1])) - jnp.exp(jnp.sum(lv[2] * lv[3])) + lam_init

        w_l = w_in[l]
        w_perm = jnp.concatenate(
            [w_l[:, :ORIG_GD], w_l[:, ORIG_GD + 2 * GLA_RANK:], w_l[:, ORIG_GD:ORIG_GD + 2 * GLA_RANK],
             jnp.zeros((D_MODEL, D_IN_PAD - C_GD - 2 * GLA_RANK), f32)], axis=1).astype(bf16)
        u = input_projection(x, norm1[l], mod, w_perm)

        def cols(rows, c0, width):
            return rows[..., c0:c0 + width]

        up = u[:N_PROMPT].reshape(Bp, Lp, D_IN_PAD)
        us = u[N_PROMPT:].reshape(Bs, Ls, D_IN_PAD)

        zero = jnp.zeros((Bp, GLA_HEADS, GLA_DK, GLA_DV), f32)
        o_gla_p, gs = _gla_mixer(cols(up, C_GQ, 512), cols(up, C_GK, 512), cols(up, C_GV, 1024),
                                 cols(up, C_GG, 1024), cols(up, C_GD, 32), gla_w_up[l], gla_b_up[l],
                                 gla_norm[l], zero, zero)
        dk_p = cols(up, C_DK, 1024)
        dv_p = cols(up, C_DV, 1024)
        o_diff_p = _diff_attention(cols(up, C_DQ, 1024).reshape(Bp, Lp, DIFF_HEADS, 2, DIFF_HD),
                                   dk_p.reshape(Bp, Lp, DIFF_HEADS, 2, DIFF_HD),
                                   dv_p.reshape(Bp, Lp, DIFF_HEADS, DIFF_VD), lam, lam_init, diff_norm[l])
        nk_p = cols(up, C_NK, 1024).reshape(Bp, Lp, NAT_HEADS, NAT_HD)
        nv_p = cols(up, C_NV, 1024).reshape(Bp, Lp, NAT_HEADS, NAT_HD)
        o_nat_p = _dense_attention(cols(up, C_NQ, 1024).reshape(Bp, Lp, NAT_HEADS, NAT_HD), nk_p, nv_p)
        o_pool_p = _pool_mixer(cols(up, C_POOL, 1024), pool_w[l], pool_scale[l])
        new_dk.append(dk_p.reshape(Bp, Lp, DIFF_HEADS, 2 * DIFF_HD))
        new_dv.append(dv_p.reshape(Bp, Lp, DIFF_HEADS, DIFF_VD))
        new_nk.append(nk_p)
        new_nv.append(nv_p)
        new_gs.append(gs)

        st = state_gla[:, l].astype(f32)
        o_gla_s, _ = _gla_mixer(cols(us, C_GQ, 512), cols(us, C_GK, 512), cols(us, C_GV, 1024),
                                cols(us, C_GG, 1024), cols(us, C_GD, 32), gla_w_up[l], gla_b_up[l],
                                gla_norm[l], st[:, 0], st[:, 1])
        dq5 = _axial_rope(cols(us, C_DQ, 1024).reshape(Bs, Ls, DIFF_HEADS, 2, DIFF_HD))
        dk5 = _axial_rope(cols(us, C_DK, 1024).reshape(Bs, Ls, DIFF_HEADS, 2, DIFF_HD))
        ck = cache_diff_k[:, l].astype(f32).reshape(Bs, PAST_LEN, DIFF_HEADS, 2, DIFF_HD)
        cv = cache_diff_v[:, l].astype(f32)
        o_diff_s = _diff_attention(
            dq5, jnp.concatenate([dk5, ck], axis=1),
            jnp.concatenate([cols(us, C_DV, 1024).reshape(Bs, Ls, DIFF_HEADS, DIFF_VD), cv], axis=1),
            lam, lam_init, diff_norm[l])
        o_nat_s = _nat_latent(cols(us, C_NQ, 1024).reshape(Bs, Ls, NAT_HEADS, NAT_HD),
                              cols(us, C_NK, 1024).reshape(Bs, Ls, NAT_HEADS, NAT_HD),
                              cols(us, C_NV, 1024).reshape(Bs, Ls, NAT_HEADS, NAT_HD),
                              cache_nat_k[:, l].astype(f32), cache_nat_v[:, l].astype(f32), nat_rpb[l])
        o_pool_s = _pool_mixer(cols(us, C_POOL, 1024), pool_w[l], pool_scale[l])

        def both(a, b):
            return jnp.concatenate([a.reshape(N_PROMPT, GROUP_W), b.reshape(N_SAMPLE, GROUP_W)],
                                   axis=0).astype(bf16)

        parts = [both(o_pool_p, o_pool_s), both(o_gla_p, o_gla_s), both(o_diff_p, o_diff_s),
                 both(o_nat_p, o_nat_s)]
        x = output_projection(parts, w_out[l].astype(bf16), x, mod)

        wr = jnp.zeros((D_MODEL, LANES), f32)
        wr = wr.at[:, :MOE_GROUPS].set(router_group_w[l]).at[:, MOE_GROUPS:MOE_GROUPS + MOE_EXPERTS].set(
            router_expert_w[l])
        br = jnp.zeros((1, LANES), f32)
        br = br.at[0, :MOE_GROUPS].set(router_group_b[l]).at[0, MOE_GROUPS:MOE_GROUPS + MOE_EXPERTS].set(
            router_expert_b[l])
        h2, route = moe_router(x, norm2[l], mod, wr, br)
        pos, slot_token, tile_expert, n_used = route_layout(route)
        xs = jnp.take(h2, slot_token, axis=0)
        ys = expert_ffn(tile_expert, n_used, xs, expert_w_gate[l].astype(bf16),
                        expert_w_up[l].astype(bf16), expert_w_down[l].astype(bf16))
        y0 = jnp.take(ys, pos[:, 0], axis=0)
        y1 = jnp.take(ys, pos[:, 1], axis=0)
        x = moe_combine(x, y0, y1, route, mod, norm_final if l == DEPTH - 1 else None)

    y_prompt = x[:N_PROMPT].reshape(Bp, Lp, D_MODEL)
    y_sample = x[N_PROMPT:].reshape(Bs, Ls, D_MODEL)
    return (y_prompt, y_sample, jnp.stack(new_dk, axis=1), jnp.stack(new_dv, axis=1),
            jnp.stack(new_nk, axis=1), jnp.stack(new_nv, axis=1), jnp.stack(new_gs, axis=1))
```

```python
import functools
import math

import jax
import jax.numpy as jnp
import numpy as np
from jax import lax
from jax.experimental import pallas as pl
from jax.experimental.pallas import tpu as pltpu

f32 = jnp.float32
bf16 = jnp.bfloat16

D_MODEL = 4096
BATCH = 32
SEQ = 256
DEPTH = 2
DEC_BATCH = 4
DEC_SEQ = 4096
PAST_LEN = 512
GRID_W = 64
GROUP_W = D_MODEL // 4
POOL_WINDOWS = (2, 4, 8, 16)
POOL_GW = GROUP_W // 4
GLA_HEADS = 8
GLA_DV = GROUP_W // GLA_HEADS
GLA_DK = GLA_DV // 2
GLA_RANK = 16
GLA_GATE_NORM = 16.0
GLA_CHUNK = 64
DIFF_HEADS = 8
DIFF_VD = GROUP_W // DIFF_HEADS
DIFF_HD = DIFF_VD // 2
ROPE_BASE = 10000.0
NAT_HEADS = 8
NAT_HD = GROUP_W // NAT_HEADS
NAT_WH = 8
NAT_WW = 16
MOE_GROUPS = 4
MOE_PER_GROUP = 8
MOE_EXPERTS = MOE_GROUPS * MOE_PER_GROUP
MOE_TOPK = 2
MOE_FF = 512
Q_BLOCK = 128
EPS = 1e-6

N_PROMPT = BATCH * SEQ
N_SAMPLE = DEC_BATCH * DEC_SEQ
N_TOK = N_PROMPT + N_SAMPLE
LANES = 128
MOD_ROWS = 8

C_POOL, C_GQ, C_GK, C_GV, C_GG = 0, 1024, 1536, 2048, 3072
C_DQ, C_DK, C_DV, C_NQ, C_NK, C_NV, C_GD = 4096, 5120, 6144, 7168, 8192, 9216, 10240
D_IN_PAD = 10752
ORIG_GD = 3 * GROUP_W + 2 * GLA_HEADS * GLA_DK

TM = 512
TN_IN = 768
TN_OUT = 1024
TM_E = 256
N_ASSIGN = N_TOK * MOE_TOPK
N_ETILES = N_ASSIGN // TM_E + MOE_EXPERTS
VMEM_LIMIT = 56 * 1024 * 1024


def _cparams(sem):
    return pltpu.CompilerParams(dimension_semantics=sem, vmem_limit_bytes=VMEM_LIMIT)


def _mod_row(i):
    npt = N_PROMPT // TM
    return jnp.where(i < npt, 0, 1 + (i - npt) // (DEC_SEQ // TM))


def _ada_kernel(c_ref, w_ref, b_ref, o_ref):
    @pl.when(pl.program_id(2) == 0)
    def _():
        o_ref[0] = jnp.broadcast_to(b_ref[0], o_ref.shape[1:])

    c = c_ref[...]
    a = (c * jax.nn.sigmoid(c)).astype(bf16)
    o_ref[0] += jnp.dot(a, w_ref[0].astype(bf16), preferred_element_type=f32)


def ada_modulation(cvec, w_ada, b_ada):
    tk, tn = 2048, 1024
    n6 = 6 * D_MODEL
    return pl.pallas_call(
        _ada_kernel,
        grid=(DEPTH, n6 // tn, D_MODEL // tk),
        in_specs=[pl.BlockSpec((MOD_ROWS, tk), lambda l, j, k: (0, k)),
                  pl.BlockSpec((1, tk, tn), lambda l, j, k: (l, k, j)),
                  pl.BlockSpec((1, 1, tn), lambda l, j, k: (l, 0, j))],
        out_specs=pl.BlockSpec((1, MOD_ROWS, tn), lambda l, j, k: (l, 0, j)),
        out_shape=jax.ShapeDtypeStruct((DEPTH, MOD_ROWS, n6), f32),
        compiler_params=_cparams(("parallel", "parallel", "arbitrary")),
        name="ada_modulation",
    )(cvec, w_ada, b_ada.reshape(DEPTH, 1, n6))


def _modulated_norm(x, g, shift, scale):
    y = x * lax.rsqrt(jnp.mean(x * x, axis=-1, keepdims=True) + EPS) * g
    return y * (1.0 + scale) + shift


def _win_kernel(x_ref, g_ref, mod_ref, w_ref, o_ref, h_ref):
    @pl.when(pl.program_id(1) == 0)
    def _():
        h = _modulated_norm(x_ref[...], g_ref[...], mod_ref[0, 0:1, :], mod_ref[0, 1:2, :])
        h_ref[...] = h.astype(bf16)

    o_ref[...] = jnp.dot(h_ref[...], w_ref[...], preferred_element_type=f32)


def input_projection(x, g_norm, mod, w):
    return pl.pallas_call(
        _win_kernel,
        grid=(N_TOK // TM, D_IN_PAD // TN_IN),
        in_specs=[pl.BlockSpec((TM, D_MODEL), lambda i, j: (i, 0)),
                  pl.BlockSpec((1, D_MODEL), lambda i, j: (0, 0)),
                  pl.BlockSpec((1, 6, D_MODEL), lambda i, j: (_mod_row(i), 0, 0)),
                  pl.BlockSpec((D_MODEL, TN_IN), lambda i, j: (0, j))],
        out_specs=pl.BlockSpec((TM, TN_IN), lambda i, j: (i, j)),
        out_shape=jax.ShapeDtypeStruct((N_TOK, D_IN_PAD), f32),
        scratch_shapes=[pltpu.VMEM((TM, D_MODEL), bf16)],
        compiler_params=_cparams(("parallel", "arbitrary")),
        name="input_projection",
    )(x, g_norm.reshape(1, D_MODEL), mod, w)


def _wout_kernel(a0, a1, a2, a3, w0, w1, w2, w3, x_ref, mod_ref, o_ref):
    acc = jnp.dot(a0[...], w0[...], preferred_element_type=f32)
    acc += jnp.dot(a1[...], w1[...], preferred_element_type=f32)
    acc += jnp.dot(a2[...], w2[...], preferred_element_type=f32)
    acc += jnp.dot(a3[...], w3[...], preferred_element_type=f32)
    o_ref[...] = x_ref[...] + mod_ref[0, 2:3, :] * acc


def output_projection(parts, w, x, mod):
    a_specs = [pl.BlockSpec((TM, GROUP_W), lambda i, j: (i, 0)) for _ in range(4)]
    w_specs = [pl.BlockSpec((GROUP_W, TN_OUT), functools.partial(lambda i, j, c: (c, j), c=c))
               for c in range(4)]
    return pl.pallas_call(
        _wout_kernel,
        grid=(N_TOK // TM, D_MODEL // TN_OUT),
        in_specs=a_specs + w_specs + [
            pl.BlockSpec((TM, TN_OUT), lambda i, j: (i, j)),
            pl.BlockSpec((1, 6, TN_OUT), lambda i, j: (_mod_row(i), 0, j))],
        out_specs=pl.BlockSpec((TM, TN_OUT), lambda i, j: (i, j)),
        out_shape=jax.ShapeDtypeStruct((N_TOK, D_MODEL), f32),
        compiler_params=_cparams(("parallel", "arbitrary")),
        name="output_projection",
    )(*parts, w, w, w, w, x, mod)


def _split_bf16(a):
    hi = a.astype(bf16)
    lo = (a - hi.astype(f32)).astype(bf16)
    return hi, lo


def _router_kernel(x_ref, g_ref, mod_ref, wr_ref, br_ref, h_ref, r_ref):
    h = _modulated_norm(x_ref[...], g_ref[...], mod_ref[0, 3:4, :], mod_ref[0, 4:5, :])
    h_ref[...] = h.astype(bf16)
    h_hi, h_lo = _split_bf16(h)
    w_hi, w_lo = _split_bf16(wr_ref[...])
    lg = (jnp.dot(h_hi, w_hi, preferred_element_type=f32)
          + jnp.dot(h_lo, w_hi, preferred_element_type=f32)
          + jnp.dot(h_hi, w_lo, preferred_element_type=f32)) + br_ref[...]
    lane = lax.broadcasted_iota(jnp.int32, lg.shape, 1).astype(f32)
    ninf = -jnp.inf

    def first_max(v):
        m = jnp.max(v, axis=-1, keepdims=True)
        idx = jnp.min(jnp.where(v == m, lane, float(LANES)), axis=-1, keepdims=True)
        return m, idx

    gmask = lane < MOE_GROUPS
    mg, gi = first_max(jnp.where(gmask, lg, ninf))
    p_top = 1.0 / jnp.sum(jnp.where(gmask, jnp.exp(lg - mg), 0.0), axis=-1, keepdims=True)
    lo = MOE_GROUPS + gi * MOE_PER_GROUP
    le = jnp.where((lane >= lo) & (lane < lo + MOE_PER_GROUP), lg, ninf)
    v1, i1 = first_max(le)
    v2, i2 = first_max(jnp.where(lane == i1, ninf, le))
    t = jnp.exp(v2 - v1)
    w1 = p_top / (1.0 + t)
    w2 = w1 * t
    r_ref[...] = jnp.where(lane == 0.0, i1 - MOE_GROUPS,
                           jnp.where(lane == 1.0, i2 - MOE_GROUPS,
                                     jnp.where(lane == 2.0, w1, jnp.where(lane == 3.0, w2, 0.0))))


def moe_router(x, g_norm, mod, wr, br):
    tm = 256
    return pl.pallas_call(
        _router_kernel,
        grid=(N_TOK // tm,),
        in_specs=[pl.BlockSpec((tm, D_MODEL), lambda i: (i, 0)),
                  pl.BlockSpec((1, D_MODEL), lambda i: (0, 0)),
                  pl.BlockSpec((1, 6, D_MODEL), lambda i: (_mod_row(i // (TM // tm)), 0, 0)),
                  pl.BlockSpec((D_MODEL, LANES), lambda i: (0, 0)),
                  pl.BlockSpec((1, LANES), lambda i: (0, 0))],
        out_specs=[pl.BlockSpec((tm, D_MODEL), lambda i: (i, 0)),
                   pl.BlockSpec((tm, LANES), lambda i: (i, 0))],
        out_shape=[jax.ShapeDtypeStruct((N_TOK, D_MODEL), bf16),
                   jax.ShapeDtypeStruct((N_TOK, LANES), f32)],
        compiler_params=_cparams(("parallel",)),
        name="moe_router",
    )(x, g_norm.reshape(1, D_MODEL), mod, wr, br)


def _ffn_kernel(te_ref, nu_ref, x_ref, wg_ref, wu_ref, wd_ref, o_ref):
    @pl.when(pl.program_id(0) < nu_ref[0])
    def _():
        x = x_ref[...]
        a = jnp.dot(x, wg_ref[0], preferred_element_type=f32)
        u = jnp.dot(x, wu_ref[0], preferred_element_type=f32)
        mid = (a * jax.nn.sigmoid(a) * u).astype(bf16)
        o_ref[...] = jnp.dot(mid, wd_ref[0], preferred_element_type=f32)


def expert_ffn(tile_expert, n_used, xs, wg, wu, wd):
    row_blk = lambda t, te, nu: (jnp.minimum(t, nu[0] - 1), 0)
    return pl.pallas_call(
        _ffn_kernel,
        grid_spec=pltpu.PrefetchScalarGridSpec(
            num_scalar_prefetch=2,
            grid=(N_ETILES,),
            in_specs=[pl.BlockSpec((TM_E, D_MODEL), row_blk),
                      pl.BlockSpec((1, D_MODEL, MOE_FF), lambda t, te, nu: (te[t], 0, 0)),
                      pl.BlockSpec((1, D_MODEL, MOE_FF), lambda t, te, nu: (te[t], 0, 0)),
                      pl.BlockSpec((1, MOE_FF, D_MODEL), lambda t, te, nu: (te[t], 0, 0))],
            out_specs=pl.BlockSpec((TM_E, D_MODEL), row_blk)),
        out_shape=jax.ShapeDtypeStruct((N_ETILES * TM_E, D_MODEL), f32),
        compiler_params=_cparams(("arbitrary",)),
        name="expert_ffn",
    )(tile_expert, n_used, xs, wg, wu, wd)


def route_layout(route):
    e = route[:, :MOE_TOPK].astype(jnp.int32).reshape(-1)
    onehot = (e[:, None] == jnp.arange(MOE_EXPERTS, dtype=jnp.int32)[None, :]).astype(jnp.int32)
    csum = jnp.cumsum(onehot, axis=0)
    rank = jnp.sum(onehot * (csum - 1), axis=1)
    cnt = csum[-1]
    tiles_e = (cnt + TM_E - 1) // TM_E
    tile_end = jnp.cumsum(tiles_e)
    tile_start = tile_end - tiles_e
    pos = tile_start[e] * TM_E + rank
    slot_token = jnp.zeros((N_ETILES * TM_E,), jnp.int32).at[pos].set(
        jnp.arange(N_ASSIGN, dtype=jnp.int32) // MOE_TOPK)
    n_used = tile_end[-1]
    t = jnp.arange(N_ETILES, dtype=jnp.int32)
    tile_expert = jnp.searchsorted(tile_end, jnp.minimum(t, n_used - 1), side="right").astype(jnp.int32)
    tile_expert = jnp.minimum(tile_expert, MOE_EXPERTS - 1)
    return pos.reshape(N_TOK, MOE_TOPK), slot_token, tile_expert, n_used.reshape(1).astype(jnp.int32)


def _combine_kernel(x_ref, y0_ref, y1_ref, r_ref, mod_ref, o_ref):
    y = r_ref[:, 2:3] * y0_ref[...] + r_ref[:, 3:4] * y1_ref[...]
    o_ref[...] = x_ref[...] + mod_ref[0, 5:6, :] * y


def _combine_norm_kernel(x_ref, y0_ref, y1_ref, r_ref, mod_ref, g_ref, o_ref):
    y = r_ref[:, 2:3] * y0_ref[...] + r_ref[:, 3:4] * y1_ref[...]
    x = x_ref[...] + mod_ref[0, 5:6, :] * y
    o_ref[...] = x * lax.rsqrt(jnp.mean(x * x, axis=-1, keepdims=True) + EPS) * g_ref[...]


def moe_combine(x, y0, y1, route, mod, g_final=None):
    tm = 256
    row = pl.BlockSpec((tm, D_MODEL), lambda i: (i, 0))
    in_specs = [row, row, row, pl.BlockSpec((tm, LANES), lambda i: (i, 0)),
                pl.BlockSpec((1, 6, D_MODEL), lambda i: (_mod_row(i // (TM // tm)), 0, 0))]
    args = [x, y0, y1, route, mod]
    body = _combine_kernel
    if g_final is not None:
        in_specs.append(pl.BlockSpec((1, D_MODEL), lambda i: (0, 0)))
        args.append(g_final.reshape(1, D_MODEL))
        body = _combine_norm_kernel
    return pl.pallas_call(
        body,
        grid=(N_TOK // tm,),
        in_specs=in_specs,
        out_specs=row,
        out_shape=jax.ShapeDtypeStruct((N_TOK, D_MODEL), f32),
        compiler_params=_cparams(("parallel",)),
        name="moe_combine",
    )(*args)


def _diff_prep_kernel(rope, q_ref, k_ref, cos_ref, sin_ref, q2_ref, kb_ref):
    lane = lax.broadcasted_iota(jnp.int32, q_ref.shape, 1)

    def rot(x):
        if not rope:
            return x
        partner = jnp.where((lane & 31) >= 16, pltpu.roll(x, 16, 1), pltpu.roll(x, LANES - 16, 1))
        return x * cos_ref[...] + partner * sin_ref[...]

    q = rot(q_ref[...]) * DIFF_HD ** -0.5
    first = lane < DIFF_HD
    q2_ref[0] = jnp.where(first, q, 0.0).astype(bf16)
    q2_ref[1] = jnp.where(first, 0.0, q).astype(bf16)
    kb_ref[...] = rot(k_ref[...]).astype(bf16)


def _rope_tables():
    t = jnp.arange(DEC_SEQ)
    row = (t // GRID_W).astype(f32)
    col = (t % GRID_W).astype(f32)
    nf = DIFF_HD // 4
    inv = ROPE_BASE ** (-jnp.arange(nf, dtype=f32) / nf)
    lane = np.arange(LANES)
    pos = jnp.where(((lane // 32) % 2 == 0)[None, :], row[:, None], col[:, None])
    ang = pos * inv[lane % nf][None, :]
    sign = np.where(lane % 32 >= nf, 1.0, -1.0).astype(np.float32)
    return jnp.cos(ang), jnp.sin(ang) * sign[None, :]


def diff_prep(u, row0, n_rows, rope):
    t = 1024
    r0 = row0 // t
    nseq = DEC_SEQ // t
    cos, sin = _rope_tables() if rope else (jnp.zeros((t, LANES), f32), jnp.zeros((t, LANES), f32))
    tab = pl.BlockSpec((t, LANES), (lambda i, h: (i % nseq, 0)) if rope else (lambda i, h: (0, 0)))
    return pl.pallas_call(
        functools.partial(_diff_prep_kernel, rope),
        grid=(n_rows // t, DIFF_HEADS),
        in_specs=[pl.BlockSpec((t, LANES), lambda i, h: (r0 + i, C_DQ // LANES + h)),
                  pl.BlockSpec((t, LANES), lambda i, h: (r0 + i, C_DK // LANES + h)),
                  tab, tab],
        out_specs=[pl.BlockSpec((2, t, LANES), lambda i, h: (0, i, h)),
                   pl.BlockSpec((t, LANES), lambda i, h: (i, h))],
        out_shape=[jax.ShapeDtypeStruct((2, n_rows, GROUP_W), bf16),
                   jax.ShapeDtypeStruct((n_rows, GROUP_W), bf16)],
        compiler_params=_cparams(("parallel", "parallel")),
        name="diff_prep",
    )(u, u, cos, sin)


def _flash_kernel(n_stack, tq, scale, out_scale, q_ref, k_ref, v_ref, lam_ref, g_ref, o_ref,
                  m_ref, l_ref, acc_ref):
    kj = pl.program_id(3)

    @pl.when(kj == 0)
    def _():
        m_ref[...] = jnp.full(m_ref.shape, -jnp.inf, f32)
        l_ref[...] = jnp.zeros(l_ref.shape, f32)
        acc_ref[...] = jnp.zeros(acc_ref.shape, f32)

    q = q_ref[...].reshape(n_stack * tq, LANES).astype(bf16)
    s = lax.dot_general(q, k_ref[0].astype(bf16), (((1,), (1,)), ((), ())), preferred_element_type=f32)
    if scale != 1.0:
        s = s * scale
    m_prev = m_ref[...]
    m_new = jnp.maximum(m_prev, jnp.max(s, axis=-1, keepdims=True))
    alpha = jnp.exp(m_prev - m_new)
    p = jnp.exp(s - m_new)
    l_ref[...] = alpha * l_ref[...] + jnp.sum(p, axis=-1, keepdims=True)
    acc_ref[...] = alpha * acc_ref[...] + jnp.dot(p.astype(bf16), v_ref[0].astype(bf16),
                                                  preferred_element_type=f32)
    m_ref[...] = m_new

    @pl.when(kj == pl.num_programs(3) - 1)
    def _():
        o = acc_ref[...] / l_ref[...]
        if n_stack == 2:
            o = o[:tq] - lam_ref[...] * o[tq:]
            o = o * lax.rsqrt(jnp.mean(o * o, axis=-1, keepdims=True) + EPS) * g_ref[...] * out_scale
        o_ref[...] = o.astype(o_ref.dtype)


def flash_attention(q, k, v, *, n_batch, lq, tq, tk, qcol, kcol, vcol, qrow0=0, kb0=0, scale=1.0,
                    lam=None, g=None, out_scale=1.0):
    n_stack = q.shape[0]
    lk = k.shape[1]
    nq = lq // tq
    lam = jnp.zeros((1, LANES), f32) if lam is None else jnp.full((1, LANES), lam, f32)
    g = jnp.ones((1, LANES), f32) if g is None else g.reshape(1, LANES).astype(f32)
    rows = n_stack * tq
    return pl.pallas_call(
        functools.partial(_flash_kernel, n_stack, tq, scale, out_scale),
        grid=(n_batch, DIFF_HEADS, nq, lk // tk),
        in_specs=[pl.BlockSpec((n_stack, tq, LANES), lambda b, h, i, j: (0, qrow0 + b * nq + i, qcol + h)),
                  pl.BlockSpec((1, tk, LANES), lambda b, h, i, j: (kb0 + b, j, kcol + h)),
                  pl.BlockSpec((1, tk, LANES), lambda b, h, i, j: (kb0 + b, j, vcol + h)),
                  pl.BlockSpec((1, LANES), lambda b, h, i, j: (0, 0)),
                  pl.BlockSpec((1, LANES), lambda b, h, i, j: (0, 0))],
        out_specs=pl.BlockSpec((tq, LANES), lambda b, h, i, j: (b * nq + i, h)),
        out_shape=jax.ShapeDtypeStruct((n_batch * lq, GROUP_W), bf16),
        scratch_shapes=[pltpu.VMEM((rows, 1), f32), pltpu.VMEM((rows, 1), f32),
                        pltpu.VMEM((rows, LANES), f32)],
        compiler_params=_cparams(("parallel", "parallel", "parallel", "arbitrary")),
        name="flash_attention",
    )(q, k, v, lam, g)


NAT_ROWS = DEC_SEQ // GRID_W
NAT_RB = 8
NAT_BAND = NAT_WH * GRID_W
NEG_BIG = -1e30


def _nat_kernel(q_ref, k_ref, v_ref, kc_ref, vc_ref, bias_ref, o_ref):
    rb = pl.program_id(2)
    scale = NAT_HD ** -0.5
    kc = kc_ref[0].astype(bf16)
    vc = vc_ref[0].astype(bf16)
    nt = (((1,), (1,)), ((), ()))
    for i in range(NAT_RB):
        r = rb * NAT_RB + i
        rs = jnp.clip(r - NAT_WH // 2, 0, NAT_ROWS - NAT_WH)
        k0 = pl.multiple_of(rs * GRID_W, GRID_W)
        kband = k_ref[pl.ds(k0, NAT_BAND), :].astype(bf16)
        vband = v_ref[pl.ds(k0, NAT_BAND), :].astype(bf16)
        q = q_ref[i * GRID_W:(i + 1) * GRID_W, :].astype(bf16)
        sb = lax.dot_general(q, kband, nt, preferred_element_type=f32) * scale + bias_ref[rs - r + NAT_WH - 1, 0]
        sc = lax.dot_general(q, kc, nt, preferred_element_type=f32) * scale
        m = jnp.maximum(jnp.max(sb, axis=-1, keepdims=True), jnp.max(sc, axis=-1, keepdims=True))
        pb = jnp.exp(sb - m)
        pc = jnp.exp(sc - m)
        l = jnp.sum(pb, axis=-1, keepdims=True) + jnp.sum(pc, axis=-1, keepdims=True)
        o = (jnp.dot(pb.astype(bf16), vband, preferred_element_type=f32)
             + jnp.dot(pc.astype(bf16), vc, preferred_element_type=f32)) / l
        o_ref[i * GRID_W:(i + 1) * GRID_W, :] = o.astype(o_ref.dtype)


def _nat_bias_table(rpb):
    colv = np.arange(GRID_W)
    cstart = np.clip(colv - NAT_WW // 2, 0, GRID_W - NAT_WW)
    col_mask = (colv[None, :] >= cstart[:, None]) & (colv[None, :] < cstart[:, None] + NAT_WW)
    col_idx = np.clip(colv[None, :] - colv[:, None] + NAT_WW - 1, 0, 2 * NAT_WW - 2)
    rpb_c = rpb.astype(f32)[:, :, col_idx]
    row_idx = np.arange(NAT_WH)[:, None] + np.arange(NAT_WH)[None, :]
    tab = rpb_c[:, row_idx]
    tab = jnp.where(col_mask[None, None, None], tab, NEG_BIG)
    return tab.transpose(1, 0, 3, 2, 4).reshape(NAT_WH, NAT_HEADS, GRID_W, NAT_BAND)


def nat_attention(u, row0, kc, vc, rpb):
    qblk = NAT_RB * GRID_W
    q0 = row0 // qblk
    b0 = row0 // DEC_SEQ
    return pl.pallas_call(
        _nat_kernel,
        grid=(DEC_BATCH, NAT_HEADS, NAT_ROWS // NAT_RB),
        in_specs=[pl.BlockSpec((qblk, LANES), lambda b, h, r: (q0 + b * (NAT_ROWS // NAT_RB) + r, C_NQ // LANES + h)),
                  pl.BlockSpec((DEC_SEQ, LANES), lambda b, h, r: (b0 + b, C_NK // LANES + h)),
                  pl.BlockSpec((DEC_SEQ, LANES), lambda b, h, r: (b0 + b, C_NV // LANES + h)),
                  pl.BlockSpec((1, PAST_LEN, LANES), lambda b, h, r: (b, 0, h)),
                  pl.BlockSpec((1, PAST_LEN, LANES), lambda b, h, r: (b, 0, h)),
                  pl.BlockSpec((NAT_WH, 1, GRID_W, NAT_BAND), lambda b, h, r: (0, h, 0, 0))],
        out_specs=pl.BlockSpec((qblk, LANES), lambda b, h, r: (b * (NAT_ROWS // NAT_RB) + r, h)),
        out_shape=jax.ShapeDtypeStruct((N_SAMPLE, GROUP_W), bf16),
        compiler_params=_cparams(("parallel", "parallel", "arbitrary")),
        name="nat_attention",
    )(u, u, u, kc, vc, _nat_bias_table(rpb))


def _pool_kernel(seq, u_ref, w_ref, s_ref, o_ref):
    grp = pl.program_id(1)
    for gi, win in enumerate(POOL_WINDOWS):
        @pl.when(grp == gi)
        def _(win=win):
            u = u_ref[...]
            t = lax.broadcasted_iota(jnp.int32, u.shape, 0)
            acc = jnp.zeros_like(u)
            for d in range(-(win // 2), win // 2):
                shifted = u if d == 0 else pltpu.roll(u, (-d) % seq, 0)
                acc += jnp.where((t + d >= 0) & (t + d < seq), shifted, 0.0)
            cnt = (jnp.minimum(t + win // 2, seq) - jnp.maximum(t - win // 2, 0)).astype(f32)
            p = acc / cnt - u
            y = jnp.dot(p.astype(bf16), w_ref[0].astype(bf16), preferred_element_type=f32) * s_ref[...]
            o_ref[...] = y.astype(o_ref.dtype)


def pool_mixer(u, row0, n_seq, seq, w, s):
    return pl.pallas_call(
        functools.partial(_pool_kernel, seq),
        grid=(n_seq, len(POOL_WINDOWS)),
        in_specs=[pl.BlockSpec((seq, POOL_GW), lambda b, g: (row0 // seq + b, g)),
                  pl.BlockSpec((1, POOL_GW, POOL_GW), lambda b, g: (g, 0, 0)),
                  pl.BlockSpec((1, POOL_GW), lambda b, g: (0, g))],
        out_specs=pl.BlockSpec((seq, POOL_GW), lambda b, g: (b, g)),
        out_shape=jax.ShapeDtypeStruct((n_seq * seq, GROUP_W), bf16),
        compiler_params=_cparams(("parallel", "parallel")),
        name="pool_mixer",
    )(u, w, s.reshape(1, GROUP_W))


GLA_QK = GLA_HEADS * GLA_DK


def _gla_kernel(reverse, final, n_chunks, *refs):
    if final:
        (q_ref, k_ref, v_ref, gd_ref, wup_ref, bup_ref, s0_ref, op_ref, gg_ref, gn_ref,
         o_ref, s_ref, st_ref) = refs
    else:
        q_ref, k_ref, v_ref, gd_ref, wup_ref, bup_ref, s0_ref, o_ref, s_ref, st_ref = refs
    j = pl.program_id(1)
    C = GLA_CHUNK

    @pl.when(j == 0)
    def _():
        for h in range(GLA_HEADS):
            s0 = s0_ref[0, h]
            z = jnp.zeros_like(s0)
            st_ref[h] = jnp.concatenate([s0, z] if h % 2 == 0 else [z, s0], axis=0).T

    ti = lax.broadcasted_iota(jnp.int32, (C, C), 0)
    tj = lax.broadcasted_iota(jnp.int32, (C, C), 1)
    tri = (ti <= tj) if reverse else (ti >= tj)
    tri_b = jnp.where(tri, 1.0, 0.0).astype(bf16)
    first_half = lax.broadcasted_iota(jnp.int32, (C, LANES), 1) < GLA_DK
    ref_row = C // 2 - 1 if reverse else C // 2
    end_row = 0 if reverse else C - 1
    nt = (((1,), (1,)), ((), ()))

    def chunk(ci, carry):
        c = n_chunks - 1 - ci if reverse else ci
        rows = pl.ds(pl.multiple_of(c * C, C), C)
        logit = jnp.dot(gd_ref[rows, :].astype(bf16), wup_ref[...], preferred_element_type=f32) + bup_ref[...]
        la = (jnp.minimum(logit, 0.0) - jnp.log1p(jnp.exp(-jnp.abs(logit)))) * (1.0 / GLA_GATE_NORM)
        hi = la.astype(bf16)
        r1 = la - hi.astype(f32)
        mid = r1.astype(bf16)
        lo = (r1 - mid.astype(f32)).astype(bf16)
        b = (jnp.dot(tri_b, hi, preferred_element_type=f32) + jnp.dot(tri_b, mid, preferred_element_type=f32)
             + jnp.dot(tri_b, lo, preferred_element_type=f32))
        bref = b[ref_row:ref_row + 1, :]
        bl = b[end_row:end_row + 1, :]
        q = q_ref[rows, :] * GLA_DK ** -0.5
        k = k_ref[rows, :]
        qs = q * jnp.exp(b - bref)
        ks = (k * jnp.exp(bref - b)).astype(bf16)
        qe = (q * jnp.exp(b)).astype(bf16)
        kd = k * jnp.exp(bl - b)
        ebl = jnp.exp(bl)
        for h in range(GLA_HEADS):
            pair = slice((h // 2) * LANES, (h // 2 + 1) * LANES)
            mine = first_half if h % 2 == 0 else jnp.logical_not(first_half)
            cols = slice(h * GLA_DV, (h + 1) * GLA_DV)
            a = lax.dot_general(jnp.where(mine, qs[:, pair], 0.0).astype(bf16), ks[:, pair], nt,
                                preferred_element_type=f32)
            a = jnp.where(tri, a, 0.0).astype(bf16)
            vh = v_ref[rows, cols]
            st = st_ref[h]
            o = (jnp.dot(a, vh.astype(bf16), preferred_element_type=f32)
                 + lax.dot_general(qe[:, pair], st.astype(bf16), nt, preferred_element_type=f32))
            st_ref[h] = st * ebl[:, pair] + jnp.dot(vh.T.astype(bf16),
                                                    jnp.where(mine, kd[:, pair], 0.0).astype(bf16),
                                                    preferred_element_type=f32)
            if final:
                o = o + op_ref[rows, cols]
                o = o * lax.rsqrt(jnp.mean(o * o, axis=-1, keepdims=True) + EPS) * gn_ref[...]
                gate = gg_ref[rows, cols]
                o = o * (gate * jax.nn.sigmoid(gate))
            o_ref[rows, cols] = o.astype(o_ref.dtype)
        return carry

    lax.fori_loop(0, n_chunks, chunk, 0)

    @pl.when(j == pl.num_programs(1) - 1)
    def _():
        for h in range(GLA_HEADS):
            half = (h % 2) * GLA_DK
            s_ref[0, h] = st_ref[h].T[half:half + GLA_DK, :]


def _gla_pass(u, row0, n_seq, seq, tb, z, w_up, b_up, s0, o_prev=None, g_norm=None):
    reverse = z == 1
    final = o_prev is not None
    nblk = seq // tb
    rb0 = row0 // tb
    wup = jnp.zeros((LANES, GLA_QK), f32).at[z * GLA_RANK:(z + 1) * GLA_RANK].set(w_up[z]).astype(bf16)

    def blk(b, j):
        return b * nblk + (nblk - 1 - j if reverse else j)

    def ucol(width, c0):
        return pl.BlockSpec((tb, width), lambda b, j: (rb0 + blk(b, j), c0 // width))

    in_specs = [ucol(GLA_QK, C_GQ), ucol(GLA_QK, C_GK), ucol(GROUP_W, C_GV), ucol(LANES, C_GD),
                pl.BlockSpec((LANES, GLA_QK), lambda b, j: (0, 0)),
                pl.BlockSpec((1, GLA_QK), lambda b, j: (0, 0)),
                pl.BlockSpec((1, GLA_HEADS, GLA_DK, GLA_DV), lambda b, j: (b, 0, 0, 0))]
    args = [u, u, u, u, wup, b_up[z].reshape(1, GLA_QK), s0]
    if final:
        in_specs += [pl.BlockSpec((tb, GROUP_W), lambda b, j: (blk(b, j), 0)), ucol(GROUP_W, C_GG),
                     pl.BlockSpec((1, GLA_DV), lambda b, j: (0, 0))]
        args += [o_prev, u, g_norm.reshape(1, GLA_DV)]
    return pl.pallas_call(
        functools.partial(_gla_kernel, reverse, final, tb // GLA_CHUNK),
        grid=(n_seq, nblk),
        in_specs=in_specs,
        out_specs=[pl.BlockSpec((tb, GROUP_W), lambda b, j: (blk(b, j), 0)),
                   pl.BlockSpec((1, GLA_HEADS, GLA_DK, GLA_DV), lambda b, j: (b, 0, 0, 0))],
        out_shape=[jax.ShapeDtypeStruct((n_seq * seq, GROUP_W), bf16 if final else f32),
                   jax.ShapeDtypeStruct((n_seq, GLA_HEADS, GLA_DK, GLA_DV), f32)],
        scratch_shapes=[pltpu.VMEM((GLA_HEADS, GLA_DV, LANES), f32)],
        compiler_params=_cparams(("parallel", "arbitrary")),
        name="gla_backward" if reverse else "gla_forward",
    )(*args)


def gla_mixer(u, row0, n_seq, seq, tb, w_up, b_up, g_norm, s0_f, s0_b):
    o_f, s_f = _gla_pass(u, row0, n_seq, seq, tb, 0, w_up, b_up, s0_f)
    o, s_b = _gla_pass(u, row0, n_seq, seq, tb, 1, w_up, b_up, s0_b, o_prev=o_f, g_norm=g_norm)
    return o, jnp.stack([s_f, s_b], axis=1)


def _head_rms(o, g):
    return o * lax.rsqrt(jnp.mean(o * o, axis=-1, keepdims=True) + EPS) * g.astype(f32)


def _pool_mixer(u, w, s):
    B, L, _ = u.shape
    cs = jnp.concatenate([jnp.zeros((B, 1, GROUP_W), u.dtype), jnp.cumsum(u, axis=1)], axis=1)
    t = jnp.arange(L)
    parts = []
    for i, win in enumerate(POOL_WINDOWS):
        lo = jnp.clip(t - win // 2, 0, L)
        hi = jnp.clip(t + win // 2, 0, L)
        cg = cs[..., i * POOL_GW:(i + 1) * POOL_GW]
        mean = (cg[:, hi] - cg[:, lo]) / (hi - lo).astype(u.dtype)[None, :, None]
        parts.append(mean - u[..., i * POOL_GW:(i + 1) * POOL_GW])
    p = jnp.stack(parts, axis=2)
    y = jnp.einsum('blgc,gcd->blgd', p, w.astype(u.dtype)).reshape(B, L, GROUP_W)
    return y * s.astype(u.dtype)


def _gla_scan(q, k, v, la, s0):
    B, L, H, _ = q.shape
    C = GLA_CHUNK
    n = L // C
    mask = jnp.tril(jnp.ones((C, C), dtype=bool))

    def chunks(a):
        return jnp.moveaxis(a.reshape(B, n, C, H, a.shape[-1]), 1, 0)

    def step(S, inp):
        qc, kc, vc, lac = inp
        b = jnp.cumsum(lac, axis=1)
        ref = b[:, C // 2:C // 2 + 1]
        a = jnp.einsum('bihd,bjhd->bhij', qc * jnp.exp(b - ref), kc * jnp.exp(ref - b))
        a = jnp.where(mask, a, 0.0)
        o = (jnp.einsum('bhij,bjhv->bihv', a, vc)
             + jnp.einsum('bihd,bhdv->bihv', qc * jnp.exp(b), S))
        bl = b[:, -1]
        S = (jnp.exp(bl)[..., None] * S
             + jnp.einsum('bjhd,bjhv->bhdv', kc * jnp.exp(bl[:, None] - b), vc))
        return S, o

    S, o = lax.scan(step, s0, (chunks(q), chunks(k), chunks(v), chunks(la)))
    return jnp.moveaxis(o, 0, 1).reshape(B, L, H, v.shape[-1]), S


def _gla_mixer(q_in, k_in, v_in, g_in, gd_in, w_up, b_up, g_norm, s0_f, s0_b):
    B, L, _ = q_in.shape
    q = q_in.reshape(B, L, GLA_HEADS, GLA_DK) * GLA_DK ** -0.5
    k = k_in.reshape(B, L, GLA_HEADS, GLA_DK)
    v = v_in.reshape(B, L, GLA_HEADS, GLA_DV)
    gd = gd_in.reshape(B, L, 2, GLA_RANK)
    logit = jnp.einsum('blzr,zre->blze', gd, w_up.astype(f32)) + b_up.astype(f32)
    la = (jax.nn.log_sigmoid(logit) / GLA_GATE_NORM).reshape(B, L, 2, GLA_HEADS, GLA_DK)
    o_f, s_f = _gla_scan(q, k, v, la[:, :, 0], s0_f)
    rev = lambda a: jnp.flip(a, axis=1)
    o_b, s_b = _gla_scan(rev(q), rev(k), rev(v), rev(la[:, :, 1]), s0_b)
    o = _head_rms(o_f + rev(o_b), g_norm).reshape(B, L, GROUP_W) * jax.nn.silu(g_in)
    return o, jnp.stack([s_f, s_b], axis=1)


def _axial_rope(x):
    L = x.shape[1]
    t = jnp.arange(L)
    row = (t // GRID_W).astype(f32)
    col = (t % GRID_W).astype(f32)
    half = DIFF_HD // 2
    nf = half // 2
    inv = ROPE_BASE ** (-jnp.arange(nf, dtype=f32) / nf)

    def rot(xh, pos):
        ang = pos[:, None] * inv
        cos = jnp.cos(ang)[None, :, None, None, :]
        sin = jnp.sin(ang)[None, :, None, None, :]
        x1, x2 = xh[..., :nf], xh[..., nf:]
        return jnp.concatenate([x1 * cos - x2 * sin, x2 * cos + x1 * sin], axis=-1)

    return jnp.concatenate([rot(x[..., :half], row), rot(x[..., half:], col)], axis=-1)


def _diff_attention(q, k, v, lam, lam_init, g_norm):
    B, Lq = q.shape[:2]
    nb = Lq // Q_BLOCK
    qb = jnp.moveaxis(q.reshape(B, nb, Q_BLOCK, DIFF_HEADS, 2, DIFF_HD), 1, 0)

    def blk(qi):
        s = jnp.einsum('bqhcd,bkhcd->bchqk', qi, k) * DIFF_HD ** -0.5
        p = jax.nn.softmax(s, axis=-1)
        a = p[:, 0] - lam * p[:, 1]
        return jnp.einsum('bhqk,bkhv->bqhv', a, v)

    o = jnp.moveaxis(lax.map(blk, qb), 0, 1).reshape(B, Lq, DIFF_HEADS, DIFF_VD)
    return (_head_rms(o, g_norm) * (1.0 - lam_init)).reshape(B, Lq, GROUP_W)


def _dense_attention(q, k, v):
    B, Lq, H, D = q.shape
    nb = Lq // Q_BLOCK
    qb = jnp.moveaxis(q.reshape(B, nb, Q_BLOCK, H, D), 1, 0)

    def blk(qi):
        p = jax.nn.softmax(jnp.einsum('bqhd,bkhd->bhqk', qi, k) * D ** -0.5, axis=-1)
        return jnp.einsum('bhqk,bkhd->bqhd', p, v)

    return jnp.moveaxis(lax.map(blk, qb), 0, 1).reshape(B, Lq, H * D)


def _nat_latent(q, k, v, kc, vc, rpb):
    B, L, H, D = q.shape
    rows = L // GRID_W
    wh = min(NAT_WH, rows)
    qg = q.reshape(B, rows, GRID_W, H, D)
    kg = k.reshape(B, rows, GRID_W, H, D)
    vg = v.reshape(B, rows, GRID_W, H, D)
    colv = np.arange(GRID_W)
    cstart = np.clip(colv - NAT_WW // 2, 0, GRID_W - NAT_WW)
    col_mask = (colv[None, :] >= cstart[:, None]) & (colv[None, :] < cstart[:, None] + NAT_WW)
    col_idx = np.clip(colv[None, :] - colv[:, None] + NAT_WW - 1, 0, 2 * NAT_WW - 2)
    rpb_c = rpb.astype(f32)[:, :, col_idx]
    scale = D ** -0.5
    nkey = wh * GRID_W

    def row(r):
        rs = jnp.clip(r - wh // 2, 0, rows - wh)
        kr = lax.dynamic_slice_in_dim(kg, rs, wh, axis=1)
        vr = lax.dynamic_slice_in_dim(vg, rs, wh, axis=1)
        qr = lax.dynamic_index_in_dim(qg, r, axis=1, keepdims=False)
        row_idx = rs + jnp.arange(wh) - r + (NAT_WH - 1)
        bias = jnp.take(rpb_c, row_idx, axis=1).transpose(0, 2, 1, 3)
        s = jnp.einsum('bqhd,bwkhd->bhqwk', qr, kr) * scale + bias[None]
        s = jnp.where(col_mask[None, None, :, None, :], s, -jnp.inf).reshape(B, H, GRID_W, nkey)
        sc = jnp.einsum('bqhd,bkhd->bhqk', qr, kc) * scale
        p = jax.nn.softmax(jnp.concatenate([s, sc], axis=-1), axis=-1)
        return (jnp.einsum('bhqk,bkhd->bqhd', p[..., :nkey], vr.reshape(B, nkey, H, D))
                + jnp.einsum('bhqk,bkhd->bqhd', p[..., nkey:], vc))

    o = lax.map(row, jnp.arange(rows))
    return jnp.moveaxis(o, 0, 1).reshape(B, L, H * D)


def kernel(x_prompt, x_sample, cache_diff_k, cache_diff_v, cache_nat_k, cache_nat_v, state_gla, c, c_ctx, w_ada, b_ada, norm1, w_in, pool_w, pool_scale, gla_w_up, gla_b_up, gla_norm, diff_lambda, diff_norm, nat_rpb, w_out, norm2, router_group_w, router_group_b, router_expert_w, router_expert_b, expert_w_gate, expert_w_up, expert_w_down, norm_final):
    Bp, Lp, Bs, Ls = BATCH, SEQ, DEC_BATCH, DEC_SEQ
    x = jnp.concatenate([x_prompt.reshape(N_PROMPT, D_MODEL), x_sample.reshape(N_SAMPLE, D_MODEL)], axis=0)
    cvec = jnp.zeros((MOD_ROWS, D_MODEL), f32).at[0].set(c_ctx).at[1:1 + DEC_BATCH].set(c)
    mod_all = ada_modulation(cvec, w_ada, b_ada).reshape(DEPTH, MOD_ROWS, 6, D_MODEL)

    new_dk, new_dv, new_nk, new_nv, new_gs = [], [], [], [], []
    for l in range(DEPTH):
        mod = mod_all[l]
        lam_init = 0.8 - 0.6 * math.exp(-0.3 * l)
        lv = diff_lambda[l].astype(f32)
        lam = jnp.exp(jnp.sum(lv[0] * lv[1])) - jnp.exp(jnp.sum(lv[2] * lv[3])) + lam_init

        w_l = w_in[l]
        w_perm = jnp.concatenate(
            [w_l[:, :ORIG_GD], w_l[:, ORIG_GD + 2 * GLA_RANK:], w_l[:, ORIG_GD:ORIG_GD + 2 * GLA_RANK],
             jnp.zeros((D_MODEL, D_IN_PAD - C_GD - 2 * GLA_RANK), f32)], axis=1).astype(bf16)
        u = input_projection(x, norm1[l], mod, w_perm)

        u_seq = u.reshape(N_TOK // Lp, Lp, D_IN_PAD)
        up = u[:N_PROMPT]

        zero = jnp.zeros((Bp, GLA_HEADS, GLA_DK, GLA_DV), f32)
        o_gla_p, gs = gla_mixer(u, 0, Bp, Lp, Lp, gla_w_up[l], gla_b_up[l], gla_norm[l], zero, zero)
        q2_p, kb_p = diff_prep(u, 0, N_PROMPT, rope=False)
        o_diff_p = flash_attention(q2_p, kb_p.reshape(Bp, Lp, GROUP_W), u_seq, n_batch=Bp, lq=Lp, tq=Lp, tk=Lp,
                                   qcol=0, kcol=0, vcol=C_DV // LANES, lam=lam, g=diff_norm[l],
                                   out_scale=1.0 - lam_init)
        o_nat_p = flash_attention(u.reshape(1, N_TOK, D_IN_PAD), u_seq, u_seq, n_batch=Bp, lq=Lp, tq=Lp, tk=Lp,
                                  qcol=C_NQ // LANES, kcol=C_NK // LANES, vcol=C_NV // LANES,
                                  scale=NAT_HD ** -0.5)
        o_pool_p = pool_mixer(u, 0, Bp, Lp, pool_w[l], pool_scale[l])
        new_dk.append(up[:, C_DK:C_DK + GROUP_W].reshape(Bp, Lp, DIFF_HEADS, 2 * DIFF_HD))
        new_dv.append(up[:, C_DV:C_DV + GROUP_W].reshape(Bp, Lp, DIFF_HEADS, DIFF_VD))
        new_nk.append(up[:, C_NK:C_NK + GROUP_W].reshape(Bp, Lp, NAT_HEADS, NAT_HD))
        new_nv.append(up[:, C_NV:C_NV + GROUP_W].reshape(Bp, Lp, NAT_HEADS, NAT_HD))
        new_gs.append(gs)

        st = state_gla[:, l].astype(f32)
        o_gla_s, _ = gla_mixer(u, N_PROMPT, Bs, Ls, 512, gla_w_up[l], gla_b_up[l], gla_norm[l],
                               st[:, 0], st[:, 1])
        q2_s, kb_s = diff_prep(u, N_PROMPT, N_SAMPLE, rope=True)
        k_all = jnp.concatenate([kb_s.reshape(Bs, Ls, GROUP_W),
                                 cache_diff_k[:, l].reshape(Bs, PAST_LEN, GROUP_W).astype(bf16)], axis=1)
        v_all = jnp.concatenate([u[N_PROMPT:, C_DV:C_DV + GROUP_W].astype(bf16).reshape(Bs, Ls, GROUP_W),
                                 cache_diff_v[:, l].reshape(Bs, PAST_LEN, GROUP_W).astype(bf16)], axis=1)
        o_diff_s = flash_attention(q2_s, k_all, v_all, n_batch=Bs, lq=Ls, tq=512, tk=512, qcol=0, kcol=0, vcol=0,
                                   lam=lam, g=diff_norm[l], out_scale=1.0 - lam_init)
        o_nat_s = nat_attention(u, N_PROMPT, cache_nat_k[:, l].reshape(Bs, PAST_LEN, GROUP_W),
                                cache_nat_v[:, l].reshape(Bs, PAST_LEN, GROUP_W), nat_rpb[l])
        o_pool_s = pool_mixer(u, N_PROMPT, Bs, Ls, pool_w[l], pool_scale[l])

        def both(a, b):
            return jnp.concatenate([a, b], axis=0)

        parts = [both(o_pool_p, o_pool_s), both(o_gla_p, o_gla_s), both(o_diff_p, o_diff_s),
                 both(o_nat_p, o_nat_s)]
        x = output_projection(parts, w_out[l].astype(bf16), x, mod)

        wr = jnp.zeros((D_MODEL, LANES), f32)
        wr = wr.at[:, :MOE_GROUPS].set(router_group_w[l]).at[:, MOE_GROUPS:MOE_GROUPS + MOE_EXPERTS].set(
            router_expert_w[l])
        br = jnp.zeros((1, LANES), f32)
        br = br.at[0, :MOE_GROUPS].set(router_group_b[l]).at[0, MOE_GROUPS:MOE_GROUPS + MOE_EXPERTS].set(
            router_expert_b[l])
        h2, route = moe_router(x, norm2[l], mod, wr, br)
        pos, slot_token, tile_expert, n_used = route_layout(route)
        xs = jnp.take(h2, slot_token, axis=0)
        ys = expert_ffn(tile_expert, n_used, xs, expert_w_gate[l].astype(bf16),
                        expert_w_up[l].astype(bf16), expert_w_down[l].astype(bf16))
        y0 = jnp.take(ys, pos[:, 0], axis=0)
        y1 = jnp.take(ys, pos[:, 1], axis=0)
        x = moe_combine(x, y0, y1, route, mod, norm_final if l == DEPTH - 1 else None)

    y_prompt = x[:N_PROMPT].reshape(Bp, Lp, D_MODEL)
    y_sample = x[N_PROMPT:].reshape(Bs, Ls, D_MODEL)
    return (y_prompt, y_sample, jnp.stack(new_dk, axis=1), jnp.stack(new_dv, axis=1),
            jnp.stack(new_nk, axis=1), jnp.stack(new_nv, axis=1), jnp.stack(new_gs, axis=1))
```

```python
import functools
import math

import jax
import jax.numpy as jnp
import numpy as np
from jax import lax
from jax.experimental import pallas as pl
from jax.experimental.pallas import tpu as pltpu

f32 = jnp.float32
bf16 = jnp.bfloat16

D_MODEL = 4096
BATCH = 32
SEQ = 256
DEPTH = 2
DEC_BATCH = 4
DEC_SEQ = 4096
PAST_LEN = 512
GRID_W = 64
GROUP_W = D_MODEL // 4
POOL_WINDOWS = (2, 4, 8, 16)
POOL_GW = GROUP_W // 4
GLA_HEADS = 8
GLA_DV = GROUP_W // GLA_HEADS
GLA_DK = GLA_DV // 2
GLA_RANK = 16
GLA_GATE_NORM = 16.0
GLA_CHUNK = 64
DIFF_HEADS = 8
DIFF_VD = GROUP_W // DIFF_HEADS
DIFF_HD = DIFF_VD // 2
ROPE_BASE = 10000.0
NAT_HEADS = 8
NAT_HD = GROUP_W // NAT_HEADS
NAT_WH = 8
NAT_WW = 16
MOE_GROUPS = 4
MOE_PER_GROUP = 8
MOE_EXPERTS = MOE_GROUPS * MOE_PER_GROUP
MOE_TOPK = 2
MOE_FF = 512
Q_BLOCK = 128
EPS = 1e-6

N_PROMPT = BATCH * SEQ
N_SAMPLE = DEC_BATCH * DEC_SEQ
N_TOK = N_PROMPT + N_SAMPLE
LANES = 128
MOD_ROWS = 8

C_POOL, C_GQ, C_GK, C_GV, C_GG = 0, 1024, 1536, 2048, 3072
C_DQ, C_DK, C_DV, C_NQ, C_NK, C_NV, C_GD = 4096, 5120, 6144, 7168, 8192, 9216, 10240
D_IN_PAD = 10752
ORIG_GD = 3 * GROUP_W + 2 * GLA_HEADS * GLA_DK

TM = 512
TN_IN = 768
TN_OUT = 1024
TM_E = 256
N_ASSIGN = N_TOK * MOE_TOPK
N_ETILES = N_ASSIGN // TM_E + MOE_EXPERTS
VMEM_LIMIT = 56 * 1024 * 1024


def _cparams(sem):
    return pltpu.CompilerParams(dimension_semantics=sem, vmem_limit_bytes=VMEM_LIMIT)


def _mod_row(i):
    npt = N_PROMPT // TM
    return jnp.where(i < npt, 0, 1 + (i - npt) // (DEC_SEQ // TM))


def _ada_kernel(c_ref, w_ref, b_ref, o_ref):
    @pl.when(pl.program_id(2) == 0)
    def _():
        o_ref[0] = jnp.broadcast_to(b_ref[0], o_ref.shape[1:])

    c = c_ref[...]
    a = (c * jax.nn.sigmoid(c)).astype(bf16)
    o_ref[0] += jnp.dot(a, w_ref[0].astype(bf16), preferred_element_type=f32)


def ada_modulation(cvec, w_ada, b_ada):
    tk, tn = 2048, 1024
    n6 = 6 * D_MODEL
    return pl.pallas_call(
        _ada_kernel,
        grid=(DEPTH, n6 // tn, D_MODEL // tk),
        in_specs=[pl.BlockSpec((MOD_ROWS, tk), lambda l, j, k: (0, k)),
                  pl.BlockSpec((1, tk, tn), lambda l, j, k: (l, k, j)),
                  pl.BlockSpec((1, 1, tn), lambda l, j, k: (l, 0, j))],
        out_specs=pl.BlockSpec((1, MOD_ROWS, tn), lambda l, j, k: (l, 0, j)),
        out_shape=jax.ShapeDtypeStruct((DEPTH, MOD_ROWS, n6), f32),
        compiler_params=_cparams(("parallel", "parallel", "arbitrary")),
        name="ada_modulation",
    )(cvec, w_ada, b_ada.reshape(DEPTH, 1, n6))


def _modulated_norm(x, g, shift, scale):
    y = x * lax.rsqrt(jnp.mean(x * x, axis=-1, keepdims=True) + EPS) * g
    return y * (1.0 + scale) + shift


def _win_kernel(x_ref, g_ref, mod_ref, w_ref, o_ref, h_ref):
    @pl.when(pl.program_id(1) == 0)
    def _():
        h = _modulated_norm(x_ref[...], g_ref[...], mod_ref[0, 0:1, :], mod_ref[0, 1:2, :])
        h_ref[...] = h.astype(bf16)

    o_ref[...] = jnp.dot(h_ref[...], w_ref[...], preferred_element_type=f32)


def input_projection(x, g_norm, mod, w):
    return pl.pallas_call(
        _win_kernel,
        grid=(N_TOK // TM, D_IN_PAD // TN_IN),
        in_specs=[pl.BlockSpec((TM, D_MODEL), lambda i, j: (i, 0)),
                  pl.BlockSpec((1, D_MODEL), lambda i, j: (0, 0)),
                  pl.BlockSpec((1, 6, D_MODEL), lambda i, j: (_mod_row(i), 0, 0)),
                  pl.BlockSpec((D_MODEL, TN_IN), lambda i, j: (0, j))],
        out_specs=pl.BlockSpec((TM, TN_IN), lambda i, j: (i, j)),
        out_shape=jax.ShapeDtypeStruct((N_TOK, D_IN_PAD), f32),
        scratch_shapes=[pltpu.VMEM((TM, D_MODEL), bf16)],
        compiler_params=_cparams(("parallel", "arbitrary")),
        name="input_projection",
    )(x, g_norm.reshape(1, D_MODEL), mod, w)


def _wout_kernel(a0, a1, a2, a3, w0, w1, w2, w3, x_ref, mod_ref, o_ref):
    acc = jnp.dot(a0[...], w0[...], preferred_element_type=f32)
    acc += jnp.dot(a1[...], w1[...], preferred_element_type=f32)
    acc += jnp.dot(a2[...], w2[...], preferred_element_type=f32)
    acc += jnp.dot(a3[...], w3[...], preferred_element_type=f32)
    o_ref[...] = x_ref[...] + mod_ref[0, 2:3, :] * acc


def output_projection(parts, w, x, mod):
    a_specs = [pl.BlockSpec((TM, GROUP_W), lambda i, j: (i, 0)) for _ in range(4)]
    w_specs = [pl.BlockSpec((GROUP_W, TN_OUT), functools.partial(lambda i, j, c: (c, j), c=c))
               for c in range(4)]
    return pl.pallas_call(
        _wout_kernel,
        grid=(N_TOK // TM, D_MODEL // TN_OUT),
        in_specs=a_specs + w_specs + [
            pl.BlockSpec((TM, TN_OUT), lambda i, j: (i, j)),
            pl.BlockSpec((1, 6, TN_OUT), lambda i, j: (_mod_row(i), 0, j))],
        out_specs=pl.BlockSpec((TM, TN_OUT), lambda i, j: (i, j)),
        out_shape=jax.ShapeDtypeStruct((N_TOK, D_MODEL), f32),
        compiler_params=_cparams(("parallel", "arbitrary")),
        name="output_projection",
    )(*parts, w, w, w, w, x, mod)


def _split_bf16(a):
    hi = a.astype(bf16)
    lo = (a - hi.astype(f32)).astype(bf16)
    return hi, lo


def _router_kernel(x_ref, g_ref, mod_ref, wr_ref, br_ref, h_ref, r_ref, cnt_ref):
    h = _modulated_norm(x_ref[...], g_ref[...], mod_ref[0, 3:4, :], mod_ref[0, 4:5, :])
    h_ref[...] = h.astype(bf16)
    h_hi, h_lo = _split_bf16(h)
    w_hi, w_lo = _split_bf16(wr_ref[...])
    lg = (jnp.dot(h_hi, w_hi, preferred_element_type=f32)
          + jnp.dot(h_lo, w_hi, preferred_element_type=f32)
          + jnp.dot(h_hi, w_lo, preferred_element_type=f32)) + br_ref[...]
    lane = lax.broadcasted_iota(jnp.int32, lg.shape, 1).astype(f32)
    ninf = -jnp.inf

    def first_max(v):
        m = jnp.max(v, axis=-1, keepdims=True)
        idx = jnp.min(jnp.where(v == m, lane, float(LANES)), axis=-1, keepdims=True)
        return m, idx

    gmask = lane < MOE_GROUPS
    mg, gi = first_max(jnp.where(gmask, lg, ninf))
    p_top = 1.0 / jnp.sum(jnp.where(gmask, jnp.exp(lg - mg), 0.0), axis=-1, keepdims=True)
    lo = MOE_GROUPS + gi * MOE_PER_GROUP
    le = jnp.where((lane >= lo) & (lane < lo + MOE_PER_GROUP), lg, ninf)
    v1, i1 = first_max(le)
    v2, i2 = first_max(jnp.where(lane == i1, ninf, le))
    t = jnp.exp(v2 - v1)
    w1 = p_top / (1.0 + t)
    w2 = w1 * t

    @pl.when(pl.program_id(0) == 0)
    def _():
        cnt_ref[...] = jnp.zeros(cnt_ref.shape, f32)

    tm = lg.shape[0]
    chosen = jnp.where((lane == i1) | (lane == i2), 1.0, 0.0)
    earlier = (lax.broadcasted_iota(jnp.int32, (tm, tm), 1) < lax.broadcasted_iota(jnp.int32, (tm, tm), 0))
    before = jnp.dot(jnp.where(earlier, 1.0, 0.0).astype(bf16), chosen.astype(bf16),
                     preferred_element_type=f32) + cnt_ref[0:1, :]
    rank1 = jnp.sum(jnp.where(lane == i1, before, 0.0), axis=-1, keepdims=True)
    rank2 = jnp.sum(jnp.where(lane == i2, before, 0.0), axis=-1, keepdims=True)
    cnt_ref[...] = cnt_ref[...] + jnp.sum(chosen, axis=0, keepdims=True)

    slab = jnp.zeros(lg.shape, f32)
    for pos, val in enumerate((i1 - MOE_GROUPS, i2 - MOE_GROUPS, w1, w2, rank1, rank2)):
        slab = jnp.where(lane == float(pos), val, slab)
    r_ref[...] = slab


def moe_router(x, g_norm, mod, wr, br):
    tm = 256
    return pl.pallas_call(
        _router_kernel,
        grid=(N_TOK // tm,),
        in_specs=[pl.BlockSpec((tm, D_MODEL), lambda i: (i, 0)),
                  pl.BlockSpec((1, D_MODEL), lambda i: (0, 0)),
                  pl.BlockSpec((1, 6, D_MODEL), lambda i: (_mod_row(i // (TM // tm)), 0, 0)),
                  pl.BlockSpec((D_MODEL, LANES), lambda i: (0, 0)),
                  pl.BlockSpec((1, LANES), lambda i: (0, 0))],
        out_specs=[pl.BlockSpec((tm, D_MODEL), lambda i: (i, 0)),
                   pl.BlockSpec((tm, LANES), lambda i: (i, 0)),
                   pl.BlockSpec((MOD_ROWS, LANES), lambda i: (0, 0))],
        out_shape=[jax.ShapeDtypeStruct((N_TOK, D_MODEL), bf16),
                   jax.ShapeDtypeStruct((N_TOK, LANES), f32),
                   jax.ShapeDtypeStruct((MOD_ROWS, LANES), f32)],
        compiler_params=_cparams(("arbitrary",)),
        name="moe_router",
    )(x, g_norm.reshape(1, D_MODEL), mod, wr, br)


def _ffn_kernel(te_ref, nu_ref, x_ref, wg_ref, wu_ref, wd_ref, o_ref):
    @pl.when(pl.program_id(0) < nu_ref[0])
    def _():
        x = x_ref[...]
        a = jnp.dot(x, wg_ref[0], preferred_element_type=f32)
        u = jnp.dot(x, wu_ref[0], preferred_element_type=f32)
        mid = (a * jax.nn.sigmoid(a) * u).astype(bf16)
        o_ref[...] = jnp.dot(mid, wd_ref[0], preferred_element_type=f32)


def expert_ffn(tile_expert, n_used, xs, wg, wu, wd):
    row_blk = lambda t, te, nu: (jnp.minimum(t, nu[0] - 1), 0)
    return pl.pallas_call(
        _ffn_kernel,
        grid_spec=pltpu.PrefetchScalarGridSpec(
            num_scalar_prefetch=2,
            grid=(N_ETILES,),
            in_specs=[pl.BlockSpec((TM_E, D_MODEL), row_blk),
                      pl.BlockSpec((1, D_MODEL, MOE_FF), lambda t, te, nu: (te[t], 0, 0)),
                      pl.BlockSpec((1, D_MODEL, MOE_FF), lambda t, te, nu: (te[t], 0, 0)),
                      pl.BlockSpec((1, MOE_FF, D_MODEL), lambda t, te, nu: (te[t], 0, 0))],
            out_specs=pl.BlockSpec((TM_E, D_MODEL), row_blk)),
        out_shape=jax.ShapeDtypeStruct((N_ETILES * TM_E, D_MODEL), f32),
        compiler_params=_cparams(("arbitrary",)),
        name="expert_ffn",
    )(tile_expert, n_used, xs, wg, wu, wd)


def route_layout(route, counts):
    e = route[:, :MOE_TOPK].astype(jnp.int32)
    rank = route[:, 4:4 + MOE_TOPK].astype(jnp.int32)
    cnt = counts[0, MOE_GROUPS:MOE_GROUPS + MOE_EXPERTS].astype(jnp.int32)
    tiles_e = (cnt + TM_E - 1) // TM_E
    tile_end = jnp.cumsum(tiles_e)
    tile_start = tile_end - tiles_e
    pos = tile_start[e] * TM_E + rank
    slot_token = jnp.zeros((N_ETILES * TM_E,), jnp.int32).at[pos.reshape(-1)].set(
        jnp.arange(N_ASSIGN, dtype=jnp.int32) // MOE_TOPK)
    n_used = tile_end[-1]
    t = jnp.minimum(jnp.arange(N_ETILES, dtype=jnp.int32), n_used - 1)
    tile_expert = jnp.sum((t[:, None] >= tile_end[None, :]).astype(jnp.int32), axis=1)
    tile_expert = jnp.minimum(tile_expert, MOE_EXPERTS - 1)
    return pos, slot_token, tile_expert, n_used.reshape(1).astype(jnp.int32)


def _combine_kernel(x_ref, y0_ref, y1_ref, r_ref, mod_ref, o_ref):
    y = r_ref[:, 2:3] * y0_ref[...] + r_ref[:, 3:4] * y1_ref[...]
    o_ref[...] = x_ref[...] + mod_ref[0, 5:6, :] * y


def _combine_norm_kernel(x_ref, y0_ref, y1_ref, r_ref, mod_ref, g_ref, o_ref):
    y = r_ref[:, 2:3] * y0_ref[...] + r_ref[:, 3:4] * y1_ref[...]
    x = x_ref[...] + mod_ref[0, 5:6, :] * y
    o_ref[...] = x * lax.rsqrt(jnp.mean(x * x, axis=-1, keepdims=True) + EPS) * g_ref[...]


def moe_combine(x, y0, y1, route, mod, g_final=None):
    tm = 256
    row = pl.BlockSpec((tm, D_MODEL), lambda i: (i, 0))
    in_specs = [row, row, row, pl.BlockSpec((tm, LANES), lambda i: (i, 0)),
                pl.BlockSpec((1, 6, D_MODEL), lambda i: (_mod_row(i // (TM // tm)), 0, 0))]
    args = [x, y0, y1, route, mod]
    body = _combine_kernel
    if g_final is not None:
        in_specs.append(pl.BlockSpec((1, D_MODEL), lambda i: (0, 0)))
        args.append(g_final.reshape(1, D_MODEL))
        body = _combine_norm_kernel
    return pl.pallas_call(
        body,
        grid=(N_TOK // tm,),
        in_specs=in_specs,
        out_specs=row,
        out_shape=jax.ShapeDtypeStruct((N_TOK, D_MODEL), f32),
        compiler_params=_cparams(("parallel",)),
        name="moe_combine",
    )(*args)


def _diff_prep_kernel(rope, q_ref, k_ref, cos_ref, sin_ref, q2_ref, kb_ref):
    lane = lax.broadcasted_iota(jnp.int32, q_ref.shape, 1)

    def rot(x):
        if not rope:
            return x
        partner = jnp.where((lane & 31) >= 16, pltpu.roll(x, 16, 1), pltpu.roll(x, LANES - 16, 1))
        return x * cos_ref[...] + partner * sin_ref[...]

    q = rot(q_ref[...]) * DIFF_HD ** -0.5
    first = lane < DIFF_HD
    q2_ref[0] = jnp.where(first, q, 0.0).astype(bf16)
    q2_ref[1] = jnp.where(first, 0.0, q).astype(bf16)
    kb_ref[...] = rot(k_ref[...]).astype(bf16)


def _rope_tables():
    t = jnp.arange(DEC_SEQ)
    row = (t // GRID_W).astype(f32)
    col = (t % GRID_W).astype(f32)
    nf = DIFF_HD // 4
    inv = ROPE_BASE ** (-jnp.arange(nf, dtype=f32) / nf)
    lane = np.arange(LANES)
    pos = jnp.where(((lane // 32) % 2 == 0)[None, :], row[:, None], col[:, None])
    ang = pos * inv[lane % nf][None, :]
    sign = np.where(lane % 32 >= nf, 1.0, -1.0).astype(np.float32)
    return jnp.cos(ang), jnp.sin(ang) * sign[None, :]


def diff_prep(u, row0, n_rows, rope):
    t = 1024
    r0 = row0 // t
    nseq = DEC_SEQ // t
    cos, sin = _rope_tables() if rope else (jnp.zeros((t, LANES), f32), jnp.zeros((t, LANES), f32))
    tab = pl.BlockSpec((t, LANES), (lambda i, h: (i % nseq, 0)) if rope else (lambda i, h: (0, 0)))
    return pl.pallas_call(
        functools.partial(_diff_prep_kernel, rope),
        grid=(n_rows // t, DIFF_HEADS),
        in_specs=[pl.BlockSpec((t, LANES), lambda i, h: (r0 + i, C_DQ // LANES + h)),
                  pl.BlockSpec((t, LANES), lambda i, h: (r0 + i, C_DK // LANES + h)),
                  tab, tab],
        out_specs=[pl.BlockSpec((2, t, LANES), lambda i, h: (0, i, h)),
                   pl.BlockSpec((t, LANES), lambda i, h: (i, h))],
        out_shape=[jax.ShapeDtypeStruct((2, n_rows, GROUP_W), bf16),
                   jax.ShapeDtypeStruct((n_rows, GROUP_W), bf16)],
        compiler_params=_cparams(("parallel", "parallel")),
        name="diff_prep",
    )(u, u, cos, sin)


def _flash_kernel(n_stack, tq, sub, scale, out_scale, v_transposed, q_ref, k_ref, v_ref, lam_ref, g_ref, o_ref,
                  m_ref, l_ref, acc_ref):
    kj = pl.program_id(3)

    @pl.when(kj == 0)
    def _():
        m_ref[...] = jnp.full(m_ref.shape, -jnp.inf, f32)
        l_ref[...] = jnp.zeros(l_ref.shape, f32)
        acc_ref[...] = jnp.zeros(acc_ref.shape, f32)

    q = q_ref[...].reshape(n_stack * tq, LANES).astype(bf16)
    tk = k_ref.shape[1]
    m, l, acc = m_ref[...], l_ref[...], acc_ref[...]
    for c in range(tk // sub):
        keys = slice(c * sub, (c + 1) * sub)
        s = lax.dot_general(k_ref[0, keys, :].astype(bf16), q, (((1,), (1,)), ((), ())),
                            preferred_element_type=f32)
        if scale != 1.0:
            s = s * scale
        m_new = jnp.maximum(m, jnp.max(s, axis=0, keepdims=True))
        alpha = jnp.exp(m - m_new)
        p = jnp.exp(s - m_new)
        l = alpha * l + jnp.sum(p, axis=0, keepdims=True)
        vt = v_ref[0, :, keys].astype(bf16) if v_transposed else v_ref[0, keys, :].T.astype(bf16)
        acc = alpha * acc + jnp.dot(vt, p.astype(bf16), preferred_element_type=f32)
        m = m_new
    m_ref[...], l_ref[...], acc_ref[...] = m, l, acc

    @pl.when(kj == pl.num_programs(3) - 1)
    def _():
        o = acc_ref[...] / l_ref[...]
        if n_stack == 2:
            o = o[:, :tq] - lam_ref[0:1, 0:1] * o[:, tq:]
            o = o * lax.rsqrt(jnp.mean(o * o, axis=0, keepdims=True) + EPS)
            o = o.T * (g_ref[...] * out_scale)
        else:
            o = o.T
        o_ref[...] = o.astype(o_ref.dtype)


def flash_attention(q, k, v, *, n_batch, lq, tq, tk, qcol, kcol, vcol, sub=512, v_transposed=False, qrow0=0, kb0=0,
                    scale=1.0, lam=None, g=None, out_scale=1.0):
    n_stack = q.shape[0]
    lk = k.shape[1]
    nq = lq // tq
    lam = jnp.zeros((1, LANES), f32) if lam is None else jnp.full((1, LANES), lam, f32)
    g = jnp.ones((1, LANES), f32) if g is None else g.reshape(1, LANES).astype(f32)
    rows = n_stack * tq
    if v_transposed:
        v_spec = pl.BlockSpec((1, LANES, tk), lambda b, h, i, j: (kb0 + b, vcol + h, j))
    else:
        v_spec = pl.BlockSpec((1, tk, LANES), lambda b, h, i, j: (kb0 + b, j, vcol + h))
    return pl.pallas_call(
        functools.partial(_flash_kernel, n_stack, tq, min(sub, tk), scale, out_scale, v_transposed),
        grid=(n_batch, DIFF_HEADS, nq, lk // tk),
        in_specs=[pl.BlockSpec((n_stack, tq, LANES), lambda b, h, i, j: (0, qrow0 + b * nq + i, qcol + h)),
                  pl.BlockSpec((1, tk, LANES), lambda b, h, i, j: (kb0 + b, j, kcol + h)),
                  v_spec,
                  pl.BlockSpec((1, LANES), lambda b, h, i, j: (0, 0)),
                  pl.BlockSpec((1, LANES), lambda b, h, i, j: (0, 0))],
        out_specs=pl.BlockSpec((tq, LANES), lambda b, h, i, j: (b * nq + i, h)),
        out_shape=jax.ShapeDtypeStruct((n_batch * lq, GROUP_W), bf16),
        scratch_shapes=[pltpu.VMEM((1, rows), f32), pltpu.VMEM((1, rows), f32),
                        pltpu.VMEM((LANES, rows), f32)],
        compiler_params=_cparams(("parallel", "parallel", "parallel", "arbitrary")),
        name="flash_attention",
    )(q, k, v, lam, g)


NAT_ROWS = DEC_SEQ // GRID_W
NAT_RB = 8
NAT_BAND = NAT_WH * GRID_W
NEG_BIG = -1e30


def _nat_kernel(q_ref, k_ref, v_ref, kc_ref, vc_ref, bias_ref, o_ref):
    rb = pl.program_id(2)
    scale = NAT_HD ** -0.5
    kc = kc_ref[0].astype(bf16)
    vc = vc_ref[0].astype(bf16)
    nt = (((1,), (1,)), ((), ()))
    for i in range(NAT_RB):
        r = rb * NAT_RB + i
        rs = jnp.clip(r - NAT_WH // 2, 0, NAT_ROWS - NAT_WH)
        k0 = pl.multiple_of(rs * GRID_W, GRID_W)
        kband = k_ref[pl.ds(k0, NAT_BAND), :].astype(bf16)
        vband = v_ref[pl.ds(k0, NAT_BAND), :].astype(bf16)
        q = q_ref[i * GRID_W:(i + 1) * GRID_W, :].astype(bf16)
        sb = lax.dot_general(q, kband, nt, preferred_element_type=f32) * scale + bias_ref[rs - r + NAT_WH - 1, 0]
        sc = lax.dot_general(q, kc, nt, preferred_element_type=f32) * scale
        m = jnp.maximum(jnp.max(sb, axis=-1, keepdims=True), jnp.max(sc, axis=-1, keepdims=True))
        pb = jnp.exp(sb - m)
        pc = jnp.exp(sc - m)
        l = jnp.sum(pb, axis=-1, keepdims=True) + jnp.sum(pc, axis=-1, keepdims=True)
        o = (jnp.dot(pb.astype(bf16), vband, preferred_element_type=f32)
             + jnp.dot(pc.astype(bf16), vc, preferred_element_type=f32)) / l
        o_ref[i * GRID_W:(i + 1) * GRID_W, :] = o.astype(o_ref.dtype)


def _nat_bias_table(rpb):
    colv = np.arange(GRID_W)
    cstart = np.clip(colv - NAT_WW // 2, 0, GRID_W - NAT_WW)
    col_mask = (colv[None, :] >= cstart[:, None]) & (colv[None, :] < cstart[:, None] + NAT_WW)
    col_idx = np.clip(colv[None, :] - colv[:, None] + NAT_WW - 1, 0, 2 * NAT_WW - 2)
    rpb_c = rpb.astype(f32)[:, :, col_idx]
    row_idx = np.arange(NAT_WH)[:, None] + np.arange(NAT_WH)[None, :]
    tab = rpb_c[:, row_idx]
    tab = jnp.where(col_mask[None, None, None], tab, NEG_BIG)
    return tab.transpose(1, 0, 3, 2, 4).reshape(NAT_WH, NAT_HEADS, GRID_W, NAT_BAND)


def nat_attention(u, row0, kc, vc, rpb):
    qblk = NAT_RB * GRID_W
    q0 = row0 // qblk
    b0 = row0 // DEC_SEQ
    return pl.pallas_call(
        _nat_kernel,
        grid=(DEC_BATCH, NAT_HEADS, NAT_ROWS // NAT_RB),
        in_specs=[pl.BlockSpec((qblk, LANES), lambda b, h, r: (q0 + b * (NAT_ROWS // NAT_RB) + r, C_NQ // LANES + h)),
                  pl.BlockSpec((DEC_SEQ, LANES), lambda b, h, r: (b0 + b, C_NK // LANES + h)),
                  pl.BlockSpec((DEC_SEQ, LANES), lambda b, h, r: (b0 + b, C_NV // LANES + h)),
                  pl.BlockSpec((1, PAST_LEN, LANES), lambda b, h, r: (b, 0, h)),
                  pl.BlockSpec((1, PAST_LEN, LANES), lambda b, h, r: (b, 0, h)),
                  pl.BlockSpec((NAT_WH, 1, GRID_W, NAT_BAND), lambda b, h, r: (0, h, 0, 0))],
        out_specs=pl.BlockSpec((qblk, LANES), lambda b, h, r: (b * (NAT_ROWS // NAT_RB) + r, h)),
        out_shape=jax.ShapeDtypeStruct((N_SAMPLE, GROUP_W), bf16),
        compiler_params=_cparams(("parallel", "parallel", "arbitrary")),
        name="nat_attention",
    )(u, u, u, kc, vc, _nat_bias_table(rpb))


def _pool_kernel(seq, u_ref, w_ref, s_ref, o_ref):
    grp = pl.program_id(1)
    for gi, win in enumerate(POOL_WINDOWS):
        @pl.when(grp == gi)
        def _(win=win):
            u = u_ref[...]
            t = lax.broadcasted_iota(jnp.int32, u.shape, 0)
            acc = jnp.zeros_like(u)
            for d in range(-(win // 2), win // 2):
                shifted = u if d == 0 else pltpu.roll(u, (-d) % seq, 0)
                acc += jnp.where((t + d >= 0) & (t + d < seq), shifted, 0.0)
            cnt = (jnp.minimum(t + win // 2, seq) - jnp.maximum(t - win // 2, 0)).astype(f32)
            p = acc / cnt - u
            y = jnp.dot(p.astype(bf16), w_ref[0].astype(bf16), preferred_element_type=f32) * s_ref[...]
            o_ref[...] = y.astype(o_ref.dtype)


def pool_mixer(u, row0, n_seq, seq, w, s):
    return pl.pallas_call(
        functools.partial(_pool_kernel, seq),
        grid=(n_seq, len(POOL_WINDOWS)),
        in_specs=[pl.BlockSpec((seq, POOL_GW), lambda b, g: (row0 // seq + b, g)),
                  pl.BlockSpec((1, POOL_GW, POOL_GW), lambda b, g: (g, 0, 0)),
                  pl.BlockSpec((1, POOL_GW), lambda b, g: (0, g))],
        out_specs=pl.BlockSpec((seq, POOL_GW), lambda b, g: (b, g)),
        out_shape=jax.ShapeDtypeStruct((n_seq * seq, GROUP_W), bf16),
        compiler_params=_cparams(("parallel", "parallel")),
        name="pool_mixer",
    )(u, w, s.reshape(1, GROUP_W))


GLA_QK = GLA_HEADS * GLA_DK


def _gla_kernel(reverse, final, n_chunks, *refs):
    if final:
        (q_ref, k_ref, v_ref, gd_ref, wup_ref, bup_ref, s0_ref, op_ref, gg_ref, gn_ref,
         o_ref, s_ref, st_ref) = refs
    else:
        q_ref, k_ref, v_ref, gd_ref, wup_ref, bup_ref, s0_ref, o_ref, s_ref, st_ref = refs
    j = pl.program_id(1)
    C = GLA_CHUNK

    @pl.when(j == 0)
    def _():
        for h in range(GLA_HEADS):
            s0 = s0_ref[0, h]
            z = jnp.zeros_like(s0)
            st_ref[h] = jnp.concatenate([s0, z] if h % 2 == 0 else [z, s0], axis=0).T

    ti = lax.broadcasted_iota(jnp.int32, (C, C), 0)
    tj = lax.broadcasted_iota(jnp.int32, (C, C), 1)
    tri = (ti <= tj) if reverse else (ti >= tj)
    tri_b = jnp.where(tri, 1.0, 0.0).astype(bf16)
    first_half = lax.broadcasted_iota(jnp.int32, (C, LANES), 1) < GLA_DK
    ref_row = C // 2 - 1 if reverse else C // 2
    end_row = 0 if reverse else C - 1
    nt = (((1,), (1,)), ((), ()))

    def chunk(ci, carry):
        c = n_chunks - 1 - ci if reverse else ci
        rows = pl.ds(pl.multiple_of(c * C, C), C)
        logit = jnp.dot(gd_ref[rows, :].astype(bf16), wup_ref[...], preferred_element_type=f32) + bup_ref[...]
        la = (jnp.minimum(logit, 0.0) - jnp.log1p(jnp.exp(-jnp.abs(logit)))) * (1.0 / GLA_GATE_NORM)
        hi = la.astype(bf16)
        r1 = la - hi.astype(f32)
        mid = r1.astype(bf16)
        lo = (r1 - mid.astype(f32)).astype(bf16)
        b = (jnp.dot(tri_b, hi, preferred_element_type=f32) + jnp.dot(tri_b, mid, preferred_element_type=f32)
             + jnp.dot(tri_b, lo, preferred_element_type=f32))
        bref = b[ref_row:ref_row + 1, :]
        bl = b[end_row:end_row + 1, :]
        q = q_ref[rows, :] * GLA_DK ** -0.5
        k = k_ref[rows, :]
        qs = q * jnp.exp(b - bref)
        ks = (k * jnp.exp(bref - b)).astype(bf16)
        qe = (q * jnp.exp(b)).astype(bf16)
        kd = k * jnp.exp(bl - b)
        ebl = jnp.exp(bl)
        for h in range(GLA_HEADS):
            pair = slice((h // 2) * LANES, (h // 2 + 1) * LANES)
            mine = first_half if h % 2 == 0 else jnp.logical_not(first_half)
            cols = slice(h * GLA_DV, (h + 1) * GLA_DV)
            a = lax.dot_general(jnp.where(mine, qs[:, pair], 0.0).astype(bf16), ks[:, pair], nt,
                                preferred_element_type=f32)
            a = jnp.where(tri, a, 0.0).astype(bf16)
            vh = v_ref[rows, cols]
            st = st_ref[h]
            o = (jnp.dot(a, vh.astype(bf16), preferred_element_type=f32)
                 + lax.dot_general(qe[:, pair], st.astype(bf16), nt, preferred_element_type=f32))
            st_ref[h] = st * ebl[:, pair] + jnp.dot(vh.T.astype(bf16),
                                                    jnp.where(mine, kd[:, pair], 0.0).astype(bf16),
                                                    preferred_element_type=f32)
            if final:
                o = o + op_ref[rows, cols]
                o = o * lax.rsqrt(jnp.mean(o * o, axis=-1, keepdims=True) + EPS) * gn_ref[...]
                gate = gg_ref[rows, cols]
                o = o * (gate * jax.nn.sigmoid(gate))
            o_ref[rows, cols] = o.astype(o_ref.dtype)
        return carry

    lax.fori_loop(0, n_chunks, chunk, 0)

    @pl.when(j == pl.num_programs(1) - 1)
    def _():
        for h in range(GLA_HEADS):
            half = (h % 2) * GLA_DK
            s_ref[0, h] = st_ref[h].T[half:half + GLA_DK, :]


def _gla_pass(u, row0, n_seq, seq, tb, z, w_up, b_up, s0, o_prev=None, g_norm=None):
    reverse = z == 1
    final = o_prev is not None
    nblk = seq // tb
    rb0 = row0 // tb
    wup = jnp.zeros((LANES, GLA_QK), f32).at[z * GLA_RANK:(z + 1) * GLA_RANK].set(w_up[z]).astype(bf16)

    def blk(b, j):
        return b * nblk + (nblk - 1 - j if reverse else j)

    def ucol(width, c0):
        return pl.BlockSpec((tb, width), lambda b, j: (rb0 + blk(b, j), c0 // width))

    in_specs = [ucol(GLA_QK, C_GQ), ucol(GLA_QK, C_GK), ucol(GROUP_W, C_GV), ucol(LANES, C_GD),
                pl.BlockSpec((LANES, GLA_QK), lambda b, j: (0, 0)),
                pl.BlockSpec((1, GLA_QK), lambda b, j: (0, 0)),
                pl.BlockSpec((1, GLA_HEADS, GLA_DK, GLA_DV), lambda b, j: (b, 0, 0, 0))]
    args = [u, u, u, u, wup, b_up[z].reshape(1, GLA_QK), s0]
    if final:
        in_specs += [pl.BlockSpec((tb, GROUP_W), lambda b, j: (blk(b, j), 0)), ucol(GROUP_W, C_GG),
                     pl.BlockSpec((1, GLA_DV), lambda b, j: (0, 0))]
        args += [o_prev, u, g_norm.reshape(1, GLA_DV)]
    return pl.pallas_call(
        functools.partial(_gla_kernel, reverse, final, tb // GLA_CHUNK),
        grid=(n_seq, nblk),
        in_specs=in_specs,
        out_specs=[pl.BlockSpec((tb, GROUP_W), lambda b, j: (blk(b, j), 0)),
                   pl.BlockSpec((1, GLA_HEADS, GLA_DK, GLA_DV), lambda b, j: (b, 0, 0, 0))],
        out_shape=[jax.ShapeDtypeStruct((n_seq * seq, GROUP_W), bf16 if final else f32),
                   jax.ShapeDtypeStruct((n_seq, GLA_HEADS, GLA_DK, GLA_DV), f32)],
        scratch_shapes=[pltpu.VMEM((GLA_HEADS, GLA_DV, LANES), f32)],
        compiler_params=_cparams(("parallel", "arbitrary")),
        name="gla_backward" if reverse else "gla_forward",
    )(*args)


def gla_mixer(u, row0, n_seq, seq, tb, w_up, b_up, g_norm, s0_f, s0_b):
    o_f, s_f = _gla_pass(u, row0, n_seq, seq, tb, 0, w_up, b_up, s0_f)
    o, s_b = _gla_pass(u, row0, n_seq, seq, tb, 1, w_up, b_up, s0_b, o_prev=o_f, g_norm=g_norm)
    return o, jnp.stack([s_f, s_b], axis=1)


def _head_rms(o, g):
    return o * lax.rsqrt(jnp.mean(o * o, axis=-1, keepdims=True) + EPS) * g.astype(f32)


def _pool_mixer(u, w, s):
    B, L, _ = u.shape
    cs = jnp.concatenate([jnp.zeros((B, 1, GROUP_W), u.dtype), jnp.cumsum(u, axis=1)], axis=1)
    t = jnp.arange(L)
    parts = []
    for i, win in enumerate(POOL_WINDOWS):
        lo = jnp.clip(t - win // 2, 0, L)
        hi = jnp.clip(t + win // 2, 0, L)
        cg = cs[..., i * POOL_GW:(i + 1) * POOL_GW]
        mean = (cg[:, hi] - cg[:, lo]) / (hi - lo).astype(u.dtype)[None, :, None]
        parts.append(mean - u[..., i * POOL_GW:(i + 1) * POOL_GW])
    p = jnp.stack(parts, axis=2)
    y = jnp.einsum('blgc,gcd->blgd', p, w.astype(u.dtype)).reshape(B, L, GROUP_W)
    return y * s.astype(u.dtype)


def _gla_scan(q, k, v, la, s0):
    B, L, H, _ = q.shape
    C = GLA_CHUNK
    n = L // C
    mask = jnp.tril(jnp.ones((C, C), dtype=bool))

    def chunks(a):
        return jnp.moveaxis(a.reshape(B, n, C, H, a.shape[-1]), 1, 0)

    def step(S, inp):
        qc, kc, vc, lac = inp
        b = jnp.cumsum(lac, axis=1)
        ref = b[:, C // 2:C // 2 + 1]
        a = jnp.einsum('bihd,bjhd->bhij', qc * jnp.exp(b - ref), kc * jnp.exp(ref - b))
        a = jnp.where(mask, a, 0.0)
        o = (jnp.einsum('bhij,bjhv->bihv', a, vc)
             + jnp.einsum('bihd,bhdv->bihv', qc * jnp.exp(b), S))
        bl = b[:, -1]
        S = (jnp.exp(bl)[..., None] * S
             + jnp.einsum('bjhd,bjhv->bhdv', kc * jnp.exp(bl[:, None] - b), vc))
        return S, o

    S, o = lax.scan(step, s0, (chunks(q), chunks(k), chunks(v), chunks(la)))
    return jnp.moveaxis(o, 0, 1).reshape(B, L, H, v.shape[-1]), S


def _gla_mixer(q_in, k_in, v_in, g_in, gd_in, w_up, b_up, g_norm, s0_f, s0_b):
    B, L, _ = q_in.shape
    q = q_in.reshape(B, L, GLA_HEADS, GLA_DK) * GLA_DK ** -0.5
    k = k_in.reshape(B, L, GLA_HEADS, GLA_DK)
    v = v_in.reshape(B, L, GLA_HEADS, GLA_DV)
    gd = gd_in.reshape(B, L, 2, GLA_RANK)
    logit = jnp.einsum('blzr,zre->blze', gd, w_up.astype(f32)) + b_up.astype(f32)
    la = (jax.nn.log_sigmoid(logit) / GLA_GATE_NORM).reshape(B, L, 2, GLA_HEADS, GLA_DK)
    o_f, s_f = _gla_scan(q, k, v, la[:, :, 0], s0_f)
    rev = lambda a: jnp.flip(a, axis=1)
    o_b, s_b = _gla_scan(rev(q), rev(k), rev(v), rev(la[:, :, 1]), s0_b)
    o = _head_rms(o_f + rev(o_b), g_norm).reshape(B, L, GROUP_W) * jax.nn.silu(g_in)
    return o, jnp.stack([s_f, s_b], axis=1)


def _axial_rope(x):
    L = x.shape[1]
    t = jnp.arange(L)
    row = (t // GRID_W).astype(f32)
    col = (t % GRID_W).astype(f32)
    half = DIFF_HD // 2
    nf = half // 2
    inv = ROPE_BASE ** (-jnp.arange(nf, dtype=f32) / nf)

    def rot(xh, pos):
        ang = pos[:, None] * inv
        cos = jnp.cos(ang)[None, :, None, None, :]
        sin = jnp.sin(ang)[None, :, None, None, :]
        x1, x2 = xh[..., :nf], xh[..., nf:]
        return jnp.concatenate([x1 * cos - x2 * sin, x2 * cos + x1 * sin], axis=-1)

    return jnp.concatenate([rot(x[..., :half], row), rot(x[..., half:], col)], axis=-1)


def _diff_attention(q, k, v, lam, lam_init, g_norm):
    B, Lq = q.shape[:2]
    nb = Lq // Q_BLOCK
    qb = jnp.moveaxis(q.reshape(B, nb, Q_BLOCK, DIFF_HEADS, 2, DIFF_HD), 1, 0)

    def blk(qi):
        s = jnp.einsum('bqhcd,bkhcd->bchqk', qi, k) * DIFF_HD ** -0.5
        p = jax.nn.softmax(s, axis=-1)
        a = p[:, 0] - lam * p[:, 1]
        return jnp.einsum('bhqk,bkhv->bqhv', a, v)

    o = jnp.moveaxis(lax.map(blk, qb), 0, 1).reshape(B, Lq, DIFF_HEADS, DIFF_VD)
    return (_head_rms(o, g_norm) * (1.0 - lam_init)).reshape(B, Lq, GROUP_W)


def _dense_attention(q, k, v):
    B, Lq, H, D = q.shape
    nb = Lq // Q_BLOCK
    qb = jnp.moveaxis(q.reshape(B, nb, Q_BLOCK, H, D), 1, 0)

    def blk(qi):
        p = jax.nn.softmax(jnp.einsum('bqhd,bkhd->bhqk', qi, k) * D ** -0.5, axis=-1)
        return jnp.einsum('bhqk,bkhd->bqhd', p, v)

    return jnp.moveaxis(lax.map(blk, qb), 0, 1).reshape(B, Lq, H * D)


def _nat_latent(q, k, v, kc, vc, rpb):
    B, L, H, D = q.shape
    rows = L // GRID_W
    wh = min(NAT_WH, rows)
    qg = q.reshape(B, rows, GRID_W, H, D)
    kg = k.reshape(B, rows, GRID_W, H, D)
    vg = v.reshape(B, rows, GRID_W, H, D)
    colv = np.arange(GRID_W)
    cstart = np.clip(colv - NAT_WW // 2, 0, GRID_W - NAT_WW)
    col_mask = (colv[None, :] >= cstart[:, None]) & (colv[None, :] < cstart[:, None] + NAT_WW)
    col_idx = np.clip(colv[None, :] - colv[:, None] + NAT_WW - 1, 0, 2 * NAT_WW - 2)
    rpb_c = rpb.astype(f32)[:, :, col_idx]
    scale = D ** -0.5
    nkey = wh * GRID_W

    def row(r):
        rs = jnp.clip(r - wh // 2, 0, rows - wh)
        kr = lax.dynamic_slice_in_dim(kg, rs, wh, axis=1)
        vr = lax.dynamic_slice_in_dim(vg, rs, wh, axis=1)
        qr = lax.dynamic_index_in_dim(qg, r, axis=1, keepdims=False)
        row_idx = rs + jnp.arange(wh) - r + (NAT_WH - 1)
        bias = jnp.take(rpb_c, row_idx, axis=1).transpose(0, 2, 1, 3)
        s = jnp.einsum('bqhd,bwkhd->bhqwk', qr, kr) * scale + bias[None]
        s = jnp.where(col_mask[None, None, :, None, :], s, -jnp.inf).reshape(B, H, GRID_W, nkey)
        sc = jnp.einsum('bqhd,bkhd->bhqk', qr, kc) * scale
        p = jax.nn.softmax(jnp.concatenate([s, sc], axis=-1), axis=-1)
        return (jnp.einsum('bhqk,bkhd->bqhd', p[..., :nkey], vr.reshape(B, nkey, H, D))
                + jnp.einsum('bhqk,bkhd->bqhd', p[..., nkey:], vc))

    o = lax.map(row, jnp.arange(rows))
    return jnp.moveaxis(o, 0, 1).reshape(B, L, H * D)


def kernel(x_prompt, x_sample, cache_diff_k, cache_diff_v, cache_nat_k, cache_nat_v, state_gla, c, c_ctx, w_ada, b_ada, norm1, w_in, pool_w, pool_scale, gla_w_up, gla_b_up, gla_norm, diff_lambda, diff_norm, nat_rpb, w_out, norm2, router_group_w, router_group_b, router_expert_w, router_expert_b, expert_w_gate, expert_w_up, expert_w_down, norm_final):
    Bp, Lp, Bs, Ls = BATCH, SEQ, DEC_BATCH, DEC_SEQ
    x = jnp.concatenate([x_prompt.reshape(N_PROMPT, D_MODEL), x_sample.reshape(N_SAMPLE, D_MODEL)], axis=0)
    cvec = jnp.zeros((MOD_ROWS, D_MODEL), f32).at[0].set(c_ctx).at[1:1 + DEC_BATCH].set(c)
    mod_all = ada_modulation(cvec, w_ada, b_ada).reshape(DEPTH, MOD_ROWS, 6, D_MODEL)

    new_dk, new_dv, new_nk, new_nv, new_gs = [], [], [], [], []
    for l in range(DEPTH):
        mod = mod_all[l]
        lam_init = 0.8 - 0.6 * math.exp(-0.3 * l)
        lv = diff_lambda[l].astype(f32)
        lam = jnp.exp(jnp.sum(lv[0] * lv[1])) - jnp.exp(jnp.sum(lv[2] * lv[3])) + lam_init

        w_l = w_in[l]
        w_perm = jnp.concatenate(
            [w_l[:, :ORIG_GD], w_l[:, ORIG_GD + 2 * GLA_RANK:], w_l[:, ORIG_GD:ORIG_GD + 2 * GLA_RANK],
             jnp.zeros((D_MODEL, D_IN_PAD - C_GD - 2 * GLA_RANK), f32)], axis=1).astype(bf16)
        u = input_projection(x, norm1[l], mod, w_perm)

        u_seq = u.reshape(N_TOK // Lp, Lp, D_IN_PAD)
        up = u[:N_PROMPT]

        zero = jnp.zeros((Bp, GLA_HEADS, GLA_DK, GLA_DV), f32)
        o_gla_p, gs = gla_mixer(u, 0, Bp, Lp, Lp, gla_w_up[l], gla_b_up[l], gla_norm[l], zero, zero)
        q2_p, kb_p = diff_prep(u, 0, N_PROMPT, rope=False)
        o_diff_p = flash_attention(q2_p, kb_p.reshape(Bp, Lp, GROUP_W), u_seq, n_batch=Bp, lq=Lp, tq=Lp, tk=Lp,
                                   qcol=0, kcol=0, vcol=C_DV // LANES, lam=lam, g=diff_norm[l],
                                   out_scale=1.0 - lam_init)
        o_nat_p = flash_attention(u.reshape(1, N_TOK, D_IN_PAD), u_seq, u_seq, n_batch=Bp, lq=Lp, tq=Lp, tk=Lp,
                                  qcol=C_NQ // LANES, kcol=C_NK // LANES, vcol=C_NV // LANES,
                                  scale=NAT_HD ** -0.5)
        o_pool_p = pool_mixer(u, 0, Bp, Lp, pool_w[l], pool_scale[l])
        new_dk.append(up[:, C_DK:C_DK + GROUP_W].reshape(Bp, Lp, DIFF_HEADS, 2 * DIFF_HD))
        new_dv.append(up[:, C_DV:C_DV + GROUP_W].reshape(Bp, Lp, DIFF_HEADS, DIFF_VD))
        new_nk.append(up[:, C_NK:C_NK + GROUP_W].reshape(Bp, Lp, NAT_HEADS, NAT_HD))
        new_nv.append(up[:, C_NV:C_NV + GROUP_W].reshape(Bp, Lp, NAT_HEADS, NAT_HD))
        new_gs.append(gs)

        st = state_gla[:, l].astype(f32)
        o_gla_s, _ = gla_mixer(u, N_PROMPT, Bs, Ls, 512, gla_w_up[l], gla_b_up[l], gla_norm[l],
                               st[:, 0], st[:, 1])
        q2_s, kb_s = diff_prep(u, N_PROMPT, N_SAMPLE, rope=True)
        k_all = jnp.concatenate([kb_s.reshape(Bs, Ls, GROUP_W),
                                 cache_diff_k[:, l].reshape(Bs, PAST_LEN, GROUP_W).astype(bf16)], axis=1)
        v_all = jnp.concatenate([u[N_PROMPT:, C_DV:C_DV + GROUP_W].astype(bf16).reshape(Bs, Ls, GROUP_W),
                                 cache_diff_v[:, l].reshape(Bs, PAST_LEN, GROUP_W).astype(bf16)], axis=1)
        o_diff_s = flash_attention(q2_s, k_all, jnp.swapaxes(v_all, 1, 2), n_batch=Bs, lq=Ls, tq=2048, tk=1536,
                                   qcol=0, kcol=0, vcol=0, v_transposed=True,
                                   lam=lam, g=diff_norm[l], out_scale=1.0 - lam_init)
        o_nat_s = nat_attention(u, N_PROMPT, cache_nat_k[:, l].reshape(Bs, PAST_LEN, GROUP_W),
                                cache_nat_v[:, l].reshape(Bs, PAST_LEN, GROUP_W), nat_rpb[l])
        o_pool_s = pool_mixer(u, N_PROMPT, Bs, Ls, pool_w[l], pool_scale[l])

        def both(a, b):
            return jnp.concatenate([a, b], axis=0)

        parts = [both(o_pool_p, o_pool_s), both(o_gla_p, o_gla_s), both(o_diff_p, o_diff_s),
                 both(o_nat_p, o_nat_s)]
        x = output_projection(parts, w_out[l].astype(bf16), x, mod)

        wr = jnp.zeros((D_MODEL, LANES), f32)
        wr = wr.at[:, :MOE_GROUPS].set(router_group_w[l]).at[:, MOE_GROUPS:MOE_GROUPS + MOE_EXPERTS].set(
            router_expert_w[l])
        br = jnp.zeros((1, LANES), f32)
        br = br.at[0, :MOE_GROUPS].set(router_group_b[l]).at[0, MOE_GROUPS:MOE_GROUPS + MOE_EXPERTS].set(
            router_expert_b[l])
        h2, route, counts = moe_router(x, norm2[l], mod, wr, br)
        pos, slot_token, tile_expert, n_used = route_layout(route, counts)
        xs = jnp.take(h2, slot_token, axis=0)
        ys = expert_ffn(tile_expert, n_used, xs, expert_w_gate[l].astype(bf16),
                        expert_w_up[l].astype(bf16), expert_w_down[l].astype(bf16))
        y0 = jnp.take(ys, pos[:, 0], axis=0)
        y1 = jnp.take(ys, pos[:, 1], axis=0)
        x = moe_combine(x, y0, y1, route, mod, norm_final if l == DEPTH - 1 else None)

    y_prompt = x[:N_PROMPT].reshape(Bp, Lp, D_MODEL)
    y_sample = x[N_PROMPT:].reshape(Bs, Ls, D_MODEL)
    return (y_prompt, y_sample, jnp.stack(new_dk, axis=1), jnp.stack(new_dv, axis=1),
            jnp.stack(new_nk, axis=1), jnp.stack(new_nv, axis=1), jnp.stack(new_gs, axis=1))
```

```python
import functools
import math

import jax
import jax.numpy as jnp
import numpy as np
from jax import lax
from jax.experimental import pallas as pl
from jax.experimental.pallas import tpu as pltpu

f32 = jnp.float32
bf16 = jnp.bfloat16

D_MODEL = 4096
BATCH = 32
SEQ = 256
DEPTH = 2
DEC_BATCH = 4
DEC_SEQ = 4096
PAST_LEN = 512
GRID_W = 64
GROUP_W = D_MODEL // 4
POOL_WINDOWS = (2, 4, 8, 16)
POOL_GW = GROUP_W // 4
GLA_HEADS = 8
GLA_DV = GROUP_W // GLA_HEADS
GLA_DK = GLA_DV // 2
GLA_RANK = 16
GLA_GATE_NORM = 16.0
GLA_CHUNK = 64
DIFF_HEADS = 8
DIFF_VD = GROUP_W // DIFF_HEADS
DIFF_HD = DIFF_VD // 2
ROPE_BASE = 10000.0
NAT_HEADS = 8
NAT_HD = GROUP_W // NAT_HEADS
NAT_WH = 8
NAT_WW = 16
MOE_GROUPS = 4
MOE_PER_GROUP = 8
MOE_EXPERTS = MOE_GROUPS * MOE_PER_GROUP
MOE_TOPK = 2
MOE_FF = 512
Q_BLOCK = 128
EPS = 1e-6

N_PROMPT = BATCH * SEQ
N_SAMPLE = DEC_BATCH * DEC_SEQ
N_TOK = N_PROMPT + N_SAMPLE
LANES = 128
MOD_ROWS = 8

C_POOL, C_GQ, C_GK, C_GV, C_GG = 0, 1024, 1536, 2048, 3072
C_DQ, C_DK, C_DV, C_NQ, C_NK, C_NV, C_GD = 4096, 5120, 6144, 7168, 8192, 9216, 10240
D_IN_PAD = 10752
ORIG_GD = 3 * GROUP_W + 2 * GLA_HEADS * GLA_DK

TM = 512
TN_IN = 768
TN_OUT = 1024
TM_E = 256
N_ASSIGN = N_TOK * MOE_TOPK
N_ETILES = N_ASSIGN // TM_E + MOE_EXPERTS
VMEM_LIMIT = 56 * 1024 * 1024


def _cparams(sem):
    return pltpu.CompilerParams(dimension_semantics=sem, vmem_limit_bytes=VMEM_LIMIT)


def _mod_row(i):
    npt = N_PROMPT // TM
    return jnp.where(i < npt, 0, 1 + (i - npt) // (DEC_SEQ // TM))


def _ada_kernel(c_ref, w_ref, b_ref, o_ref):
    @pl.when(pl.program_id(2) == 0)
    def _():
        o_ref[0] = jnp.broadcast_to(b_ref[0], o_ref.shape[1:])

    c = c_ref[...]
    a = (c * jax.nn.sigmoid(c)).astype(bf16)
    o_ref[0] += jnp.dot(a, w_ref[0].astype(bf16), preferred_element_type=f32)


def ada_modulation(cvec, w_ada, b_ada):
    tk, tn = 2048, 1024
    n6 = 6 * D_MODEL
    return pl.pallas_call(
        _ada_kernel,
        grid=(DEPTH, n6 // tn, D_MODEL // tk),
        in_specs=[pl.BlockSpec((MOD_ROWS, tk), lambda l, j, k: (0, k)),
                  pl.BlockSpec((1, tk, tn), lambda l, j, k: (l, k, j)),
                  pl.BlockSpec((1, 1, tn), lambda l, j, k: (l, 0, j))],
        out_specs=pl.BlockSpec((1, MOD_ROWS, tn), lambda l, j, k: (l, 0, j)),
        out_shape=jax.ShapeDtypeStruct((DEPTH, MOD_ROWS, n6), f32),
        compiler_params=_cparams(("parallel", "parallel", "arbitrary")),
        name="ada_modulation",
    )(cvec, w_ada, b_ada.reshape(DEPTH, 1, n6))


def _modulated_norm(x, g, shift, scale):
    y = x * lax.rsqrt(jnp.mean(x * x, axis=-1, keepdims=True) + EPS) * g
    return y * (1.0 + scale) + shift


def _win_kernel(x_ref, g_ref, mod_ref, w_ref, o_ref, h_ref):
    @pl.when(pl.program_id(1) == 0)
    def _():
        h = _modulated_norm(x_ref[...], g_ref[...], mod_ref[0, 0:1, :], mod_ref[0, 1:2, :])
        h_ref[...] = h.astype(bf16)

    o_ref[...] = jnp.dot(h_ref[...], w_ref[...], preferred_element_type=f32)


def input_projection(x, g_norm, mod, w):
    return pl.pallas_call(
        _win_kernel,
        grid=(N_TOK // TM, D_IN_PAD // TN_IN),
        in_specs=[pl.BlockSpec((TM, D_MODEL), lambda i, j: (i, 0)),
                  pl.BlockSpec((1, D_MODEL), lambda i, j: (0, 0)),
                  pl.BlockSpec((1, 6, D_MODEL), lambda i, j: (_mod_row(i), 0, 0)),
                  pl.BlockSpec((D_MODEL, TN_IN), lambda i, j: (0, j))],
        out_specs=pl.BlockSpec((TM, TN_IN), lambda i, j: (i, j)),
        out_shape=jax.ShapeDtypeStruct((N_TOK, D_IN_PAD), f32),
        scratch_shapes=[pltpu.VMEM((TM, D_MODEL), bf16)],
        compiler_params=_cparams(("parallel", "arbitrary")),
        name="input_projection",
    )(x, g_norm.reshape(1, D_MODEL), mod, w)


def _wout_kernel(a0, a1, a2, a3, w0, w1, w2, w3, x_ref, mod_ref, o_ref):
    acc = jnp.dot(a0[...], w0[...], preferred_element_type=f32)
    acc += jnp.dot(a1[...], w1[...], preferred_element_type=f32)
    acc += jnp.dot(a2[...], w2[...], preferred_element_type=f32)
    acc += jnp.dot(a3[...], w3[...], preferred_element_type=f32)
    o_ref[...] = x_ref[...] + mod_ref[0, 2:3, :] * acc


def output_projection(parts, w, x, mod):
    a_specs = [pl.BlockSpec((TM, GROUP_W), lambda i, j: (i, 0)) for _ in range(4)]
    w_specs = [pl.BlockSpec((GROUP_W, TN_OUT), functools.partial(lambda i, j, c: (c, j), c=c))
               for c in range(4)]
    return pl.pallas_call(
        _wout_kernel,
        grid=(N_TOK // TM, D_MODEL // TN_OUT),
        in_specs=a_specs + w_specs + [
            pl.BlockSpec((TM, TN_OUT), lambda i, j: (i, j)),
            pl.BlockSpec((1, 6, TN_OUT), lambda i, j: (_mod_row(i), 0, j))],
        out_specs=pl.BlockSpec((TM, TN_OUT), lambda i, j: (i, j)),
        out_shape=jax.ShapeDtypeStruct((N_TOK, D_MODEL), f32),
        compiler_params=_cparams(("parallel", "arbitrary")),
        name="output_projection",
    )(*parts, w, w, w, w, x, mod)


def _split_bf16(a):
    hi = a.astype(bf16)
    lo = (a - hi.astype(f32)).astype(bf16)
    return hi, lo


def _router_kernel(x_ref, g_ref, mod_ref, wr_ref, br_ref, h_ref, r_ref, cnt_ref):
    h = _modulated_norm(x_ref[...], g_ref[...], mod_ref[0, 3:4, :], mod_ref[0, 4:5, :])
    h_ref[...] = h.astype(bf16)
    h_hi, h_lo = _split_bf16(h)
    w_hi, w_lo = _split_bf16(wr_ref[...])
    lg = (jnp.dot(h_hi, w_hi, preferred_element_type=f32)
          + jnp.dot(h_lo, w_hi, preferred_element_type=f32)
          + jnp.dot(h_hi, w_lo, preferred_element_type=f32)) + br_ref[...]
    lane = lax.broadcasted_iota(jnp.int32, lg.shape, 1).astype(f32)
    ninf = -jnp.inf

    def first_max(v):
        m = jnp.max(v, axis=-1, keepdims=True)
        idx = jnp.min(jnp.where(v == m, lane, float(LANES)), axis=-1, keepdims=True)
        return m, idx

    gmask = lane < MOE_GROUPS
    mg, gi = first_max(jnp.where(gmask, lg, ninf))
    p_top = 1.0 / jnp.sum(jnp.where(gmask, jnp.exp(lg - mg), 0.0), axis=-1, keepdims=True)
    lo = MOE_GROUPS + gi * MOE_PER_GROUP
    le = jnp.where((lane >= lo) & (lane < lo + MOE_PER_GROUP), lg, ninf)
    v1, i1 = first_max(le)
    v2, i2 = first_max(jnp.where(lane == i1, ninf, le))
    t = jnp.exp(v2 - v1)
    w1 = p_top / (1.0 + t)
    w2 = w1 * t

    @pl.when(pl.program_id(0) == 0)
    def _():
        cnt_ref[...] = jnp.zeros(cnt_ref.shape, f32)

    tm = lg.shape[0]
    chosen = jnp.where((lane == i1) | (lane == i2), 1.0, 0.0)
    earlier = (lax.broadcasted_iota(jnp.int32, (tm, tm), 1) < lax.broadcasted_iota(jnp.int32, (tm, tm), 0))
    before = jnp.dot(jnp.where(earlier, 1.0, 0.0).astype(bf16), chosen.astype(bf16),
                     preferred_element_type=f32) + cnt_ref[0:1, :]
    rank1 = jnp.sum(jnp.where(lane == i1, before, 0.0), axis=-1, keepdims=True)
    rank2 = jnp.sum(jnp.where(lane == i2, before, 0.0), axis=-1, keepdims=True)
    cnt_ref[...] = cnt_ref[...] + jnp.sum(chosen, axis=0, keepdims=True)

    slab = jnp.zeros(lg.shape, f32)
    for pos, val in enumerate((i1 - MOE_GROUPS, i2 - MOE_GROUPS, w1, w2, rank1, rank2)):
        slab = jnp.where(lane == float(pos), val, slab)
    r_ref[...] = slab


def moe_router(x, g_norm, mod, wr, br):
    tm = 256
    return pl.pallas_call(
        _router_kernel,
        grid=(N_TOK // tm,),
        in_specs=[pl.BlockSpec((tm, D_MODEL), lambda i: (i, 0)),
                  pl.BlockSpec((1, D_MODEL), lambda i: (0, 0)),
                  pl.BlockSpec((1, 6, D_MODEL), lambda i: (_mod_row(i // (TM // tm)), 0, 0)),
                  pl.BlockSpec((D_MODEL, LANES), lambda i: (0, 0)),
                  pl.BlockSpec((1, LANES), lambda i: (0, 0))],
        out_specs=[pl.BlockSpec((tm, D_MODEL), lambda i: (i, 0)),
                   pl.BlockSpec((tm, LANES), lambda i: (i, 0)),
                   pl.BlockSpec((MOD_ROWS, LANES), lambda i: (0, 0))],
        out_shape=[jax.ShapeDtypeStruct((N_TOK, D_MODEL), bf16),
                   jax.ShapeDtypeStruct((N_TOK, LANES), f32),
                   jax.ShapeDtypeStruct((MOD_ROWS, LANES), f32)],
        compiler_params=_cparams(("arbitrary",)),
        name="moe_router",
    )(x, g_norm.reshape(1, D_MODEL), mod, wr, br)


def _ffn_kernel(te_ref, nu_ref, x_ref, wg_ref, wu_ref, wd_ref, o_ref, wg_b, wu_b, wd_b):
    t = pl.program_id(0)

    @pl.when(t < nu_ref[0])
    def _():
        @pl.when((t == 0) | (te_ref[t] != te_ref[jnp.maximum(t - 1, 0)]))
        def _():
            wg_b[...] = wg_ref[0, 0].astype(bf16)
            wu_b[...] = wu_ref[0, 0].astype(bf16)
            wd_b[...] = wd_ref[0, 0].astype(bf16)

        x = x_ref[...]
        a = jnp.dot(x, wg_b[...], preferred_element_type=f32)
        u = jnp.dot(x, wu_b[...], preferred_element_type=f32)
        mid = (a * jax.nn.sigmoid(a) * u).astype(bf16)
        o_ref[...] = jnp.dot(mid, wd_b[...], preferred_element_type=f32).astype(o_ref.dtype)

    @pl.when(t >= nu_ref[0])
    def _():
        o_ref[...] = jnp.zeros(o_ref.shape, o_ref.dtype)


def expert_ffn(layer, tile_expert, n_used, xs, wg, wu, wd):
    def w_spec(r, c):
        return pl.BlockSpec((1, 1, r, c), lambda t, te, nu: (layer, te[t], 0, 0), pipeline_mode=pl.Buffered(1))

    return pl.pallas_call(
        _ffn_kernel,
        grid_spec=pltpu.PrefetchScalarGridSpec(
            num_scalar_prefetch=2,
            grid=(N_ETILES,),
            in_specs=[pl.BlockSpec((TM_E, D_MODEL), lambda t, te, nu: (jnp.minimum(t, nu[0] - 1), 0)),
                      w_spec(D_MODEL, MOE_FF), w_spec(D_MODEL, MOE_FF), w_spec(MOE_FF, D_MODEL)],
            out_specs=pl.BlockSpec((TM_E, D_MODEL), lambda t, te, nu: (t, 0)),
            scratch_shapes=[pltpu.VMEM((D_MODEL, MOE_FF), bf16), pltpu.VMEM((D_MODEL, MOE_FF), bf16),
                            pltpu.VMEM((MOE_FF, D_MODEL), bf16)]),
        out_shape=jax.ShapeDtypeStruct((N_ETILES * TM_E, D_MODEL), bf16),
        compiler_params=_cparams(("arbitrary",)),
        name="expert_ffn",
    )(tile_expert, n_used, xs, wg, wu, wd)


def route_layout(route, counts):
    e = route[:, :MOE_TOPK].astype(jnp.int32)
    rank = route[:, 4:4 + MOE_TOPK].astype(jnp.int32)
    cnt = counts[0, MOE_GROUPS:MOE_GROUPS + MOE_EXPERTS].astype(jnp.int32)
    tiles_e = (cnt + TM_E - 1) // TM_E
    tile_end = jnp.cumsum(tiles_e)
    tile_start = tile_end - tiles_e
    pos = tile_start[e] * TM_E + rank
    slot_token = jnp.zeros((N_ETILES * TM_E,), jnp.int32).at[pos.reshape(-1)].set(
        jnp.arange(N_ASSIGN, dtype=jnp.int32) // MOE_TOPK)
    n_used = tile_end[-1]
    t = jnp.minimum(jnp.arange(N_ETILES, dtype=jnp.int32), n_used - 1)
    tile_expert = jnp.sum((t[:, None] >= tile_end[None, :]).astype(jnp.int32), axis=1)
    tile_expert = jnp.minimum(tile_expert, MOE_EXPERTS - 1)
    return pos, slot_token, tile_expert, n_used.reshape(1).astype(jnp.int32)


def _combine_kernel(x_ref, y0_ref, y1_ref, r_ref, mod_ref, o_ref):
    y = r_ref[:, 2:3] * y0_ref[...] + r_ref[:, 3:4] * y1_ref[...]
    o_ref[...] = x_ref[...] + mod_ref[0, 5:6, :] * y


def _combine_norm_kernel(x_ref, y0_ref, y1_ref, r_ref, mod_ref, g_ref, o_ref):
    y = r_ref[:, 2:3] * y0_ref[...] + r_ref[:, 3:4] * y1_ref[...]
    x = x_ref[...] + mod_ref[0, 5:6, :] * y
    o_ref[...] = x * lax.rsqrt(jnp.mean(x * x, axis=-1, keepdims=True) + EPS) * g_ref[...]


def moe_combine(x, y0, y1, route, mod, g_final=None):
    tm = 256
    row = pl.BlockSpec((tm, D_MODEL), lambda i: (i, 0))
    in_specs = [row, row, row, pl.BlockSpec((tm, LANES), lambda i: (i, 0)),
                pl.BlockSpec((1, 6, D_MODEL), lambda i: (_mod_row(i // (TM // tm)), 0, 0))]
    args = [x, y0, y1, route, mod]
    body = _combine_kernel
    if g_final is not None:
        in_specs.append(pl.BlockSpec((1, D_MODEL), lambda i: (0, 0)))
        args.append(g_final.reshape(1, D_MODEL))
        body = _combine_norm_kernel
    return pl.pallas_call(
        body,
        grid=(N_TOK // tm,),
        in_specs=in_specs,
        out_specs=row,
        out_shape=jax.ShapeDtypeStruct((N_TOK, D_MODEL), f32),
        compiler_params=_cparams(("parallel",)),
        name="moe_combine",
    )(*args)


def _diff_prep_kernel(rope, q_ref, k_ref, cos_ref, sin_ref, q2_ref, kb_ref):
    lane = lax.broadcasted_iota(jnp.int32, q_ref.shape, 1)

    def rot(x):
        if not rope:
            return x
        partner = jnp.where((lane & 31) >= 16, pltpu.roll(x, 16, 1), pltpu.roll(x, LANES - 16, 1))
        return x * cos_ref[...] + partner * sin_ref[...]

    q = rot(q_ref[...]) * DIFF_HD ** -0.5
    first = lane < DIFF_HD
    q2_ref[0] = jnp.where(first, q, 0.0).astype(bf16)
    q2_ref[1] = jnp.where(first, 0.0, q).astype(bf16)
    kb_ref[...] = rot(k_ref[...]).astype(bf16)


def _rope_tables():
    t = jnp.arange(DEC_SEQ)
    row = (t // GRID_W).astype(f32)
    col = (t % GRID_W).astype(f32)
    nf = DIFF_HD // 4
    inv = ROPE_BASE ** (-jnp.arange(nf, dtype=f32) / nf)
    lane = np.arange(LANES)
    pos = jnp.where(((lane // 32) % 2 == 0)[None, :], row[:, None], col[:, None])
    ang = pos * inv[lane % nf][None, :]
    sign = np.where(lane % 32 >= nf, 1.0, -1.0).astype(np.float32)
    return jnp.cos(ang), jnp.sin(ang) * sign[None, :]


def diff_prep(u, row0, n_rows, rope):
    t = 1024
    r0 = row0 // t
    nseq = DEC_SEQ // t
    cos, sin = _rope_tables() if rope else (jnp.zeros((t, LANES), f32), jnp.zeros((t, LANES), f32))
    tab = pl.BlockSpec((t, LANES), (lambda i, h: (i % nseq, 0)) if rope else (lambda i, h: (0, 0)))
    return pl.pallas_call(
        functools.partial(_diff_prep_kernel, rope),
        grid=(n_rows // t, DIFF_HEADS),
        in_specs=[pl.BlockSpec((t, LANES), lambda i, h: (r0 + i, C_DQ // LANES + h)),
                  pl.BlockSpec((t, LANES), lambda i, h: (r0 + i, C_DK // LANES + h)),
                  tab, tab],
        out_specs=[pl.BlockSpec((2, t, LANES), lambda i, h: (0, i, h)),
                   pl.BlockSpec((t, LANES), lambda i, h: (i, h))],
        out_shape=[jax.ShapeDtypeStruct((2, n_rows, GROUP_W), bf16),
                   jax.ShapeDtypeStruct((n_rows, GROUP_W), bf16)],
        compiler_params=_cparams(("parallel", "parallel")),
        name="diff_prep",
    )(u, u, cos, sin)


def _flash_kernel(n_stack, tq, sub, scale, out_scale, v_transposed, q_ref, k_ref, v_ref, lam_ref, g_ref, o_ref,
                  m_ref, l_ref, acc_ref):
    kj = pl.program_id(3)

    @pl.when(kj == 0)
    def _():
        m_ref[...] = jnp.full(m_ref.shape, -jnp.inf, f32)
        l_ref[...] = jnp.zeros(l_ref.shape, f32)
        acc_ref[...] = jnp.zeros(acc_ref.shape, f32)

    q = q_ref[...].reshape(n_stack * tq, LANES).astype(bf16)
    tk = k_ref.shape[1]
    m, l, acc = m_ref[...], l_ref[...], acc_ref[...]
    for c in range(tk // sub):
        keys = slice(c * sub, (c + 1) * sub)
        s = lax.dot_general(k_ref[0, keys, :].astype(bf16), q, (((1,), (1,)), ((), ())),
                            preferred_element_type=f32)
        if scale != 1.0:
            s = s * scale
        m_new = jnp.maximum(m, jnp.max(s, axis=0, keepdims=True))
        alpha = jnp.exp(m - m_new)
        p = jnp.exp(s - m_new)
        l = alpha * l + jnp.sum(p, axis=0, keepdims=True)
        vt = v_ref[0, :, keys].astype(bf16) if v_transposed else v_ref[0, keys, :].T.astype(bf16)
        acc = alpha * acc + jnp.dot(vt, p.astype(bf16), preferred_element_type=f32)
        m = m_new
    m_ref[...], l_ref[...], acc_ref[...] = m, l, acc

    @pl.when(kj == pl.num_programs(3) - 1)
    def _():
        o = acc_ref[...] / l_ref[...]
        if n_stack == 2:
            o = o[:, :tq] - lam_ref[0:1, 0:1] * o[:, tq:]
            o = o * lax.rsqrt(jnp.mean(o * o, axis=0, keepdims=True) + EPS)
            o = o.T * (g_ref[...] * out_scale)
        else:
            o = o.T
        o_ref[...] = o.astype(o_ref.dtype)


def flash_attention(q, k, v, *, n_batch, lq, tq, tk, qcol, kcol, vcol, sub=512, v_transposed=False, qrow0=0, kb0=0,
                    scale=1.0, lam=None, g=None, out_scale=1.0):
    n_stack = q.shape[0]
    lk = k.shape[1]
    nq = lq // tq
    lam = jnp.zeros((1, LANES), f32) if lam is None else jnp.full((1, LANES), lam, f32)
    g = jnp.ones((1, LANES), f32) if g is None else g.reshape(1, LANES).astype(f32)
    rows = n_stack * tq
    if v_transposed:
        v_spec = pl.BlockSpec((1, LANES, tk), lambda b, h, i, j: (kb0 + b, vcol + h, j))
    else:
        v_spec = pl.BlockSpec((1, tk, LANES), lambda b, h, i, j: (kb0 + b, j, vcol + h))
    return pl.pallas_call(
        functools.partial(_flash_kernel, n_stack, tq, min(sub, tk), scale, out_scale, v_transposed),
        grid=(n_batch, DIFF_HEADS, nq, lk // tk),
        in_specs=[pl.BlockSpec((n_stack, tq, LANES), lambda b, h, i, j: (0, qrow0 + b * nq + i, qcol + h)),
                  pl.BlockSpec((1, tk, LANES), lambda b, h, i, j: (kb0 + b, j, kcol + h)),
                  v_spec,
                  pl.BlockSpec((1, LANES), lambda b, h, i, j: (0, 0)),
                  pl.BlockSpec((1, LANES), lambda b, h, i, j: (0, 0))],
        out_specs=pl.BlockSpec((tq, LANES), lambda b, h, i, j: (b * nq + i, h)),
        out_shape=jax.ShapeDtypeStruct((n_batch * lq, GROUP_W), bf16),
        scratch_shapes=[pltpu.VMEM((1, rows), f32), pltpu.VMEM((1, rows), f32),
                        pltpu.VMEM((LANES, rows), f32)],
        compiler_params=_cparams(("parallel", "parallel", "parallel", "arbitrary")),
        name="flash_attention",
    )(q, k, v, lam, g)


NAT_ROWS = DEC_SEQ // GRID_W
NAT_RB = 8
NAT_BAND = NAT_WH * GRID_W
NEG_BIG = -1e30


def _nat_kernel(q_ref, k_ref, v_ref, kc_ref, vc_ref, bias_ref, o_ref):
    rb = pl.program_id(2)
    scale = NAT_HD ** -0.5
    kc = kc_ref[0].astype(bf16)
    vc = vc_ref[0].astype(bf16)
    nt = (((1,), (1,)), ((), ()))
    for i in range(NAT_RB):
        r = rb * NAT_RB + i
        rs = jnp.clip(r - NAT_WH // 2, 0, NAT_ROWS - NAT_WH)
        k0 = pl.multiple_of(rs * GRID_W, GRID_W)
        kband = k_ref[pl.ds(k0, NAT_BAND), :].astype(bf16)
        vband = v_ref[pl.ds(k0, NAT_BAND), :].astype(bf16)
        q = q_ref[i * GRID_W:(i + 1) * GRID_W, :].astype(bf16)
        sb = lax.dot_general(q, kband, nt, preferred_element_type=f32) * scale + bias_ref[rs - r + NAT_WH - 1, 0]
        sc = lax.dot_general(q, kc, nt, preferred_element_type=f32) * scale
        m = jnp.maximum(jnp.max(sb, axis=-1, keepdims=True), jnp.max(sc, axis=-1, keepdims=True))
        pb = jnp.exp(sb - m)
        pc = jnp.exp(sc - m)
        l = jnp.sum(pb, axis=-1, keepdims=True) + jnp.sum(pc, axis=-1, keepdims=True)
        o = (jnp.dot(pb.astype(bf16), vband, preferred_element_type=f32)
             + jnp.dot(pc.astype(bf16), vc, preferred_element_type=f32)) / l
        o_ref[i * GRID_W:(i + 1) * GRID_W, :] = o.astype(o_ref.dtype)


def _nat_bias_table(rpb):
    colv = np.arange(GRID_W)
    cstart = np.clip(colv - NAT_WW // 2, 0, GRID_W - NAT_WW)
    col_mask = (colv[None, :] >= cstart[:, None]) & (colv[None, :] < cstart[:, None] + NAT_WW)
    col_idx = np.clip(colv[None, :] - colv[:, None] + NAT_WW - 1, 0, 2 * NAT_WW - 2)
    rpb_c = rpb.astype(f32)[:, :, col_idx]
    row_idx = np.arange(NAT_WH)[:, None] + np.arange(NAT_WH)[None, :]
    tab = rpb_c[:, row_idx]
    tab = jnp.where(col_mask[None, None, None], tab, NEG_BIG)
    return tab.transpose(1, 0, 3, 2, 4).reshape(NAT_WH, NAT_HEADS, GRID_W, NAT_BAND)


def nat_attention(u, row0, kc, vc, rpb):
    qblk = NAT_RB * GRID_W
    q0 = row0 // qblk
    b0 = row0 // DEC_SEQ
    return pl.pallas_call(
        _nat_kernel,
        grid=(DEC_BATCH, NAT_HEADS, NAT_ROWS // NAT_RB),
        in_specs=[pl.BlockSpec((qblk, LANES), lambda b, h, r: (q0 + b * (NAT_ROWS // NAT_RB) + r, C_NQ // LANES + h)),
                  pl.BlockSpec((DEC_SEQ, LANES), lambda b, h, r: (b0 + b, C_NK // LANES + h)),
                  pl.BlockSpec((DEC_SEQ, LANES), lambda b, h, r: (b0 + b, C_NV // LANES + h)),
                  pl.BlockSpec((1, PAST_LEN, LANES), lambda b, h, r: (b, 0, h)),
                  pl.BlockSpec((1, PAST_LEN, LANES), lambda b, h, r: (b, 0, h)),
                  pl.BlockSpec((NAT_WH, 1, GRID_W, NAT_BAND), lambda b, h, r: (0, h, 0, 0))],
        out_specs=pl.BlockSpec((qblk, LANES), lambda b, h, r: (b * (NAT_ROWS // NAT_RB) + r, h)),
        out_shape=jax.ShapeDtypeStruct((N_SAMPLE, GROUP_W), bf16),
        compiler_params=_cparams(("parallel", "parallel", "arbitrary")),
        name="nat_attention",
    )(u, u, u, kc, vc, _nat_bias_table(rpb))


def _pool_kernel(seq, u_ref, w_ref, s_ref, o_ref):
    grp = pl.program_id(1)
    for gi, win in enumerate(POOL_WINDOWS):
        @pl.when(grp == gi)
        def _(win=win):
            u = u_ref[...]
            t = lax.broadcasted_iota(jnp.int32, u.shape, 0)
            acc = jnp.zeros_like(u)
            for d in range(-(win // 2), win // 2):
                shifted = u if d == 0 else pltpu.roll(u, (-d) % seq, 0)
                acc += jnp.where((t + d >= 0) & (t + d < seq), shifted, 0.0)
            cnt = (jnp.minimum(t + win // 2, seq) - jnp.maximum(t - win // 2, 0)).astype(f32)
            p = acc / cnt - u
            y = jnp.dot(p.astype(bf16), w_ref[0].astype(bf16), preferred_element_type=f32) * s_ref[...]
            o_ref[...] = y.astype(o_ref.dtype)


def pool_mixer(u, row0, n_seq, seq, w, s):
    return pl.pallas_call(
        functools.partial(_pool_kernel, seq),
        grid=(n_seq, len(POOL_WINDOWS)),
        in_specs=[pl.BlockSpec((seq, POOL_GW), lambda b, g: (row0 // seq + b, g)),
                  pl.BlockSpec((1, POOL_GW, POOL_GW), lambda b, g: (g, 0, 0)),
                  pl.BlockSpec((1, POOL_GW), lambda b, g: (0, g))],
        out_specs=pl.BlockSpec((seq, POOL_GW), lambda b, g: (b, g)),
        out_shape=jax.ShapeDtypeStruct((n_seq * seq, GROUP_W), bf16),
        compiler_params=_cparams(("parallel", "parallel")),
        name="pool_mixer",
    )(u, w, s.reshape(1, GROUP_W))


GLA_QK = GLA_HEADS * GLA_DK


def _gla_kernel(reverse, final, n_chunks, *refs):
    if final:
        (q_ref, k_ref, v_ref, gd_ref, wup_ref, bup_ref, s0_ref, op_ref, gg_ref, gn_ref,
         o_ref, s_ref, st_ref) = refs
    else:
        q_ref, k_ref, v_ref, gd_ref, wup_ref, bup_ref, s0_ref, o_ref, s_ref, st_ref = refs
    j = pl.program_id(1)
    C = GLA_CHUNK

    @pl.when(j == 0)
    def _():
        for h in range(GLA_HEADS):
            s0 = s0_ref[0, h]
            z = jnp.zeros_like(s0)
            st_ref[h] = jnp.concatenate([s0, z] if h % 2 == 0 else [z, s0], axis=0).T

    ti = lax.broadcasted_iota(jnp.int32, (C, C), 0)
    tj = lax.broadcasted_iota(jnp.int32, (C, C), 1)
    tri = (ti <= tj) if reverse else (ti >= tj)
    tri_b = jnp.where(tri, 1.0, 0.0).astype(bf16)
    first_half = lax.broadcasted_iota(jnp.int32, (C, LANES), 1) < GLA_DK
    ref_row = C // 2 - 1 if reverse else C // 2
    end_row = 0 if reverse else C - 1
    nt = (((1,), (1,)), ((), ()))

    def chunk(ci, carry):
        c = n_chunks - 1 - ci if reverse else ci
        rows = pl.ds(pl.multiple_of(c * C, C), C)
        logit = jnp.dot(gd_ref[rows, :].astype(bf16), wup_ref[...], preferred_element_type=f32) + bup_ref[...]
        la = (jnp.minimum(logit, 0.0) - jnp.log1p(jnp.exp(-jnp.abs(logit)))) * (1.0 / GLA_GATE_NORM)
        hi = la.astype(bf16)
        r1 = la - hi.astype(f32)
        mid = r1.astype(bf16)
        lo = (r1 - mid.astype(f32)).astype(bf16)
        b = (jnp.dot(tri_b, hi, preferred_element_type=f32) + jnp.dot(tri_b, mid, preferred_element_type=f32)
             + jnp.dot(tri_b, lo, preferred_element_type=f32))
        bref = b[ref_row:ref_row + 1, :]
        bl = b[end_row:end_row + 1, :]
        q = q_ref[rows, :] * GLA_DK ** -0.5
        k = k_ref[rows, :]
        qs = q * jnp.exp(b - bref)
        ks = (k * jnp.exp(bref - b)).astype(bf16)
        qe = (q * jnp.exp(b)).astype(bf16)
        kd = k * jnp.exp(bl - b)
        ebl = jnp.exp(bl)
        for h in range(GLA_HEADS):
            pair = slice((h // 2) * LANES, (h // 2 + 1) * LANES)
            mine = first_half if h % 2 == 0 else jnp.logical_not(first_half)
            cols = slice(h * GLA_DV, (h + 1) * GLA_DV)
            a = lax.dot_general(jnp.where(mine, qs[:, pair], 0.0).astype(bf16), ks[:, pair], nt,
                                preferred_element_type=f32)
            a = jnp.where(tri, a, 0.0).astype(bf16)
            vh = v_ref[rows, cols]
            st = st_ref[h]
            o = (jnp.dot(a, vh.astype(bf16), preferred_element_type=f32)
                 + lax.dot_general(qe[:, pair], st.astype(bf16), nt, preferred_element_type=f32))
            st_ref[h] = st * ebl[:, pair] + jnp.dot(vh.T.astype(bf16),
                                                    jnp.where(mine, kd[:, pair], 0.0).astype(bf16),
                                                    preferred_element_type=f32)
            if final:
                o = o + op_ref[rows, cols]
                o = o * lax.rsqrt(jnp.mean(o * o, axis=-1, keepdims=True) + EPS) * gn_ref[...]
                gate = gg_ref[rows, cols]
                o = o * (gate * jax.nn.sigmoid(gate))
            o_ref[rows, cols] = o.astype(o_ref.dtype)
        return carry

    lax.fori_loop(0, n_chunks, chunk, 0)

    @pl.when(j == pl.num_programs(1) - 1)
    def _():
        for h in range(GLA_HEADS):
            half = (h % 2) * GLA_DK
            s_ref[0, h] = st_ref[h].T[half:half + GLA_DK, :]


def _gla_pass(u, row0, n_seq, seq, tb, z, w_up, b_up, s0, o_prev=None, g_norm=None):
    reverse = z == 1
    final = o_prev is not None
    nblk = seq // tb
    rb0 = row0 // tb
    wup = jnp.zeros((LANES, GLA_QK), f32).at[z * GLA_RANK:(z + 1) * GLA_RANK].set(w_up[z]).astype(bf16)

    def blk(b, j):
        return b * nblk + (nblk - 1 - j if reverse else j)

    def ucol(width, c0):
        return pl.BlockSpec((tb, width), lambda b, j: (rb0 + blk(b, j), c0 // width))

    in_specs = [ucol(GLA_QK, C_GQ), ucol(GLA_QK, C_GK), ucol(GROUP_W, C_GV), ucol(LANES, C_GD),
                pl.BlockSpec((LANES, GLA_QK), lambda b, j: (0, 0)),
                pl.BlockSpec((1, GLA_QK), lambda b, j: (0, 0)),
                pl.BlockSpec((1, GLA_HEADS, GLA_DK, GLA_DV), lambda b, j: (b, 0, 0, 0))]
    args = [u, u, u, u, wup, b_up[z].reshape(1, GLA_QK), s0]
    if final:
        in_specs += [pl.BlockSpec((tb, GROUP_W), lambda b, j: (blk(b, j), 0)), ucol(GROUP_W, C_GG),
                     pl.BlockSpec((1, GLA_DV), lambda b, j: (0, 0))]
        args += [o_prev, u, g_norm.reshape(1, GLA_DV)]
    return pl.pallas_call(
        functools.partial(_gla_kernel, reverse, final, tb // GLA_CHUNK),
        grid=(n_seq, nblk),
        in_specs=in_specs,
        out_specs=[pl.BlockSpec((tb, GROUP_W), lambda b, j: (blk(b, j), 0)),
                   pl.BlockSpec((1, GLA_HEADS, GLA_DK, GLA_DV), lambda b, j: (b, 0, 0, 0))],
        out_shape=[jax.ShapeDtypeStruct((n_seq * seq, GROUP_W), bf16 if final else f32),
                   jax.ShapeDtypeStruct((n_seq, GLA_HEADS, GLA_DK, GLA_DV), f32)],
        scratch_shapes=[pltpu.VMEM((GLA_HEADS, GLA_DV, LANES), f32)],
        compiler_params=_cparams(("parallel", "arbitrary")),
        name="gla_backward" if reverse else "gla_forward",
    )(*args)


def gla_mixer(u, row0, n_seq, seq, tb, w_up, b_up, g_norm, s0_f, s0_b):
    o_f, s_f = _gla_pass(u, row0, n_seq, seq, tb, 0, w_up, b_up, s0_f)
    o, s_b = _gla_pass(u, row0, n_seq, seq, tb, 1, w_up, b_up, s0_b, o_prev=o_f, g_norm=g_norm)
    return o, jnp.stack([s_f, s_b], axis=1)


def _head_rms(o, g):
    return o * lax.rsqrt(jnp.mean(o * o, axis=-1, keepdims=True) + EPS) * g.astype(f32)


def _pool_mixer(u, w, s):
    B, L, _ = u.shape
    cs = jnp.concatenate([jnp.zeros((B, 1, GROUP_W), u.dtype), jnp.cumsum(u, axis=1)], axis=1)
    t = jnp.arange(L)
    parts = []
    for i, win in enumerate(POOL_WINDOWS):
        lo = jnp.clip(t - win // 2, 0, L)
        hi = jnp.clip(t + win // 2, 0, L)
        cg = cs[..., i * POOL_GW:(i + 1) * POOL_GW]
        mean = (cg[:, hi] - cg[:, lo]) / (hi - lo).astype(u.dtype)[None, :, None]
        parts.append(mean - u[..., i * POOL_GW:(i + 1) * POOL_GW])
    p = jnp.stack(parts, axis=2)
    y = jnp.einsum('blgc,gcd->blgd', p, w.astype(u.dtype)).reshape(B, L, GROUP_W)
    return y * s.astype(u.dtype)


def _gla_scan(q, k, v, la, s0):
    B, L, H, _ = q.shape
    C = GLA_CHUNK
    n = L // C
    mask = jnp.tril(jnp.ones((C, C), dtype=bool))

    def chunks(a):
        return jnp.moveaxis(a.reshape(B, n, C, H, a.shape[-1]), 1, 0)

    def step(S, inp):
        qc, kc, vc, lac = inp
        b = jnp.cumsum(lac, axis=1)
        ref = b[:, C // 2:C // 2 + 1]
        a = jnp.einsum('bihd,bjhd->bhij', qc * jnp.exp(b - ref), kc * jnp.exp(ref - b))
        a = jnp.where(mask, a, 0.0)
        o = (jnp.einsum('bhij,bjhv->bihv', a, vc)
             + jnp.einsum('bihd,bhdv->bihv', qc * jnp.exp(b), S))
        bl = b[:, -1]
        S = (jnp.exp(bl)[..., None] * S
             + jnp.einsum('bjhd,bjhv->bhdv', kc * jnp.exp(bl[:, None] - b), vc))
        return S, o

    S, o = lax.scan(step, s0, (chunks(q), chunks(k), chunks(v), chunks(la)))
    return jnp.moveaxis(o, 0, 1).reshape(B, L, H, v.shape[-1]), S


def _gla_mixer(q_in, k_in, v_in, g_in, gd_in, w_up, b_up, g_norm, s0_f, s0_b):
    B, L, _ = q_in.shape
    q = q_in.reshape(B, L, GLA_HEADS, GLA_DK) * GLA_DK ** -0.5
    k = k_in.reshape(B, L, GLA_HEADS, GLA_DK)
    v = v_in.reshape(B, L, GLA_HEADS, GLA_DV)
    gd = gd_in.reshape(B, L, 2, GLA_RANK)
    logit = jnp.einsum('blzr,zre->blze', gd, w_up.astype(f32)) + b_up.astype(f32)
    la = (jax.nn.log_sigmoid(logit) / GLA_GATE_NORM).reshape(B, L, 2, GLA_HEADS, GLA_DK)
    o_f, s_f = _gla_scan(q, k, v, la[:, :, 0], s0_f)
    rev = lambda a: jnp.flip(a, axis=1)
    o_b, s_b = _gla_scan(rev(q), rev(k), rev(v), rev(la[:, :, 1]), s0_b)
    o = _head_rms(o_f + rev(o_b), g_norm).reshape(B, L, GROUP_W) * jax.nn.silu(g_in)
    return o, jnp.stack([s_f, s_b], axis=1)


def _axial_rope(x):
    L = x.shape[1]
    t = jnp.arange(L)
    row = (t // GRID_W).astype(f32)
    col = (t % GRID_W).astype(f32)
    half = DIFF_HD // 2
    nf = half // 2
    inv = ROPE_BASE ** (-jnp.arange(nf, dtype=f32) / nf)

    def rot(xh, pos):
        ang = pos[:, None] * inv
        cos = jnp.cos(ang)[None, :, None, None, :]
        sin = jnp.sin(ang)[None, :, None, None, :]
        x1, x2 = xh[..., :nf], xh[..., nf:]
        return jnp.concatenate([x1 * cos - x2 * sin, x2 * cos + x1 * sin], axis=-1)

    return jnp.concatenate([rot(x[..., :half], row), rot(x[..., half:], col)], axis=-1)


def _diff_attention(q, k, v, lam, lam_init, g_norm):
    B, Lq = q.shape[:2]
    nb = Lq // Q_BLOCK
    qb = jnp.moveaxis(q.reshape(B, nb, Q_BLOCK, DIFF_HEADS, 2, DIFF_HD), 1, 0)

    def blk(qi):
        s = jnp.einsum('bqhcd,bkhcd->bchqk', qi, k) * DIFF_HD ** -0.5
        p = jax.nn.softmax(s, axis=-1)
        a = p[:, 0] - lam * p[:, 1]
        return jnp.einsum('bhqk,bkhv->bqhv', a, v)

    o = jnp.moveaxis(lax.map(blk, qb), 0, 1).reshape(B, Lq, DIFF_HEADS, DIFF_VD)
    return (_head_rms(o, g_norm) * (1.0 - lam_init)).reshape(B, Lq, GROUP_W)


def _dense_attention(q, k, v):
    B, Lq, H, D = q.shape
    nb = Lq // Q_BLOCK
    qb = jnp.moveaxis(q.reshape(B, nb, Q_BLOCK, H, D), 1, 0)

    def blk(qi):
        p = jax.nn.softmax(jnp.einsum('bqhd,bkhd->bhqk', qi, k) * D ** -0.5, axis=-1)
        return jnp.einsum('bhqk,bkhd->bqhd', p, v)

    return jnp.moveaxis(lax.map(blk, qb), 0, 1).reshape(B, Lq, H * D)


def _nat_latent(q, k, v, kc, vc, rpb):
    B, L, H, D = q.shape
    rows = L // GRID_W
    wh = min(NAT_WH, rows)
    qg = q.reshape(B, rows, GRID_W, H, D)
    kg = k.reshape(B, rows, GRID_W, H, D)
    vg = v.reshape(B, rows, GRID_W, H, D)
    colv = np.arange(GRID_W)
    cstart = np.clip(colv - NAT_WW // 2, 0, GRID_W - NAT_WW)
    col_mask = (colv[None, :] >= cstart[:, None]) & (colv[None, :] < cstart[:, None] + NAT_WW)
    col_idx = np.clip(colv[None, :] - colv[:, None] + NAT_WW - 1, 0, 2 * NAT_WW - 2)
    rpb_c = rpb.astype(f32)[:, :, col_idx]
    scale = D ** -0.5
    nkey = wh * GRID_W

    def row(r):
        rs = jnp.clip(r - wh // 2, 0, rows - wh)
        kr = lax.dynamic_slice_in_dim(kg, rs, wh, axis=1)
        vr = lax.dynamic_slice_in_dim(vg, rs, wh, axis=1)
        qr = lax.dynamic_index_in_dim(qg, r, axis=1, keepdims=False)
        row_idx = rs + jnp.arange(wh) - r + (NAT_WH - 1)
        bias = jnp.take(rpb_c, row_idx, axis=1).transpose(0, 2, 1, 3)
        s = jnp.einsum('bqhd,bwkhd->bhqwk', qr, kr) * scale + bias[None]
        s = jnp.where(col_mask[None, None, :, None, :], s, -jnp.inf).reshape(B, H, GRID_W, nkey)
        sc = jnp.einsum('bqhd,bkhd->bhqk', qr, kc) * scale
        p = jax.nn.softmax(jnp.concatenate([s, sc], axis=-1), axis=-1)
        return (jnp.einsum('bhqk,bkhd->bqhd', p[..., :nkey], vr.reshape(B, nkey, H, D))
                + jnp.einsum('bhqk,bkhd->bqhd', p[..., nkey:], vc))

    o = lax.map(row, jnp.arange(rows))
    return jnp.moveaxis(o, 0, 1).reshape(B, L, H * D)


def kernel(x_prompt, x_sample, cache_diff_k, cache_diff_v, cache_nat_k, cache_nat_v, state_gla, c, c_ctx, w_ada, b_ada, norm1, w_in, pool_w, pool_scale, gla_w_up, gla_b_up, gla_norm, diff_lambda, diff_norm, nat_rpb, w_out, norm2, router_group_w, router_group_b, router_expert_w, router_expert_b, expert_w_gate, expert_w_up, expert_w_down, norm_final):
    Bp, Lp, Bs, Ls = BATCH, SEQ, DEC_BATCH, DEC_SEQ
    x = jnp.concatenate([x_prompt.reshape(N_PROMPT, D_MODEL), x_sample.reshape(N_SAMPLE, D_MODEL)], axis=0)
    cvec = jnp.zeros((MOD_ROWS, D_MODEL), f32).at[0].set(c_ctx).at[1:1 + DEC_BATCH].set(c)
    mod_all = ada_modulation(cvec, w_ada, b_ada).reshape(DEPTH, MOD_ROWS, 6, D_MODEL)

    new_dk, new_dv, new_nk, new_nv, new_gs = [], [], [], [], []
    for l in range(DEPTH):
        mod = mod_all[l]
        lam_init = 0.8 - 0.6 * math.exp(-0.3 * l)
        lv = diff_lambda[l].astype(f32)
        lam = jnp.exp(jnp.sum(lv[0] * lv[1])) - jnp.exp(jnp.sum(lv[2] * lv[3])) + lam_init

        w_l = w_in[l]
        w_perm = jnp.concatenate(
            [w_l[:, :ORIG_GD], w_l[:, ORIG_GD + 2 * GLA_RANK:], w_l[:, ORIG_GD:ORIG_GD + 2 * GLA_RANK],
             jnp.zeros((D_MODEL, D_IN_PAD - C_GD - 2 * GLA_RANK), f32)], axis=1).astype(bf16)
        u = input_projection(x, norm1[l], mod, w_perm)

        u_seq = u.reshape(N_TOK // Lp, Lp, D_IN_PAD)
        up = u[:N_PROMPT]

        zero = jnp.zeros((Bp, GLA_HEADS, GLA_DK, GLA_DV), f32)
        o_gla_p, gs = gla_mixer(u, 0, Bp, Lp, Lp, gla_w_up[l], gla_b_up[l], gla_norm[l], zero, zero)
        q2_p, kb_p = diff_prep(u, 0, N_PROMPT, rope=False)
        o_diff_p = flash_attention(q2_p, kb_p.reshape(Bp, Lp, GROUP_W), u_seq, n_batch=Bp, lq=Lp, tq=Lp, tk=Lp,
                                   qcol=0, kcol=0, vcol=C_DV // LANES, lam=lam, g=diff_norm[l],
                                   out_scale=1.0 - lam_init)
        o_nat_p = flash_attention(u.reshape(1, N_TOK, D_IN_PAD), u_seq, u_seq, n_batch=Bp, lq=Lp, tq=Lp, tk=Lp,
                                  qcol=C_NQ // LANES, kcol=C_NK // LANES, vcol=C_NV // LANES,
                                  scale=NAT_HD ** -0.5)
        o_pool_p = pool_mixer(u, 0, Bp, Lp, pool_w[l], pool_scale[l])
        new_dk.append(up[:, C_DK:C_DK + GROUP_W].reshape(Bp, Lp, DIFF_HEADS, 2 * DIFF_HD))
        new_dv.append(up[:, C_DV:C_DV + GROUP_W].reshape(Bp, Lp, DIFF_HEADS, DIFF_VD))
        new_nk.append(up[:, C_NK:C_NK + GROUP_W].reshape(Bp, Lp, NAT_HEADS, NAT_HD))
        new_nv.append(up[:, C_NV:C_NV + GROUP_W].reshape(Bp, Lp, NAT_HEADS, NAT_HD))
        new_gs.append(gs)

        st = state_gla[:, l].astype(f32)
        o_gla_s, _ = gla_mixer(u, N_PROMPT, Bs, Ls, 512, gla_w_up[l], gla_b_up[l], gla_norm[l],
                               st[:, 0], st[:, 1])
        q2_s, kb_s = diff_prep(u, N_PROMPT, N_SAMPLE, rope=True)
        k_all = jnp.concatenate([kb_s.reshape(Bs, Ls, GROUP_W),
                                 cache_diff_k[:, l].reshape(Bs, PAST_LEN, GROUP_W).astype(bf16)], axis=1)
        v_all = jnp.concatenate([u[N_PROMPT:, C_DV:C_DV + GROUP_W].astype(bf16).reshape(Bs, Ls, GROUP_W),
                                 cache_diff_v[:, l].reshape(Bs, PAST_LEN, GROUP_W).astype(bf16)], axis=1)
        o_diff_s = flash_attention(q2_s, k_all, jnp.swapaxes(v_all, 1, 2), n_batch=Bs, lq=Ls, tq=2048, tk=1536,
                                   qcol=0, kcol=0, vcol=0, v_transposed=True,
                                   lam=lam, g=diff_norm[l], out_scale=1.0 - lam_init)
        o_nat_s = nat_attention(u, N_PROMPT, cache_nat_k[:, l].reshape(Bs, PAST_LEN, GROUP_W),
                                cache_nat_v[:, l].reshape(Bs, PAST_LEN, GROUP_W), nat_rpb[l])
        o_pool_s = pool_mixer(u, N_PROMPT, Bs, Ls, pool_w[l], pool_scale[l])

        def both(a, b):
            return jnp.concatenate([a, b], axis=0)

        parts = [both(o_pool_p, o_pool_s), both(o_gla_p, o_gla_s), both(o_diff_p, o_diff_s),
                 both(o_nat_p, o_nat_s)]
        x = output_projection(parts, w_out[l].astype(bf16), x, mod)

        wr = jnp.zeros((D_MODEL, LANES), f32)
        wr = wr.at[:, :MOE_GROUPS].set(router_group_w[l]).at[:, MOE_GROUPS:MOE_GROUPS + MOE_EXPERTS].set(
            router_expert_w[l])
        br = jnp.zeros((1, LANES), f32)
        br = br.at[0, :MOE_GROUPS].set(router_group_b[l]).at[0, MOE_GROUPS:MOE_GROUPS + MOE_EXPERTS].set(
            router_expert_b[l])
        h2, route, counts = moe_router(x, norm2[l], mod, wr, br)
        pos, slot_token, tile_expert, n_used = route_layout(route, counts)
        xs = h2.at[slot_token].get(mode="promise_in_bounds")
        ys = expert_ffn(l, tile_expert, n_used, xs, expert_w_gate, expert_w_up, expert_w_down)
        y0 = ys.at[pos[:, 0]].get(mode="promise_in_bounds")
        y1 = ys.at[pos[:, 1]].get(mode="promise_in_bounds")
        x = moe_combine(x, y0, y1, route, mod, norm_final if l == DEPTH - 1 else None)

    y_prompt = x[:N_PROMPT].reshape(Bp, Lp, D_MODEL)
    y_sample = x[N_PROMPT:].reshape(Bs, Ls, D_MODEL)
    return (y_prompt, y_sample, jnp.stack(new_dk, axis=1), jnp.stack(new_dv, axis=1),
            jnp.stack(new_nk, axis=1), jnp.stack(new_nv, axis=1), jnp.stack(new_gs, axis=1))
```

```python
import functools
import math

import jax
import jax.numpy as jnp
import numpy as np
from jax import lax
from jax.experimental import pallas as pl
from jax.experimental.pallas import tpu as pltpu

f32 = jnp.float32
bf16 = jnp.bfloat16

D_MODEL = 4096
BATCH = 32
SEQ = 256
DEPTH = 2
DEC_BATCH = 4
DEC_SEQ = 4096
PAST_LEN = 512
GRID_W = 64
GROUP_W = D_MODEL // 4
POOL_WINDOWS = (2, 4, 8, 16)
POOL_GW = GROUP_W // 4
GLA_HEADS = 8
GLA_DV = GROUP_W // GLA_HEADS
GLA_DK = GLA_DV // 2
GLA_RANK = 16
GLA_GATE_NORM = 16.0
GLA_CHUNK = 64
DIFF_HEADS = 8
DIFF_VD = GROUP_W // DIFF_HEADS
DIFF_HD = DIFF_VD // 2
ROPE_BASE = 10000.0
NAT_HEADS = 8
NAT_HD = GROUP_W // NAT_HEADS
NAT_WH = 8
NAT_WW = 16
MOE_GROUPS = 4
MOE_PER_GROUP = 8
MOE_EXPERTS = MOE_GROUPS * MOE_PER_GROUP
MOE_TOPK = 2
MOE_FF = 512
Q_BLOCK = 128
EPS = 1e-6

N_PROMPT = BATCH * SEQ
N_SAMPLE = DEC_BATCH * DEC_SEQ
N_TOK = N_PROMPT + N_SAMPLE
LANES = 128
MOD_ROWS = 8

C_POOL, C_GQ, C_GK, C_GV, C_GG = 0, 1024, 1536, 2048, 3072
C_DQ, C_DK, C_DV, C_NQ, C_NK, C_NV, C_GD = 4096, 5120, 6144, 7168, 8192, 9216, 10240
D_IN_PAD = 10752
ORIG_GD = 3 * GROUP_W + 2 * GLA_HEADS * GLA_DK

TM = 512
TN_IN = 768
TN_OUT = 1024
TM_E = 256
N_ASSIGN = N_TOK * MOE_TOPK
N_ETILES = N_ASSIGN // TM_E + MOE_EXPERTS
VMEM_LIMIT = 56 * 1024 * 1024


def _cparams(sem):
    return pltpu.CompilerParams(dimension_semantics=sem, vmem_limit_bytes=VMEM_LIMIT)


def _mod_row(i):
    npt = N_PROMPT // TM
    return jnp.where(i < npt, 0, 1 + (i - npt) // (DEC_SEQ // TM))


def _ada_kernel(c_ref, w_ref, b_ref, o_ref):
    @pl.when(pl.program_id(2) == 0)
    def _():
        o_ref[0] = jnp.broadcast_to(b_ref[0], o_ref.shape[1:])

    c = c_ref[...]
    a = (c * jax.nn.sigmoid(c)).astype(bf16)
    o_ref[0] += jnp.dot(a, w_ref[0].astype(bf16), preferred_element_type=f32)


def ada_modulation(cvec, w_ada, b_ada):
    tk, tn = 2048, 1024
    n6 = 6 * D_MODEL
    return pl.pallas_call(
        _ada_kernel,
        grid=(DEPTH, n6 // tn, D_MODEL // tk),
        in_specs=[pl.BlockSpec((MOD_ROWS, tk), lambda l, j, k: (0, k)),
                  pl.BlockSpec((1, tk, tn), lambda l, j, k: (l, k, j)),
                  pl.BlockSpec((1, 1, tn), lambda l, j, k: (l, 0, j))],
        out_specs=pl.BlockSpec((1, MOD_ROWS, tn), lambda l, j, k: (l, 0, j)),
        out_shape=jax.ShapeDtypeStruct((DEPTH, MOD_ROWS, n6), f32),
        compiler_params=_cparams(("parallel", "parallel", "arbitrary")),
        name="ada_modulation",
    )(cvec, w_ada, b_ada.reshape(DEPTH, 1, n6))


def _modulated_norm(x, g, shift, scale):
    y = x * lax.rsqrt(jnp.mean(x * x, axis=-1, keepdims=True) + EPS) * g
    return y * (1.0 + scale) + shift


def _win_kernel(x_ref, g_ref, mod_ref, w_ref, o_ref, h_ref):
    @pl.when(pl.program_id(1) == 0)
    def _():
        h = _modulated_norm(x_ref[...], g_ref[...], mod_ref[0, 0:1, :], mod_ref[0, 1:2, :])
        h_ref[...] = h.astype(bf16)

    o_ref[...] = jnp.dot(h_ref[...], w_ref[...], preferred_element_type=f32)


def input_projection(x, g_norm, mod, w):
    return pl.pallas_call(
        _win_kernel,
        grid=(N_TOK // TM, D_IN_PAD // TN_IN),
        in_specs=[pl.BlockSpec((TM, D_MODEL), lambda i, j: (i, 0)),
                  pl.BlockSpec((1, D_MODEL), lambda i, j: (0, 0)),
                  pl.BlockSpec((1, 6, D_MODEL), lambda i, j: (_mod_row(i), 0, 0)),
                  pl.BlockSpec((D_MODEL, TN_IN), lambda i, j: (0, j))],
        out_specs=pl.BlockSpec((TM, TN_IN), lambda i, j: (i, j)),
        out_shape=jax.ShapeDtypeStruct((N_TOK, D_IN_PAD), f32),
        scratch_shapes=[pltpu.VMEM((TM, D_MODEL), bf16)],
        compiler_params=_cparams(("parallel", "arbitrary")),
        name="input_projection",
    )(x, g_norm.reshape(1, D_MODEL), mod, w)


def _wout_kernel(a0, a1, a2, a3, w0, w1, w2, w3, x_ref, mod_ref, o_ref):
    acc = jnp.dot(a0[...], w0[...], preferred_element_type=f32)
    acc += jnp.dot(a1[...], w1[...], preferred_element_type=f32)
    acc += jnp.dot(a2[...], w2[...], preferred_element_type=f32)
    acc += jnp.dot(a3[...], w3[...], preferred_element_type=f32)
    o_ref[...] = x_ref[...] + mod_ref[0, 2:3, :] * acc


def output_projection(parts, w, x, mod):
    a_specs = [pl.BlockSpec((TM, GROUP_W), lambda i, j: (i, 0)) for _ in range(4)]
    w_specs = [pl.BlockSpec((GROUP_W, TN_OUT), functools.partial(lambda i, j, c: (c, j), c=c))
               for c in range(4)]
    return pl.pallas_call(
        _wout_kernel,
        grid=(N_TOK // TM, D_MODEL // TN_OUT),
        in_specs=a_specs + w_specs + [
            pl.BlockSpec((TM, TN_OUT), lambda i, j: (i, j)),
            pl.BlockSpec((1, 6, TN_OUT), lambda i, j: (_mod_row(i), 0, j))],
        out_specs=pl.BlockSpec((TM, TN_OUT), lambda i, j: (i, j)),
        out_shape=jax.ShapeDtypeStruct((N_TOK, D_MODEL), f32),
        compiler_params=_cparams(("parallel", "arbitrary")),
        name="output_projection",
    )(*parts, w, w, w, w, x, mod)


def _split_bf16(a):
    hi = a.astype(bf16)
    lo = (a - hi.astype(f32)).astype(bf16)
    return hi, lo


def _router_kernel(x_ref, g_ref, mod_ref, wr_ref, br_ref, h_ref, r_ref, cnt_ref):
    h = _modulated_norm(x_ref[...], g_ref[...], mod_ref[0, 3:4, :], mod_ref[0, 4:5, :])
    h_ref[...] = h
    h_hi, h_lo = _split_bf16(h)
    w_hi, w_lo = _split_bf16(wr_ref[...])
    lg = (jnp.dot(h_hi, w_hi, preferred_element_type=f32)
          + jnp.dot(h_lo, w_hi, preferred_element_type=f32)
          + jnp.dot(h_hi, w_lo, preferred_element_type=f32)) + br_ref[...]
    lane = lax.broadcasted_iota(jnp.int32, lg.shape, 1).astype(f32)
    ninf = -jnp.inf

    def first_max(v):
        m = jnp.max(v, axis=-1, keepdims=True)
        idx = jnp.min(jnp.where(v == m, lane, float(LANES)), axis=-1, keepdims=True)
        return m, idx

    gmask = lane < MOE_GROUPS
    mg, gi = first_max(jnp.where(gmask, lg, ninf))
    p_top = 1.0 / jnp.sum(jnp.where(gmask, jnp.exp(lg - mg), 0.0), axis=-1, keepdims=True)
    lo = MOE_GROUPS + gi * MOE_PER_GROUP
    le = jnp.where((lane >= lo) & (lane < lo + MOE_PER_GROUP), lg, ninf)
    v1, i1 = first_max(le)
    v2, i2 = first_max(jnp.where(lane == i1, ninf, le))
    t = jnp.exp(v2 - v1)
    w1 = p_top / (1.0 + t)
    w2 = w1 * t

    @pl.when(pl.program_id(0) == 0)
    def _():
        cnt_ref[...] = jnp.zeros(cnt_ref.shape, f32)

    tm = lg.shape[0]
    chosen = jnp.where((lane == i1) | (lane == i2), 1.0, 0.0)
    earlier = (lax.broadcasted_iota(jnp.int32, (tm, tm), 1) < lax.broadcasted_iota(jnp.int32, (tm, tm), 0))
    before = jnp.dot(jnp.where(earlier, 1.0, 0.0).astype(bf16), chosen.astype(bf16),
                     preferred_element_type=f32) + cnt_ref[0:1, :]
    rank1 = jnp.sum(jnp.where(lane == i1, before, 0.0), axis=-1, keepdims=True)
    rank2 = jnp.sum(jnp.where(lane == i2, before, 0.0), axis=-1, keepdims=True)
    cnt_ref[...] = cnt_ref[...] + jnp.sum(chosen, axis=0, keepdims=True)

    slab = jnp.zeros(lg.shape, f32)
    for pos, val in enumerate((i1 - MOE_GROUPS, i2 - MOE_GROUPS, w1, w2, rank1, rank2)):
        slab = jnp.where(lane == float(pos), val, slab)
    r_ref[...] = slab


def moe_router(x, g_norm, mod, wr, br):
    tm = 256
    return pl.pallas_call(
        _router_kernel,
        grid=(N_TOK // tm,),
        in_specs=[pl.BlockSpec((tm, D_MODEL), lambda i: (i, 0)),
                  pl.BlockSpec((1, D_MODEL), lambda i: (0, 0)),
                  pl.BlockSpec((1, 6, D_MODEL), lambda i: (_mod_row(i // (TM // tm)), 0, 0)),
                  pl.BlockSpec((D_MODEL, LANES), lambda i: (0, 0)),
                  pl.BlockSpec((1, LANES), lambda i: (0, 0))],
        out_specs=[pl.BlockSpec((tm, D_MODEL), lambda i: (i, 0)),
                   pl.BlockSpec((tm, LANES), lambda i: (i, 0)),
                   pl.BlockSpec((MOD_ROWS, LANES), lambda i: (0, 0))],
        out_shape=[jax.ShapeDtypeStruct((N_TOK, D_MODEL), f32),
                   jax.ShapeDtypeStruct((N_TOK, LANES), f32),
                   jax.ShapeDtypeStruct((MOD_ROWS, LANES), f32)],
        compiler_params=_cparams(("arbitrary",)),
        name="moe_router",
    )(x, g_norm.reshape(1, D_MODEL), mod, wr, br)


def _ffn_kernel(te_ref, nu_ref, tok_ref, h_hbm, wg_ref, wu_ref, wd_ref, o_ref, xbuf, sem, wg_b, wu_b, wd_b):
    t = pl.program_id(0)
    n_used = nu_ref[0]

    def row_copy(tile, r):
        half = tile % 2
        return pltpu.make_async_copy(h_hbm.at[pl.ds(tok_ref[tile * TM_E + r], 1)],
                                     xbuf.at[half, pl.ds(r, 1)], sem.at[half])

    def for_rows(tile, fn):
        def body(r, carry):
            fn(row_copy(tile, r))
            return carry
        lax.fori_loop(0, TM_E, body, 0, unroll=8)

    @pl.when(t == 0)
    def _():
        for_rows(t, lambda cp: cp.start())

    @pl.when(t + 1 < n_used)
    def _():
        for_rows(t + 1, lambda cp: cp.start())

    @pl.when(t < n_used)
    def _():
        @pl.when((t == 0) | (te_ref[t] != te_ref[jnp.maximum(t - 1, 0)]))
        def _():
            wg_b[...] = wg_ref[0, 0].astype(bf16)
            wu_b[...] = wu_ref[0, 0].astype(bf16)
            wd_b[...] = wd_ref[0, 0].astype(bf16)

        for_rows(t, lambda cp: cp.wait())
        x = xbuf[t % 2].astype(bf16)
        a = jnp.dot(x, wg_b[...], preferred_element_type=f32)
        u = jnp.dot(x, wu_b[...], preferred_element_type=f32)
        mid = (a * jax.nn.sigmoid(a) * u).astype(bf16)
        o_ref[...] = jnp.dot(mid, wd_b[...], preferred_element_type=f32).astype(o_ref.dtype)

    @pl.when(t >= nu_ref[0])
    def _():
        o_ref[...] = jnp.zeros(o_ref.shape, o_ref.dtype)


def expert_ffn(layer, tile_expert, n_used, slot_token, h, wg, wu, wd):
    def w_spec(r, c):
        return pl.BlockSpec((1, 1, r, c), lambda t, te, nu, tok: (layer, te[t], 0, 0),
                            pipeline_mode=pl.Buffered(1))

    return pl.pallas_call(
        _ffn_kernel,
        grid_spec=pltpu.PrefetchScalarGridSpec(
            num_scalar_prefetch=3,
            grid=(N_ETILES,),
            in_specs=[pl.BlockSpec(memory_space=pl.ANY),
                      w_spec(D_MODEL, MOE_FF), w_spec(D_MODEL, MOE_FF), w_spec(MOE_FF, D_MODEL)],
            out_specs=pl.BlockSpec((TM_E, D_MODEL), lambda t, te, nu, tok: (t, 0)),
            scratch_shapes=[pltpu.VMEM((2, TM_E, D_MODEL), f32), pltpu.SemaphoreType.DMA((2,)),
                            pltpu.VMEM((D_MODEL, MOE_FF), bf16), pltpu.VMEM((D_MODEL, MOE_FF), bf16),
                            pltpu.VMEM((MOE_FF, D_MODEL), bf16)]),
        out_shape=jax.ShapeDtypeStruct((N_ETILES * TM_E, D_MODEL), bf16),
        compiler_params=_cparams(("arbitrary",)),
        name="expert_ffn",
    )(tile_expert, n_used, slot_token, h, wg, wu, wd)


def route_layout(route, counts):
    e = route[:, :MOE_TOPK].astype(jnp.int32)
    rank = route[:, 4:4 + MOE_TOPK].astype(jnp.int32)
    cnt = counts[0, MOE_GROUPS:MOE_GROUPS + MOE_EXPERTS].astype(jnp.int32)
    tiles_e = (cnt + TM_E - 1) // TM_E
    tile_end = jnp.cumsum(tiles_e)
    tile_start = tile_end - tiles_e
    pos = tile_start[e] * TM_E + rank
    slot_token = jnp.zeros((N_ETILES * TM_E,), jnp.int32).at[pos.reshape(-1)].set(
        jnp.arange(N_ASSIGN, dtype=jnp.int32) // MOE_TOPK)
    n_used = tile_end[-1]
    t = jnp.minimum(jnp.arange(N_ETILES, dtype=jnp.int32), n_used - 1)
    tile_expert = jnp.sum((t[:, None] >= tile_end[None, :]).astype(jnp.int32), axis=1)
    tile_expert = jnp.minimum(tile_expert, MOE_EXPERTS - 1)
    return pos, slot_token, tile_expert, n_used.reshape(1).astype(jnp.int32)


def _combine_kernel(x_ref, y0_ref, y1_ref, r_ref, mod_ref, o_ref):
    y = r_ref[:, 2:3] * y0_ref[...] + r_ref[:, 3:4] * y1_ref[...]
    o_ref[...] = x_ref[...] + mod_ref[0, 5:6, :] * y


def _combine_norm_kernel(x_ref, y0_ref, y1_ref, r_ref, mod_ref, g_ref, o_ref):
    y = r_ref[:, 2:3] * y0_ref[...] + r_ref[:, 3:4] * y1_ref[...]
    x = x_ref[...] + mod_ref[0, 5:6, :] * y
    o_ref[...] = x * lax.rsqrt(jnp.mean(x * x, axis=-1, keepdims=True) + EPS) * g_ref[...]


def moe_combine(x, y0, y1, route, mod, g_final=None):
    tm = 256
    row = pl.BlockSpec((tm, D_MODEL), lambda i: (i, 0))
    in_specs = [row, row, row, pl.BlockSpec((tm, LANES), lambda i: (i, 0)),
                pl.BlockSpec((1, 6, D_MODEL), lambda i: (_mod_row(i // (TM // tm)), 0, 0))]
    args = [x, y0, y1, route, mod]
    body = _combine_kernel
    if g_final is not None:
        in_specs.append(pl.BlockSpec((1, D_MODEL), lambda i: (0, 0)))
        args.append(g_final.reshape(1, D_MODEL))
        body = _combine_norm_kernel
    return pl.pallas_call(
        body,
        grid=(N_TOK // tm,),
        in_specs=in_specs,
        out_specs=row,
        out_shape=jax.ShapeDtypeStruct((N_TOK, D_MODEL), f32),
        compiler_params=_cparams(("parallel",)),
        name="moe_combine",
    )(*args)


def _diff_prep_kernel(rope, q_ref, k_ref, cos_ref, sin_ref, q2_ref, kb_ref):
    lane = lax.broadcasted_iota(jnp.int32, q_ref.shape, 1)

    def rot(x):
        if not rope:
            return x
        partner = jnp.where((lane & 31) >= 16, pltpu.roll(x, 16, 1), pltpu.roll(x, LANES - 16, 1))
        return x * cos_ref[...] + partner * sin_ref[...]

    q = rot(q_ref[...]) * DIFF_HD ** -0.5
    first = lane < DIFF_HD
    q2_ref[0] = jnp.where(first, q, 0.0).astype(bf16)
    q2_ref[1] = jnp.where(first, 0.0, q).astype(bf16)
    kb_ref[...] = rot(k_ref[...]).astype(bf16)


def _rope_tables():
    t = jnp.arange(DEC_SEQ)
    row = (t // GRID_W).astype(f32)
    col = (t % GRID_W).astype(f32)
    nf = DIFF_HD // 4
    inv = ROPE_BASE ** (-jnp.arange(nf, dtype=f32) / nf)
    lane = np.arange(LANES)
    pos = jnp.where(((lane // 32) % 2 == 0)[None, :], row[:, None], col[:, None])
    ang = pos * inv[lane % nf][None, :]
    sign = np.where(lane % 32 >= nf, 1.0, -1.0).astype(np.float32)
    return jnp.cos(ang), jnp.sin(ang) * sign[None, :]


def diff_prep(u, row0, n_rows, rope):
    t = 1024
    r0 = row0 // t
    nseq = DEC_SEQ // t
    cos, sin = _rope_tables() if rope else (jnp.zeros((t, LANES), f32), jnp.zeros((t, LANES), f32))
    tab = pl.BlockSpec((t, LANES), (lambda i, h: (i % nseq, 0)) if rope else (lambda i, h: (0, 0)))
    return pl.pallas_call(
        functools.partial(_diff_prep_kernel, rope),
        grid=(n_rows // t, DIFF_HEADS),
        in_specs=[pl.BlockSpec((t, LANES), lambda i, h: (r0 + i, C_DQ // LANES + h)),
                  pl.BlockSpec((t, LANES), lambda i, h: (r0 + i, C_DK // LANES + h)),
                  tab, tab],
        out_specs=[pl.BlockSpec((2, t, LANES), lambda i, h: (0, i, h)),
                   pl.BlockSpec((t, LANES), lambda i, h: (i, h))],
        out_shape=[jax.ShapeDtypeStruct((2, n_rows, GROUP_W), bf16),
                   jax.ShapeDtypeStruct((n_rows, GROUP_W), bf16)],
        compiler_params=_cparams(("parallel", "parallel")),
        name="diff_prep",
    )(u, u, cos, sin)


def _flash_kernel(n_stack, tq, sub, scale, out_scale, v_transposed, q_ref, k_ref, v_ref, lam_ref, g_ref, o_ref,
                  m_ref, l_ref, acc_ref):
    kj = pl.program_id(3)

    @pl.when(kj == 0)
    def _():
        m_ref[...] = jnp.full(m_ref.shape, -jnp.inf, f32)
        l_ref[...] = jnp.zeros(l_ref.shape, f32)
        acc_ref[...] = jnp.zeros(acc_ref.shape, f32)

    q = q_ref[...].reshape(n_stack * tq, LANES).astype(bf16)
    tk = k_ref.shape[1]
    m, l, acc = m_ref[...], l_ref[...], acc_ref[...]
    for c in range(tk // sub):
        keys = slice(c * sub, (c + 1) * sub)
        s = lax.dot_general(k_ref[0, keys, :].astype(bf16), q, (((1,), (1,)), ((), ())),
                            preferred_element_type=f32)
        if scale != 1.0:
            s = s * scale
        m_new = jnp.maximum(m, jnp.max(s, axis=0, keepdims=True))
        alpha = jnp.exp(m - m_new)
        p = jnp.exp(s - m_new)
        l = alpha * l + jnp.sum(p, axis=0, keepdims=True)
        vt = v_ref[0, :, keys].astype(bf16) if v_transposed else v_ref[0, keys, :].T.astype(bf16)
        acc = alpha * acc + jnp.dot(vt, p.astype(bf16), preferred_element_type=f32)
        m = m_new
    m_ref[...], l_ref[...], acc_ref[...] = m, l, acc

    @pl.when(kj == pl.num_programs(3) - 1)
    def _():
        o = acc_ref[...] / l_ref[...]
        if n_stack == 2:
            o = o[:, :tq] - lam_ref[0:1, 0:1] * o[:, tq:]
            o = o * lax.rsqrt(jnp.mean(o * o, axis=0, keepdims=True) + EPS)
            o = o.T * (g_ref[...] * out_scale)
        else:
            o = o.T
        o_ref[...] = o.astype(o_ref.dtype)


def flash_attention(q, k, v, *, n_batch, lq, tq, tk, qcol, kcol, vcol, sub=512, v_transposed=False, qrow0=0, kb0=0,
                    scale=1.0, lam=None, g=None, out_scale=1.0):
    n_stack = q.shape[0]
    lk = k.shape[1]
    nq = lq // tq
    lam = jnp.zeros((1, LANES), f32) if lam is None else jnp.full((1, LANES), lam, f32)
    g = jnp.ones((1, LANES), f32) if g is None else g.reshape(1, LANES).astype(f32)
    rows = n_stack * tq
    if v_transposed:
        v_spec = pl.BlockSpec((1, LANES, tk), lambda b, h, i, j: (kb0 + b, vcol + h, j))
    else:
        v_spec = pl.BlockSpec((1, tk, LANES), lambda b, h, i, j: (kb0 + b, j, vcol + h))
    return pl.pallas_call(
        functools.partial(_flash_kernel, n_stack, tq, min(sub, tk), scale, out_scale, v_transposed),
        grid=(n_batch, DIFF_HEADS, nq, lk // tk),
        in_specs=[pl.BlockSpec((n_stack, tq, LANES), lambda b, h, i, j: (0, qrow0 + b * nq + i, qcol + h)),
                  pl.BlockSpec((1, tk, LANES), lambda b, h, i, j: (kb0 + b, j, kcol + h)),
                  v_spec,
                  pl.BlockSpec((1, LANES), lambda b, h, i, j: (0, 0)),
                  pl.BlockSpec((1, LANES), lambda b, h, i, j: (0, 0))],
        out_specs=pl.BlockSpec((tq, LANES), lambda b, h, i, j: (b * nq + i, h)),
        out_shape=jax.ShapeDtypeStruct((n_batch * lq, GROUP_W), bf16),
        scratch_shapes=[pltpu.VMEM((1, rows), f32), pltpu.VMEM((1, rows), f32),
                        pltpu.VMEM((LANES, rows), f32)],
        compiler_params=_cparams(("parallel", "parallel", "parallel", "arbitrary")),
        name="flash_attention",
    )(q, k, v, lam, g)


NAT_ROWS = DEC_SEQ // GRID_W
NAT_RB = 8
NAT_BAND = NAT_WH * GRID_W
NEG_BIG = -1e30


def _nat_kernel(q_ref, k_ref, v_ref, kc_ref, vc_ref, bias_ref, o_ref):
    rb = pl.program_id(2)
    scale = NAT_HD ** -0.5
    kc = kc_ref[0].astype(bf16)
    vc = vc_ref[0].astype(bf16)
    nt = (((1,), (1,)), ((), ()))
    for i in range(NAT_RB):
        r = rb * NAT_RB + i
        rs = jnp.clip(r - NAT_WH // 2, 0, NAT_ROWS - NAT_WH)
        k0 = pl.multiple_of(rs * GRID_W, GRID_W)
        kband = k_ref[pl.ds(k0, NAT_BAND), :].astype(bf16)
        vband = v_ref[pl.ds(k0, NAT_BAND), :].astype(bf16)
        q = q_ref[i * GRID_W:(i + 1) * GRID_W, :].astype(bf16)
        sb = lax.dot_general(q, kband, nt, preferred_element_type=f32) * scale + bias_ref[rs - r + NAT_WH - 1, 0]
        sc = lax.dot_general(q, kc, nt, preferred_element_type=f32) * scale
        m = jnp.maximum(jnp.max(sb, axis=-1, keepdims=True), jnp.max(sc, axis=-1, keepdims=True))
        pb = jnp.exp(sb - m)
        pc = jnp.exp(sc - m)
        l = jnp.sum(pb, axis=-1, keepdims=True) + jnp.sum(pc, axis=-1, keepdims=True)
        o = (jnp.dot(pb.astype(bf16), vband, preferred_element_type=f32)
             + jnp.dot(pc.astype(bf16), vc, preferred_element_type=f32)) / l
        o_ref[i * GRID_W:(i + 1) * GRID_W, :] = o.astype(o_ref.dtype)


def _nat_bias_table(rpb):
    colv = np.arange(GRID_W)
    cstart = np.clip(colv - NAT_WW // 2, 0, GRID_W - NAT_WW)
    col_mask = (colv[None, :] >= cstart[:, None]) & (colv[None, :] < cstart[:, None] + NAT_WW)
    col_idx = np.clip(colv[None, :] - colv[:, None] + NAT_WW - 1, 0, 2 * NAT_WW - 2)
    rpb_c = rpb.astype(f32)[:, :, col_idx]
    row_idx = np.arange(NAT_WH)[:, None] + np.arange(NAT_WH)[None, :]
    tab = rpb_c[:, row_idx]
    tab = jnp.where(col_mask[None, None, None], tab, NEG_BIG)
    return tab.transpose(1, 0, 3, 2, 4).reshape(NAT_WH, NAT_HEADS, GRID_W, NAT_BAND)


def nat_attention(u, row0, kc, vc, rpb):
    qblk = NAT_RB * GRID_W
    q0 = row0 // qblk
    b0 = row0 // DEC_SEQ
    return pl.pallas_call(
        _nat_kernel,
        grid=(DEC_BATCH, NAT_HEADS, NAT_ROWS // NAT_RB),
        in_specs=[pl.BlockSpec((qblk, LANES), lambda b, h, r: (q0 + b * (NAT_ROWS // NAT_RB) + r, C_NQ // LANES + h)),
                  pl.BlockSpec((DEC_SEQ, LANES), lambda b, h, r: (b0 + b, C_NK // LANES + h)),
                  pl.BlockSpec((DEC_SEQ, LANES), lambda b, h, r: (b0 + b, C_NV // LANES + h)),
                  pl.BlockSpec((1, PAST_LEN, LANES), lambda b, h, r: (b, 0, h)),
                  pl.BlockSpec((1, PAST_LEN, LANES), lambda b, h, r: (b, 0, h)),
                  pl.BlockSpec((NAT_WH, 1, GRID_W, NAT_BAND), lambda b, h, r: (0, h, 0, 0))],
        out_specs=pl.BlockSpec((qblk, LANES), lambda b, h, r: (b * (NAT_ROWS // NAT_RB) + r, h)),
        out_shape=jax.ShapeDtypeStruct((N_SAMPLE, GROUP_W), bf16),
        compiler_params=_cparams(("parallel", "parallel", "arbitrary")),
        name="nat_attention",
    )(u, u, u, kc, vc, _nat_bias_table(rpb))


def _pool_kernel(seq, u_ref, w_ref, s_ref, o_ref):
    grp = pl.program_id(1)
    for gi, win in enumerate(POOL_WINDOWS):
        @pl.when(grp == gi)
        def _(win=win):
            u = u_ref[...]
            t = lax.broadcasted_iota(jnp.int32, u.shape, 0)
            acc = jnp.zeros_like(u)
            for d in range(-(win // 2), win // 2):
                shifted = u if d == 0 else pltpu.roll(u, (-d) % seq, 0)
                acc += jnp.where((t + d >= 0) & (t + d < seq), shifted, 0.0)
            cnt = (jnp.minimum(t + win // 2, seq) - jnp.maximum(t - win // 2, 0)).astype(f32)
            p = acc / cnt - u
            y = jnp.dot(p.astype(bf16), w_ref[0].astype(bf16), preferred_element_type=f32) * s_ref[...]
            o_ref[...] = y.astype(o_ref.dtype)


def pool_mixer(u, row0, n_seq, seq, w, s):
    return pl.pallas_call(
        functools.partial(_pool_kernel, seq),
        grid=(n_seq, len(POOL_WINDOWS)),
        in_specs=[pl.BlockSpec((seq, POOL_GW), lambda b, g: (row0 // seq + b, g)),
                  pl.BlockSpec((1, POOL_GW, POOL_GW), lambda b, g: (g, 0, 0)),
                  pl.BlockSpec((1, POOL_GW), lambda b, g: (0, g))],
        out_specs=pl.BlockSpec((seq, POOL_GW), lambda b, g: (b, g)),
        out_shape=jax.ShapeDtypeStruct((n_seq * seq, GROUP_W), bf16),
        compiler_params=_cparams(("parallel", "parallel")),
        name="pool_mixer",
    )(u, w, s.reshape(1, GROUP_W))


GLA_QK = GLA_HEADS * GLA_DK


def _gla_kernel(reverse, final, n_chunks, *refs):
    if final:
        (q_ref, k_ref, v_ref, gd_ref, wup_ref, bup_ref, s0_ref, op_ref, gg_ref, gn_ref,
         o_ref, s_ref, st_ref) = refs
    else:
        q_ref, k_ref, v_ref, gd_ref, wup_ref, bup_ref, s0_ref, o_ref, s_ref, st_ref = refs
    j = pl.program_id(1)
    C = GLA_CHUNK

    @pl.when(j == 0)
    def _():
        for h in range(GLA_HEADS):
            s0 = s0_ref[0, h]
            z = jnp.zeros_like(s0)
            st_ref[h] = jnp.concatenate([s0, z] if h % 2 == 0 else [z, s0], axis=0).T

    ti = lax.broadcasted_iota(jnp.int32, (C, C), 0)
    tj = lax.broadcasted_iota(jnp.int32, (C, C), 1)
    tri = (ti <= tj) if reverse else (ti >= tj)
    tri_b = jnp.where(tri, 1.0, 0.0).astype(bf16)
    first_half = lax.broadcasted_iota(jnp.int32, (C, LANES), 1) < GLA_DK
    ref_row = C // 2 - 1 if reverse else C // 2
    end_row = 0 if reverse else C - 1
    nt = (((1,), (1,)), ((), ()))

    def chunk(ci, carry):
        c = n_chunks - 1 - ci if reverse else ci
        rows = pl.ds(pl.multiple_of(c * C, C), C)
        logit = jnp.dot(gd_ref[rows, :].astype(bf16), wup_ref[...], preferred_element_type=f32) + bup_ref[...]
        la = (jnp.minimum(logit, 0.0) - jnp.log1p(jnp.exp(-jnp.abs(logit)))) * (1.0 / GLA_GATE_NORM)
        hi = la.astype(bf16)
        r1 = la - hi.astype(f32)
        mid = r1.astype(bf16)
        lo = (r1 - mid.astype(f32)).astype(bf16)
        b = (jnp.dot(tri_b, hi, preferred_element_type=f32) + jnp.dot(tri_b, mid, preferred_element_type=f32)
             + jnp.dot(tri_b, lo, preferred_element_type=f32))
        bref = b[ref_row:ref_row + 1, :]
        bl = b[end_row:end_row + 1, :]
        q = q_ref[rows, :] * GLA_DK ** -0.5
        k = k_ref[rows, :]
        qs = q * jnp.exp(b - bref)
        ks = (k * jnp.exp(bref - b)).astype(bf16)
        qe = (q * jnp.exp(b)).astype(bf16)
        kd = k * jnp.exp(bl - b)
        ebl = jnp.exp(bl)
        for h in range(GLA_HEADS):
            pair = slice((h // 2) * LANES, (h // 2 + 1) * LANES)
            mine = first_half if h % 2 == 0 else jnp.logical_not(first_half)
            cols = slice(h * GLA_DV, (h + 1) * GLA_DV)
            a = lax.dot_general(jnp.where(mine, qs[:, pair], 0.0).astype(bf16), ks[:, pair], nt,
                                preferred_element_type=f32)
            a = jnp.where(tri, a, 0.0).astype(bf16)
            vh = v_ref[rows, cols]
            st = st_ref[h]
            o = (jnp.dot(a, vh.astype(bf16), preferred_element_type=f32)
                 + lax.dot_general(qe[:, pair], st.astype(bf16), nt, preferred_element_type=f32))
            st_ref[h] = st * ebl[:, pair] + jnp.dot(vh.T.astype(bf16),
                                                    jnp.where(mine, kd[:, pair], 0.0).astype(bf16),
                                                    preferred_element_type=f32)
            if final:
                o = o + op_ref[rows, cols]
                o = o * lax.rsqrt(jnp.mean(o * o, axis=-1, keepdims=True) + EPS) * gn_ref[...]
                gate = gg_ref[rows, cols]
                o = o * (gate * jax.nn.sigmoid(gate))
            o_ref[rows, cols] = o.astype(o_ref.dtype)
        return carry

    lax.fori_loop(0, n_chunks, chunk, 0)

    @pl.when(j == pl.num_programs(1) - 1)
    def _():
        for h in range(GLA_HEADS):
            half = (h % 2) * GLA_DK
            s_ref[0, h] = st_ref[h].T[half:half + GLA_DK, :]


def _gla_pass(u, row0, n_seq, seq, tb, z, w_up, b_up, s0, o_prev=None, g_norm=None):
    reverse = z == 1
    final = o_prev is not None
    nblk = seq // tb
    rb0 = row0 // tb
    wup = jnp.zeros((LANES, GLA_QK), f32).at[z * GLA_RANK:(z + 1) * GLA_RANK].set(w_up[z]).astype(bf16)

    def blk(b, j):
        return b * nblk + (nblk - 1 - j if reverse else j)

    def ucol(width, c0):
        return pl.BlockSpec((tb, width), lambda b, j: (rb0 + blk(b, j), c0 // width))

    in_specs = [ucol(GLA_QK, C_GQ), ucol(GLA_QK, C_GK), ucol(GROUP_W, C_GV), ucol(LANES, C_GD),
                pl.BlockSpec((LANES, GLA_QK), lambda b, j: (0, 0)),
                pl.BlockSpec((1, GLA_QK), lambda b, j: (0, 0)),
                pl.BlockSpec((1, GLA_HEADS, GLA_DK, GLA_DV), lambda b, j: (b, 0, 0, 0))]
    args = [u, u, u, u, wup, b_up[z].reshape(1, GLA_QK), s0]
    if final:
        in_specs += [pl.BlockSpec((tb, GROUP_W), lambda b, j: (blk(b, j), 0)), ucol(GROUP_W, C_GG),
                     pl.BlockSpec((1, GLA_DV), lambda b, j: (0, 0))]
        args += [o_prev, u, g_norm.reshape(1, GLA_DV)]
    return pl.pallas_call(
        functools.partial(_gla_kernel, reverse, final, tb // GLA_CHUNK),
        grid=(n_seq, nblk),
        in_specs=in_specs,
        out_specs=[pl.BlockSpec((tb, GROUP_W), lambda b, j: (blk(b, j), 0)),
                   pl.BlockSpec((1, GLA_HEADS, GLA_DK, GLA_DV), lambda b, j: (b, 0, 0, 0))],
        out_shape=[jax.ShapeDtypeStruct((n_seq * seq, GROUP_W), bf16 if final else f32),
                   jax.ShapeDtypeStruct((n_seq, GLA_HEADS, GLA_DK, GLA_DV), f32)],
        scratch_shapes=[pltpu.VMEM((GLA_HEADS, GLA_DV, LANES), f32)],
        compiler_params=_cparams(("parallel", "arbitrary")),
        name="gla_backward" if reverse else "gla_forward",
    )(*args)


def gla_mixer(u, row0, n_seq, seq, tb, w_up, b_up, g_norm, s0_f, s0_b):
    o_f, s_f = _gla_pass(u, row0, n_seq, seq, tb, 0, w_up, b_up, s0_f)
    o, s_b = _gla_pass(u, row0, n_seq, seq, tb, 1, w_up, b_up, s0_b, o_prev=o_f, g_norm=g_norm)
    return o, jnp.stack([s_f, s_b], axis=1)


def _head_rms(o, g):
    return o * lax.rsqrt(jnp.mean(o * o, axis=-1, keepdims=True) + EPS) * g.astype(f32)


def _pool_mixer(u, w, s):
    B, L, _ = u.shape
    cs = jnp.concatenate([jnp.zeros((B, 1, GROUP_W), u.dtype), jnp.cumsum(u, axis=1)], axis=1)
    t = jnp.arange(L)
    parts = []
    for i, win in enumerate(POOL_WINDOWS):
        lo = jnp.clip(t - win // 2, 0, L)
        hi = jnp.clip(t + win // 2, 0, L)
        cg = cs[..., i * POOL_GW:(i + 1) * POOL_GW]
        mean = (cg[:, hi] - cg[:, lo]) / (hi - lo).astype(u.dtype)[None, :, None]
        parts.append(mean - u[..., i * POOL_GW:(i + 1) * POOL_GW])
    p = jnp.stack(parts, axis=2)
    y = jnp.einsum('blgc,gcd->blgd', p, w.astype(u.dtype)).reshape(B, L, GROUP_W)
    return y * s.astype(u.dtype)


def _gla_scan(q, k, v, la, s0):
    B, L, H, _ = q.shape
    C = GLA_CHUNK
    n = L // C
    mask = jnp.tril(jnp.ones((C, C), dtype=bool))

    def chunks(a):
        return jnp.moveaxis(a.reshape(B, n, C, H, a.shape[-1]), 1, 0)

    def step(S, inp):
        qc, kc, vc, lac = inp
        b = jnp.cumsum(lac, axis=1)
        ref = b[:, C // 2:C // 2 + 1]
        a = jnp.einsum('bihd,bjhd->bhij', qc * jnp.exp(b - ref), kc * jnp.exp(ref - b))
        a = jnp.where(mask, a, 0.0)
        o = (jnp.einsum('bhij,bjhv->bihv', a, vc)
             + jnp.einsum('bihd,bhdv->bihv', qc * jnp.exp(b), S))
        bl = b[:, -1]
        S = (jnp.exp(bl)[..., None] * S
             + jnp.einsum('bjhd,bjhv->bhdv', kc * jnp.exp(bl[:, None] - b), vc))
        return S, o

    S, o = lax.scan(step, s0, (chunks(q), chunks(k), chunks(v), chunks(la)))
    return jnp.moveaxis(o, 0, 1).reshape(B, L, H, v.shape[-1]), S


def _gla_mixer(q_in, k_in, v_in, g_in, gd_in, w_up, b_up, g_norm, s0_f, s0_b):
    B, L, _ = q_in.shape
    q = q_in.reshape(B, L, GLA_HEADS, GLA_DK) * GLA_DK ** -0.5
    k = k_in.reshape(B, L, GLA_HEADS, GLA_DK)
    v = v_in.reshape(B, L, GLA_HEADS, GLA_DV)
    gd = gd_in.reshape(B, L, 2, GLA_RANK)
    logit = jnp.einsum('blzr,zre->blze', gd, w_up.astype(f32)) + b_up.astype(f32)
    la = (jax.nn.log_sigmoid(logit) / GLA_GATE_NORM).reshape(B, L, 2, GLA_HEADS, GLA_DK)
    o_f, s_f = _gla_scan(q, k, v, la[:, :, 0], s0_f)
    rev = lambda a: jnp.flip(a, axis=1)
    o_b, s_b = _gla_scan(rev(q), rev(k), rev(v), rev(la[:, :, 1]), s0_b)
    o = _head_rms(o_f + rev(o_b), g_norm).reshape(B, L, GROUP_W) * jax.nn.silu(g_in)
    return o, jnp.stack([s_f, s_b], axis=1)


def _axial_rope(x):
    L = x.shape[1]
    t = jnp.arange(L)
    row = (t // GRID_W).astype(f32)
    col = (t % GRID_W).astype(f32)
    half = DIFF_HD // 2
    nf = half // 2
    inv = ROPE_BASE ** (-jnp.arange(nf, dtype=f32) / nf)

    def rot(xh, pos):
        ang = pos[:, None] * inv
        cos = jnp.cos(ang)[None, :, None, None, :]
        sin = jnp.sin(ang)[None, :, None, None, :]
        x1, x2 = xh[..., :nf], xh[..., nf:]
        return jnp.concatenate([x1 * cos - x2 * sin, x2 * cos + x1 * sin], axis=-1)

    return jnp.concatenate([rot(x[..., :half], row), rot(x[..., half:], col)], axis=-1)


def _diff_attention(q, k, v, lam, lam_init, g_norm):
    B, Lq = q.shape[:2]
    nb = Lq // Q_BLOCK
    qb = jnp.moveaxis(q.reshape(B, nb, Q_BLOCK, DIFF_HEADS, 2, DIFF_HD), 1, 0)

    def blk(qi):
        s = jnp.einsum('bqhcd,bkhcd->bchqk', qi, k) * DIFF_HD ** -0.5
        p = jax.nn.softmax(s, axis=-1)
        a = p[:, 0] - lam * p[:, 1]
        return jnp.einsum('bhqk,bkhv->bqhv', a, v)

    o = jnp.moveaxis(lax.map(blk, qb), 0, 1).reshape(B, Lq, DIFF_HEADS, DIFF_VD)
    return (_head_rms(o, g_norm) * (1.0 - lam_init)).reshape(B, Lq, GROUP_W)


def _dense_attention(q, k, v):
    B, Lq, H, D = q.shape
    nb = Lq // Q_BLOCK
    qb = jnp.moveaxis(q.reshape(B, nb, Q_BLOCK, H, D), 1, 0)

    def blk(qi):
        p = jax.nn.softmax(jnp.einsum('bqhd,bkhd->bhqk', qi, k) * D ** -0.5, axis=-1)
        return jnp.einsum('bhqk,bkhd->bqhd', p, v)

    return jnp.moveaxis(lax.map(blk, qb), 0, 1).reshape(B, Lq, H * D)


def _nat_latent(q, k, v, kc, vc, rpb):
    B, L, H, D = q.shape
    rows = L // GRID_W
    wh = min(NAT_WH, rows)
    qg = q.reshape(B, rows, GRID_W, H, D)
    kg = k.reshape(B, rows, GRID_W, H, D)
    vg = v.reshape(B, rows, GRID_W, H, D)
    colv = np.arange(GRID_W)
    cstart = np.clip(colv - NAT_WW // 2, 0, GRID_W - NAT_WW)
    col_mask = (colv[None, :] >= cstart[:, None]) & (colv[None, :] < cstart[:, None] + NAT_WW)
    col_idx = np.clip(colv[None, :] - colv[:, None] + NAT_WW - 1, 0, 2 * NAT_WW - 2)
    rpb_c = rpb.astype(f32)[:, :, col_idx]
    scale = D ** -0.5
    nkey = wh * GRID_W

    def row(r):
        rs = jnp.clip(r - wh // 2, 0, rows - wh)
        kr = lax.dynamic_slice_in_dim(kg, rs, wh, axis=1)
        vr = lax.dynamic_slice_in_dim(vg, rs, wh, axis=1)
        qr = lax.dynamic_index_in_dim(qg, r, axis=1, keepdims=False)
        row_idx = rs + jnp.arange(wh) - r + (NAT_WH - 1)
        bias = jnp.take(rpb_c, row_idx, axis=1).transpose(0, 2, 1, 3)
        s = jnp.einsum('bqhd,bwkhd->bhqwk', qr, kr) * scale + bias[None]
        s = jnp.where(col_mask[None, None, :, None, :], s, -jnp.inf).reshape(B, H, GRID_W, nkey)
        sc = jnp.einsum('bqhd,bkhd->bhqk', qr, kc) * scale
        p = jax.nn.softmax(jnp.concatenate([s, sc], axis=-1), axis=-1)
        return (jnp.einsum('bhqk,bkhd->bqhd', p[..., :nkey], vr.reshape(B, nkey, H, D))
                + jnp.einsum('bhqk,bkhd->bqhd', p[..., nkey:], vc))

    o = lax.map(row, jnp.arange(rows))
    return jnp.moveaxis(o, 0, 1).reshape(B, L, H * D)


def kernel(x_prompt, x_sample, cache_diff_k, cache_diff_v, cache_nat_k, cache_nat_v, state_gla, c, c_ctx, w_ada, b_ada, norm1, w_in, pool_w, pool_scale, gla_w_up, gla_b_up, gla_norm, diff_lambda, diff_norm, nat_rpb, w_out, norm2, router_group_w, router_group_b, router_expert_w, router_expert_b, expert_w_gate, expert_w_up, expert_w_down, norm_final):
    Bp, Lp, Bs, Ls = BATCH, SEQ, DEC_BATCH, DEC_SEQ
    x = jnp.concatenate([x_prompt.reshape(N_PROMPT, D_MODEL), x_sample.reshape(N_SAMPLE, D_MODEL)], axis=0)
    cvec = jnp.zeros((MOD_ROWS, D_MODEL), f32).at[0].set(c_ctx).at[1:1 + DEC_BATCH].set(c)
    mod_all = ada_modulation(cvec, w_ada, b_ada).reshape(DEPTH, MOD_ROWS, 6, D_MODEL)

    new_dk, new_dv, new_nk, new_nv, new_gs = [], [], [], [], []
    for l in range(DEPTH):
        mod = mod_all[l]
        lam_init = 0.8 - 0.6 * math.exp(-0.3 * l)
        lv = diff_lambda[l].astype(f32)
        lam = jnp.exp(jnp.sum(lv[0] * lv[1])) - jnp.exp(jnp.sum(lv[2] * lv[3])) + lam_init

        w_l = w_in[l]
        w_perm = jnp.concatenate(
            [w_l[:, :ORIG_GD], w_l[:, ORIG_GD + 2 * GLA_RANK:], w_l[:, ORIG_GD:ORIG_GD + 2 * GLA_RANK],
             jnp.zeros((D_MODEL, D_IN_PAD - C_GD - 2 * GLA_RANK), f32)], axis=1).astype(bf16)
        u = input_projection(x, norm1[l], mod, w_perm)

        u_seq = u.reshape(N_TOK // Lp, Lp, D_IN_PAD)
        up = u[:N_PROMPT]

        zero = jnp.zeros((Bp, GLA_HEADS, GLA_DK, GLA_DV), f32)
        o_gla_p, gs = gla_mixer(u, 0, Bp, Lp, Lp, gla_w_up[l], gla_b_up[l], gla_norm[l], zero, zero)
        q2_p, kb_p = diff_prep(u, 0, N_PROMPT, rope=False)
        o_diff_p = flash_attention(q2_p, kb_p.reshape(Bp, Lp, GROUP_W), u_seq, n_batch=Bp, lq=Lp, tq=Lp, tk=Lp,
                                   qcol=0, kcol=0, vcol=C_DV // LANES, lam=lam, g=diff_norm[l],
                                   out_scale=1.0 - lam_init)
        o_nat_p = flash_attention(u.reshape(1, N_TOK, D_IN_PAD), u_seq, u_seq, n_batch=Bp, lq=Lp, tq=Lp, tk=Lp,
                                  qcol=C_NQ // LANES, kcol=C_NK // LANES, vcol=C_NV // LANES,
                                  scale=NAT_HD ** -0.5)
        o_pool_p = pool_mixer(u, 0, Bp, Lp, pool_w[l], pool_scale[l])
        new_dk.append(up[:, C_DK:C_DK + GROUP_W].reshape(Bp, Lp, DIFF_HEADS, 2 * DIFF_HD))
        new_dv.append(up[:, C_DV:C_DV + GROUP_W].reshape(Bp, Lp, DIFF_HEADS, DIFF_VD))
        new_nk.append(up[:, C_NK:C_NK + GROUP_W].reshape(Bp, Lp, NAT_HEADS, NAT_HD))
        new_nv.append(up[:, C_NV:C_NV + GROUP_W].reshape(Bp, Lp, NAT_HEADS, NAT_HD))
        new_gs.append(gs)

        st = state_gla[:, l].astype(f32)
        o_gla_s, _ = gla_mixer(u, N_PROMPT, Bs, Ls, 512, gla_w_up[l], gla_b_up[l], gla_norm[l],
                               st[:, 0], st[:, 1])
        q2_s, kb_s = diff_prep(u, N_PROMPT, N_SAMPLE, rope=True)
        k_all = jnp.concatenate([kb_s.reshape(Bs, Ls, GROUP_W),
                                 cache_diff_k[:, l].reshape(Bs, PAST_LEN, GROUP_W).astype(bf16)], axis=1)
        v_all = jnp.concatenate([u[N_PROMPT:, C_DV:C_DV + GROUP_W].astype(bf16).reshape(Bs, Ls, GROUP_W),
                                 cache_diff_v[:, l].reshape(Bs, PAST_LEN, GROUP_W).astype(bf16)], axis=1)
        o_diff_s = flash_attention(q2_s, k_all, jnp.swapaxes(v_all, 1, 2), n_batch=Bs, lq=Ls, tq=2048, tk=1536,
                                   qcol=0, kcol=0, vcol=0, v_transposed=True,
                                   lam=lam, g=diff_norm[l], out_scale=1.0 - lam_init)
        o_nat_s = nat_attention(u, N_PROMPT, cache_nat_k[:, l].reshape(Bs, PAST_LEN, GROUP_W),
                                cache_nat_v[:, l].reshape(Bs, PAST_LEN, GROUP_W), nat_rpb[l])
        o_pool_s = pool_mixer(u, N_PROMPT, Bs, Ls, pool_w[l], pool_scale[l])

        def both(a, b):
            return jnp.concatenate([a, b], axis=0)

        parts = [both(o_pool_p, o_pool_s), both(o_gla_p, o_gla_s), both(o_diff_p, o_diff_s),
                 both(o_nat_p, o_nat_s)]
        x = output_projection(parts, w_out[l].astype(bf16), x, mod)

        wr = jnp.zeros((D_MODEL, LANES), f32)
        wr = wr.at[:, :MOE_GROUPS].set(router_group_w[l]).at[:, MOE_GROUPS:MOE_GROUPS + MOE_EXPERTS].set(
            router_expert_w[l])
        br = jnp.zeros((1, LANES), f32)
        br = br.at[0, :MOE_GROUPS].set(router_group_b[l]).at[0, MOE_GROUPS:MOE_GROUPS + MOE_EXPERTS].set(
            router_expert_b[l])
        h2, route, counts = moe_router(x, norm2[l], mod, wr, br)
        pos, slot_token, tile_expert, n_used = route_layout(route, counts)
        ys = expert_ffn(l, tile_expert, n_used, slot_token, h2, expert_w_gate, expert_w_up, expert_w_down)
        y0 = ys.at[pos[:, 0]].get(mode="promise_in_bounds")
        y1 = ys.at[pos[:, 1]].get(mode="promise_in_bounds")
        x = moe_combine(x, y0, y1, route, mod, norm_final if l == DEPTH - 1 else None)

    y_prompt = x[:N_PROMPT].reshape(Bp, Lp, D_MODEL)
    y_sample = x[N_PROMPT:].reshape(Bs, Ls, D_MODEL)
    return (y_prompt, y_sample, jnp.stack(new_dk, axis=1), jnp.stack(new_dv, axis=1),
            jnp.stack(new_nk, axis=1), jnp.stack(new_nv, axis=1), jnp.stack(new_gs, axis=1))
```

```python
import functools
import math

import jax
import jax.numpy as jnp
import numpy as np
from jax import lax
from jax.experimental import pallas as pl
from jax.experimental.pallas import tpu as pltpu

f32 = jnp.float32
bf16 = jnp.bfloat16

D_MODEL = 4096
BATCH = 32
SEQ = 256
DEPTH = 2
DEC_BATCH = 4
DEC_SEQ = 4096
PAST_LEN = 512
GRID_W = 64
GROUP_W = D_MODEL // 4
POOL_WINDOWS = (2, 4, 8, 16)
POOL_GW = GROUP_W // 4
GLA_HEADS = 8
GLA_DV = GROUP_W // GLA_HEADS
GLA_DK = GLA_DV // 2
GLA_RANK = 16
GLA_GATE_NORM = 16.0
GLA_CHUNK = 64
DIFF_HEADS = 8
DIFF_VD = GROUP_W // DIFF_HEADS
DIFF_HD = DIFF_VD // 2
ROPE_BASE = 10000.0
NAT_HEADS = 8
NAT_HD = GROUP_W // NAT_HEADS
NAT_WH = 8
NAT_WW = 16
MOE_GROUPS = 4
MOE_PER_GROUP = 8
MOE_EXPERTS = MOE_GROUPS * MOE_PER_GROUP
MOE_TOPK = 2
MOE_FF = 512
Q_BLOCK = 128
EPS = 1e-6

N_PROMPT = BATCH * SEQ
N_SAMPLE = DEC_BATCH * DEC_SEQ
N_TOK = N_PROMPT + N_SAMPLE
LANES = 128
MOD_ROWS = 8

C_POOL, C_GQ, C_GK, C_GV, C_GG = 0, 1024, 1536, 2048, 3072
C_DQ, C_DK, C_DV, C_NQ, C_NK, C_NV, C_GD = 4096, 5120, 6144, 7168, 8192, 9216, 10240
D_IN_PAD = 10752
ORIG_GD = 3 * GROUP_W + 2 * GLA_HEADS * GLA_DK

TM = 512
TN_IN = 768
TN_OUT = 1024
TM_E = 256
N_ASSIGN = N_TOK * MOE_TOPK
N_ETILES = N_ASSIGN // TM_E + MOE_EXPERTS
VMEM_LIMIT = 56 * 1024 * 1024


def _cparams(sem):
    return pltpu.CompilerParams(dimension_semantics=sem, vmem_limit_bytes=VMEM_LIMIT)


def _mod_row(i):
    npt = N_PROMPT // TM
    return jnp.where(i < npt, 0, 1 + (i - npt) // (DEC_SEQ // TM))


def _ada_kernel(c_ref, w_ref, b_ref, o_ref):
    @pl.when(pl.program_id(2) == 0)
    def _():
        o_ref[0] = jnp.broadcast_to(b_ref[0], o_ref.shape[1:])

    c = c_ref[...]
    a = (c * jax.nn.sigmoid(c)).astype(bf16)
    o_ref[0] += jnp.dot(a, w_ref[0].astype(bf16), preferred_element_type=f32)


def ada_modulation(cvec, w_ada, b_ada):
    tk, tn = 2048, 1024
    n6 = 6 * D_MODEL
    return pl.pallas_call(
        _ada_kernel,
        grid=(DEPTH, n6 // tn, D_MODEL // tk),
        in_specs=[pl.BlockSpec((MOD_ROWS, tk), lambda l, j, k: (0, k)),
                  pl.BlockSpec((1, tk, tn), lambda l, j, k: (l, k, j)),
                  pl.BlockSpec((1, 1, tn), lambda l, j, k: (l, 0, j))],
        out_specs=pl.BlockSpec((1, MOD_ROWS, tn), lambda l, j, k: (l, 0, j)),
        out_shape=jax.ShapeDtypeStruct((DEPTH, MOD_ROWS, n6), f32),
        compiler_params=_cparams(("parallel", "parallel", "arbitrary")),
        name="ada_modulation",
    )(cvec, w_ada, b_ada.reshape(DEPTH, 1, n6))


def _modulated_norm(x, g, shift, scale):
    y = x * lax.rsqrt(jnp.mean(x * x, axis=-1, keepdims=True) + EPS) * g
    return y * (1.0 + scale) + shift


def _win_kernel(x_ref, g_ref, mod_ref, w_ref, o_ref, h_ref):
    @pl.when(pl.program_id(1) == 0)
    def _():
        h = _modulated_norm(x_ref[...], g_ref[...], mod_ref[0, 0:1, :], mod_ref[0, 1:2, :])
        h_ref[...] = h.astype(bf16)

    o_ref[...] = jnp.dot(h_ref[...], w_ref[...], preferred_element_type=f32)


def input_projection(x, g_norm, mod, w):
    return pl.pallas_call(
        _win_kernel,
        grid=(N_TOK // TM, D_IN_PAD // TN_IN),
        in_specs=[pl.BlockSpec((TM, D_MODEL), lambda i, j: (i, 0)),
                  pl.BlockSpec((1, D_MODEL), lambda i, j: (0, 0)),
                  pl.BlockSpec((1, 6, D_MODEL), lambda i, j: (_mod_row(i), 0, 0)),
                  pl.BlockSpec((D_MODEL, TN_IN), lambda i, j: (0, j))],
        out_specs=pl.BlockSpec((TM, TN_IN), lambda i, j: (i, j)),
        out_shape=jax.ShapeDtypeStruct((N_TOK, D_IN_PAD), f32),
        scratch_shapes=[pltpu.VMEM((TM, D_MODEL), bf16)],
        compiler_params=_cparams(("parallel", "arbitrary")),
        name="input_projection",
    )(x, g_norm.reshape(1, D_MODEL), mod, w)


N_MIX = 4


def _wout_kernel(*refs):
    prompt_parts, sample_parts = refs[:N_MIX], refs[N_MIX:2 * N_MIX]
    w_refs = refs[2 * N_MIX:3 * N_MIX]
    x_ref, mod_ref, o_ref = refs[3 * N_MIX:]
    is_prompt = pl.program_id(0) < N_PROMPT // TM

    def project(parts):
        acc = jnp.dot(parts[0][...], w_refs[0][...], preferred_element_type=f32)
        for a, w in zip(parts[1:], w_refs[1:]):
            acc += jnp.dot(a[...], w[...], preferred_element_type=f32)
        o_ref[...] = x_ref[...] + mod_ref[0, 2:3, :] * acc

    @pl.when(is_prompt)
    def _():
        project(prompt_parts)

    @pl.when(jnp.logical_not(is_prompt))
    def _():
        project(sample_parts)


def output_projection(prompt_parts, sample_parts, w, x, mod):
    npt = N_PROMPT // TM
    p_specs = [pl.BlockSpec((TM, GROUP_W), lambda i, j: (jnp.minimum(i, npt - 1), 0)) for _ in range(N_MIX)]
    s_specs = [pl.BlockSpec((TM, GROUP_W), lambda i, j: (jnp.maximum(i - npt, 0), 0)) for _ in range(N_MIX)]
    w_specs = [pl.BlockSpec((GROUP_W, TN_OUT), functools.partial(lambda i, j, c: (c, j), c=c))
               for c in range(N_MIX)]
    return pl.pallas_call(
        _wout_kernel,
        grid=(N_TOK // TM, D_MODEL // TN_OUT),
        in_specs=p_specs + s_specs + w_specs + [
            pl.BlockSpec((TM, TN_OUT), lambda i, j: (i, j)),
            pl.BlockSpec((1, 6, TN_OUT), lambda i, j: (_mod_row(i), 0, j))],
        out_specs=pl.BlockSpec((TM, TN_OUT), lambda i, j: (i, j)),
        out_shape=jax.ShapeDtypeStruct((N_TOK, D_MODEL), f32),
        compiler_params=_cparams(("parallel", "arbitrary")),
        name="output_projection",
    )(*prompt_parts, *sample_parts, *([w] * N_MIX), x, mod)


def _split_bf16(a):
    hi = a.astype(bf16)
    lo = (a - hi.astype(f32)).astype(bf16)
    return hi, lo


def _router_kernel(x_ref, g_ref, mod_ref, wr_ref, br_ref, h_ref, r_ref, cnt_ref):
    h = _modulated_norm(x_ref[...], g_ref[...], mod_ref[0, 3:4, :], mod_ref[0, 4:5, :])
    h_ref[...] = h
    h_hi, h_lo = _split_bf16(h)
    w_hi, w_lo = _split_bf16(wr_ref[...])
    lg = (jnp.dot(h_hi, w_hi, preferred_element_type=f32)
          + jnp.dot(h_lo, w_hi, preferred_element_type=f32)
          + jnp.dot(h_hi, w_lo, preferred_element_type=f32)) + br_ref[...]
    lane = lax.broadcasted_iota(jnp.int32, lg.shape, 1).astype(f32)
    ninf = -jnp.inf

    def first_max(v):
        m = jnp.max(v, axis=-1, keepdims=True)
        idx = jnp.min(jnp.where(v == m, lane, float(LANES)), axis=-1, keepdims=True)
        return m, idx

    gmask = lane < MOE_GROUPS
    mg, gi = first_max(jnp.where(gmask, lg, ninf))
    p_top = 1.0 / jnp.sum(jnp.where(gmask, jnp.exp(lg - mg), 0.0), axis=-1, keepdims=True)
    lo = MOE_GROUPS + gi * MOE_PER_GROUP
    le = jnp.where((lane >= lo) & (lane < lo + MOE_PER_GROUP), lg, ninf)
    v1, i1 = first_max(le)
    v2, i2 = first_max(jnp.where(lane == i1, ninf, le))
    t = jnp.exp(v2 - v1)
    w1 = p_top / (1.0 + t)
    w2 = w1 * t

    @pl.when(pl.program_id(0) == 0)
    def _():
        cnt_ref[...] = jnp.zeros(cnt_ref.shape, f32)

    tm = lg.shape[0]
    chosen = jnp.where((lane == i1) | (lane == i2), 1.0, 0.0)
    earlier = (lax.broadcasted_iota(jnp.int32, (tm, tm), 1) < lax.broadcasted_iota(jnp.int32, (tm, tm), 0))
    before = jnp.dot(jnp.where(earlier, 1.0, 0.0).astype(bf16), chosen.astype(bf16),
                     preferred_element_type=f32) + cnt_ref[0:1, :]
    rank1 = jnp.sum(jnp.where(lane == i1, before, 0.0), axis=-1, keepdims=True)
    rank2 = jnp.sum(jnp.where(lane == i2, before, 0.0), axis=-1, keepdims=True)
    cnt_ref[...] = cnt_ref[...] + jnp.sum(chosen, axis=0, keepdims=True)

    slab = jnp.zeros(lg.shape, f32)
    for pos, val in enumerate((i1 - MOE_GROUPS, i2 - MOE_GROUPS, w1, w2, rank1, rank2)):
        slab = jnp.where(lane == float(pos), val, slab)
    r_ref[...] = slab


def moe_router(x, g_norm, mod, wr, br):
    tm = 256
    return pl.pallas_call(
        _router_kernel,
        grid=(N_TOK // tm,),
        in_specs=[pl.BlockSpec((tm, D_MODEL), lambda i: (i, 0)),
                  pl.BlockSpec((1, D_MODEL), lambda i: (0, 0)),
                  pl.BlockSpec((1, 6, D_MODEL), lambda i: (_mod_row(i // (TM // tm)), 0, 0)),
                  pl.BlockSpec((D_MODEL, LANES), lambda i: (0, 0)),
                  pl.BlockSpec((1, LANES), lambda i: (0, 0))],
        out_specs=[pl.BlockSpec((tm, D_MODEL), lambda i: (i, 0)),
                   pl.BlockSpec((tm, LANES), lambda i: (i, 0)),
                   pl.BlockSpec((MOD_ROWS, LANES), lambda i: (0, 0))],
        out_shape=[jax.ShapeDtypeStruct((N_TOK, D_MODEL), f32),
                   jax.ShapeDtypeStruct((N_TOK, LANES), f32),
                   jax.ShapeDtypeStruct((MOD_ROWS, LANES), f32)],
        compiler_params=_cparams(("arbitrary",)),
        name="moe_router",
    )(x, g_norm.reshape(1, D_MODEL), mod, wr, br)


def _ffn_kernel(te_ref, nu_ref, tok_ref, h_hbm, wg_ref, wu_ref, wd_ref, o_ref, xbuf, sem, wg_b, wu_b, wd_b):
    t = pl.program_id(0)
    n_used = nu_ref[0]

    def row_copy(tile, r):
        half = tile % 2
        return pltpu.make_async_copy(h_hbm.at[pl.ds(tok_ref[tile * TM_E + r], 1)],
                                     xbuf.at[half, pl.ds(r, 1)], sem.at[half])

    def for_rows(tile, fn):
        def body(r, carry):
            fn(row_copy(tile, r))
            return carry
        lax.fori_loop(0, TM_E, body, 0, unroll=8)

    @pl.when(t == 0)
    def _():
        for_rows(t, lambda cp: cp.start())

    @pl.when(t + 1 < n_used)
    def _():
        for_rows(t + 1, lambda cp: cp.start())

    @pl.when(t < n_used)
    def _():
        @pl.when((t == 0) | (te_ref[t] != te_ref[jnp.maximum(t - 1, 0)]))
        def _():
            wg_b[...] = wg_ref[0, 0].astype(bf16)
            wu_b[...] = wu_ref[0, 0].astype(bf16)
            wd_b[...] = wd_ref[0, 0].astype(bf16)

        for_rows(t, lambda cp: cp.wait())
        x = xbuf[t % 2].astype(bf16)
        a = jnp.dot(x, wg_b[...], preferred_element_type=f32)
        u = jnp.dot(x, wu_b[...], preferred_element_type=f32)
        mid = (a * jax.nn.sigmoid(a) * u).astype(bf16)
        o_ref[...] = jnp.dot(mid, wd_b[...], preferred_element_type=f32).astype(o_ref.dtype)

    @pl.when(t >= nu_ref[0])
    def _():
        o_ref[...] = jnp.zeros(o_ref.shape, o_ref.dtype)


def expert_ffn(layer, tile_expert, n_used, slot_token, h, wg, wu, wd):
    def w_spec(r, c):
        return pl.BlockSpec((1, 1, r, c), lambda t, te, nu, tok: (layer, te[t], 0, 0),
                            pipeline_mode=pl.Buffered(1))

    return pl.pallas_call(
        _ffn_kernel,
        grid_spec=pltpu.PrefetchScalarGridSpec(
            num_scalar_prefetch=3,
            grid=(N_ETILES,),
            in_specs=[pl.BlockSpec(memory_space=pl.ANY),
                      w_spec(D_MODEL, MOE_FF), w_spec(D_MODEL, MOE_FF), w_spec(MOE_FF, D_MODEL)],
            out_specs=pl.BlockSpec((TM_E, D_MODEL), lambda t, te, nu, tok: (t, 0)),
            scratch_shapes=[pltpu.VMEM((2, TM_E, D_MODEL), f32), pltpu.SemaphoreType.DMA((2,)),
                            pltpu.VMEM((D_MODEL, MOE_FF), bf16), pltpu.VMEM((D_MODEL, MOE_FF), bf16),
                            pltpu.VMEM((MOE_FF, D_MODEL), bf16)]),
        out_shape=jax.ShapeDtypeStruct((N_ETILES * TM_E, D_MODEL), bf16),
        compiler_params=_cparams(("arbitrary",)),
        name="expert_ffn",
    )(tile_expert, n_used, slot_token, h, wg, wu, wd)


def route_layout(route, counts):
    e = route[:, :MOE_TOPK].astype(jnp.int32)
    rank = route[:, 4:4 + MOE_TOPK].astype(jnp.int32)
    cnt = counts[0, MOE_GROUPS:MOE_GROUPS + MOE_EXPERTS].astype(jnp.int32)
    tiles_e = (cnt + TM_E - 1) // TM_E
    tile_end = jnp.cumsum(tiles_e)
    tile_start = tile_end - tiles_e
    pos = tile_start[e] * TM_E + rank
    slot_token = jnp.zeros((N_ETILES * TM_E,), jnp.int32).at[pos.reshape(-1)].set(
        jnp.arange(N_ASSIGN, dtype=jnp.int32) // MOE_TOPK)
    n_used = tile_end[-1]
    t = jnp.minimum(jnp.arange(N_ETILES, dtype=jnp.int32), n_used - 1)
    tile_expert = jnp.sum((t[:, None] >= tile_end[None, :]).astype(jnp.int32), axis=1)
    tile_expert = jnp.minimum(tile_expert, MOE_EXPERTS - 1)
    return pos, slot_token, tile_expert, n_used.reshape(1).astype(jnp.int32)


def _combine_kernel(x_ref, y0_ref, y1_ref, r_ref, mod_ref, o_ref):
    y = r_ref[:, 2:3] * y0_ref[...] + r_ref[:, 3:4] * y1_ref[...]
    o_ref[...] = x_ref[...] + mod_ref[0, 5:6, :] * y


def _combine_norm_kernel(x_ref, y0_ref, y1_ref, r_ref, mod_ref, g_ref, o_ref):
    y = r_ref[:, 2:3] * y0_ref[...] + r_ref[:, 3:4] * y1_ref[...]
    x = x_ref[...] + mod_ref[0, 5:6, :] * y
    o_ref[...] = x * lax.rsqrt(jnp.mean(x * x, axis=-1, keepdims=True) + EPS) * g_ref[...]


def moe_combine(x, y0, y1, route, mod, g_final=None):
    tm = 256
    row = pl.BlockSpec((tm, D_MODEL), lambda i: (i, 0))
    in_specs = [row, row, row, pl.BlockSpec((tm, LANES), lambda i: (i, 0)),
                pl.BlockSpec((1, 6, D_MODEL), lambda i: (_mod_row(i // (TM // tm)), 0, 0))]
    args = [x, y0, y1, route, mod]
    body = _combine_kernel
    if g_final is not None:
        in_specs.append(pl.BlockSpec((1, D_MODEL), lambda i: (0, 0)))
        args.append(g_final.reshape(1, D_MODEL))
        body = _combine_norm_kernel
    return pl.pallas_call(
        body,
        grid=(N_TOK // tm,),
        in_specs=in_specs,
        out_specs=row,
        out_shape=jax.ShapeDtypeStruct((N_TOK, D_MODEL), f32),
        compiler_params=_cparams(("parallel",)),
        name="moe_combine",
    )(*args)


def _diff_prep_kernel(rope, q_ref, k_ref, cos_ref, sin_ref, q2_ref, kb_ref):
    lane = lax.broadcasted_iota(jnp.int32, q_ref.shape, 1)

    def rot(x):
        if not rope:
            return x
        partner = jnp.where((lane & 31) >= 16, pltpu.roll(x, 16, 1), pltpu.roll(x, LANES - 16, 1))
        return x * cos_ref[...] + partner * sin_ref[...]

    q = rot(q_ref[...]) * DIFF_HD ** -0.5
    first = lane < DIFF_HD
    q2_ref[0] = jnp.where(first, q, 0.0).astype(bf16)
    q2_ref[1] = jnp.where(first, 0.0, q).astype(bf16)
    kb_ref[...] = rot(k_ref[...]).astype(bf16)


def _rope_tables():
    t = jnp.arange(DEC_SEQ)
    row = (t // GRID_W).astype(f32)
    col = (t % GRID_W).astype(f32)
    nf = DIFF_HD // 4
    inv = ROPE_BASE ** (-jnp.arange(nf, dtype=f32) / nf)
    lane = np.arange(LANES)
    pos = jnp.where(((lane // 32) % 2 == 0)[None, :], row[:, None], col[:, None])
    ang = pos * inv[lane % nf][None, :]
    sign = np.where(lane % 32 >= nf, 1.0, -1.0).astype(np.float32)
    return jnp.cos(ang), jnp.sin(ang) * sign[None, :]


def diff_prep(u, row0, n_rows, rope):
    t = 1024
    r0 = row0 // t
    nseq = DEC_SEQ // t
    cos, sin = _rope_tables() if rope else (jnp.zeros((t, LANES), f32), jnp.zeros((t, LANES), f32))
    tab = pl.BlockSpec((t, LANES), (lambda i, h: (i % nseq, 0)) if rope else (lambda i, h: (0, 0)))
    return pl.pallas_call(
        functools.partial(_diff_prep_kernel, rope),
        grid=(n_rows // t, DIFF_HEADS),
        in_specs=[pl.BlockSpec((t, LANES), lambda i, h: (r0 + i, C_DQ // LANES + h)),
                  pl.BlockSpec((t, LANES), lambda i, h: (r0 + i, C_DK // LANES + h)),
                  tab, tab],
        out_specs=[pl.BlockSpec((2, t, LANES), lambda i, h: (0, i, h)),
                   pl.BlockSpec((t, LANES), lambda i, h: (i, h))],
        out_shape=[jax.ShapeDtypeStruct((2, n_rows, GROUP_W), bf16),
                   jax.ShapeDtypeStruct((n_rows, GROUP_W), bf16)],
        compiler_params=_cparams(("parallel", "parallel")),
        name="diff_prep",
    )(u, u, cos, sin)


def _flash_kernel(n_stack, tq, sub, scale, out_scale, v_transposed, q_ref, k_ref, v_ref, lam_ref, g_ref, o_ref,
                  m_ref, l_ref, acc_ref):
    kj = pl.program_id(3)

    @pl.when(kj == 0)
    def _():
        m_ref[...] = jnp.full(m_ref.shape, -jnp.inf, f32)
        l_ref[...] = jnp.zeros(l_ref.shape, f32)
        acc_ref[...] = jnp.zeros(acc_ref.shape, f32)

    q = q_ref[...].reshape(n_stack * tq, LANES).astype(bf16)
    tk = k_ref.shape[1]
    m, l, acc = m_ref[...], l_ref[...], acc_ref[...]
    for c in range(tk // sub):
        keys = slice(c * sub, (c + 1) * sub)
        s = lax.dot_general(k_ref[0, keys, :].astype(bf16), q, (((1,), (1,)), ((), ())),
                            preferred_element_type=f32)
        if scale != 1.0:
            s = s * scale
        m_new = jnp.maximum(m, jnp.max(s, axis=0, keepdims=True))
        alpha = jnp.exp(m - m_new)
        p = jnp.exp(s - m_new)
        l = alpha * l + jnp.sum(p, axis=0, keepdims=True)
        vt = v_ref[0, :, keys].astype(bf16) if v_transposed else v_ref[0, keys, :].T.astype(bf16)
        acc = alpha * acc + jnp.dot(vt, p.astype(bf16), preferred_element_type=f32)
        m = m_new
    m_ref[...], l_ref[...], acc_ref[...] = m, l, acc

    @pl.when(kj == pl.num_programs(3) - 1)
    def _():
        o = acc_ref[...] / l_ref[...]
        if n_stack == 2:
            o = o[:, :tq] - lam_ref[0:1, 0:1] * o[:, tq:]
            o = o * lax.rsqrt(jnp.mean(o * o, axis=0, keepdims=True) + EPS)
            o = o.T * (g_ref[...] * out_scale)
        else:
            o = o.T
        o_ref[...] = o.astype(o_ref.dtype)


def flash_attention(q, k, v, *, n_batch, lq, tq, tk, qcol, kcol, vcol, sub=512, v_transposed=False, qrow0=0, kb0=0,
                    scale=1.0, lam=None, g=None, out_scale=1.0):
    n_stack = q.shape[0]
    lk = k.shape[1]
    nq = lq // tq
    lam = jnp.zeros((1, LANES), f32) if lam is None else jnp.full((1, LANES), lam, f32)
    g = jnp.ones((1, LANES), f32) if g is None else g.reshape(1, LANES).astype(f32)
    rows = n_stack * tq
    if v_transposed:
        v_spec = pl.BlockSpec((1, LANES, tk), lambda b, h, i, j: (kb0 + b, vcol + h, j))
    else:
        v_spec = pl.BlockSpec((1, tk, LANES), lambda b, h, i, j: (kb0 + b, j, vcol + h))
    return pl.pallas_call(
        functools.partial(_flash_kernel, n_stack, tq, min(sub, tk), scale, out_scale, v_transposed),
        grid=(n_batch, DIFF_HEADS, nq, lk // tk),
        in_specs=[pl.BlockSpec((n_stack, tq, LANES), lambda b, h, i, j: (0, qrow0 + b * nq + i, qcol + h)),
                  pl.BlockSpec((1, tk, LANES), lambda b, h, i, j: (kb0 + b, j, kcol + h)),
                  v_spec,
                  pl.BlockSpec((1, LANES), lambda b, h, i, j: (0, 0)),
                  pl.BlockSpec((1, LANES), lambda b, h, i, j: (0, 0))],
        out_specs=pl.BlockSpec((tq, LANES), lambda b, h, i, j: (b * nq + i, h)),
        out_shape=jax.ShapeDtypeStruct((n_batch * lq, GROUP_W), bf16),
        scratch_shapes=[pltpu.VMEM((1, rows), f32), pltpu.VMEM((1, rows), f32),
                        pltpu.VMEM((LANES, rows), f32)],
        compiler_params=_cparams(("parallel", "parallel", "parallel", "arbitrary")),
        name="flash_attention",
    )(q, k, v, lam, g)


NAT_ROWS = DEC_SEQ // GRID_W
NAT_RB = 8
NAT_BAND = NAT_WH * GRID_W
NEG_BIG = -1e30


def _nat_kernel(q_ref, k_ref, v_ref, kc_ref, vc_ref, bias_ref, o_ref):
    rb = pl.program_id(2)
    scale = NAT_HD ** -0.5
    kc = kc_ref[0].astype(bf16)
    vc = vc_ref[0].astype(bf16)
    nt = (((1,), (1,)), ((), ()))
    for i in range(NAT_RB):
        r = rb * NAT_RB + i
        rs = jnp.clip(r - NAT_WH // 2, 0, NAT_ROWS - NAT_WH)
        k0 = pl.multiple_of(rs * GRID_W, GRID_W)
        kband = k_ref[pl.ds(k0, NAT_BAND), :].astype(bf16)
        vband = v_ref[pl.ds(k0, NAT_BAND), :].astype(bf16)
        q = q_ref[i * GRID_W:(i + 1) * GRID_W, :].astype(bf16)
        sb = lax.dot_general(q, kband, nt, preferred_element_type=f32) * scale + bias_ref[rs - r + NAT_WH - 1, 0]
        sc = lax.dot_general(q, kc, nt, preferred_element_type=f32) * scale
        m = jnp.maximum(jnp.max(sb, axis=-1, keepdims=True), jnp.max(sc, axis=-1, keepdims=True))
        pb = jnp.exp(sb - m)
        pc = jnp.exp(sc - m)
        l = jnp.sum(pb, axis=-1, keepdims=True) + jnp.sum(pc, axis=-1, keepdims=True)
        o = (jnp.dot(pb.astype(bf16), vband, preferred_element_type=f32)
             + jnp.dot(pc.astype(bf16), vc, preferred_element_type=f32)) / l
        o_ref[i * GRID_W:(i + 1) * GRID_W, :] = o.astype(o_ref.dtype)


def _nat_bias_table(rpb):
    colv = np.arange(GRID_W)
    cstart = np.clip(colv - NAT_WW // 2, 0, GRID_W - NAT_WW)
    col_mask = (colv[None, :] >= cstart[:, None]) & (colv[None, :] < cstart[:, None] + NAT_WW)
    col_idx = np.clip(colv[None, :] - colv[:, None] + NAT_WW - 1, 0, 2 * NAT_WW - 2)
    rpb_c = rpb.astype(f32)[:, :, col_idx]
    row_idx = np.arange(NAT_WH)[:, None] + np.arange(NAT_WH)[None, :]
    tab = rpb_c[:, row_idx]
    tab = jnp.where(col_mask[None, None, None], tab, NEG_BIG)
    return tab.transpose(1, 0, 3, 2, 4).reshape(NAT_WH, NAT_HEADS, GRID_W, NAT_BAND)


def nat_attention(u, row0, kc, vc, rpb):
    qblk = NAT_RB * GRID_W
    q0 = row0 // qblk
    b0 = row0 // DEC_SEQ
    return pl.pallas_call(
        _nat_kernel,
        grid=(DEC_BATCH, NAT_HEADS, NAT_ROWS // NAT_RB),
        in_specs=[pl.BlockSpec((qblk, LANES), lambda b, h, r: (q0 + b * (NAT_ROWS // NAT_RB) + r, C_NQ // LANES + h)),
                  pl.BlockSpec((DEC_SEQ, LANES), lambda b, h, r: (b0 + b, C_NK // LANES + h)),
                  pl.BlockSpec((DEC_SEQ, LANES), lambda b, h, r: (b0 + b, C_NV // LANES + h)),
                  pl.BlockSpec((1, PAST_LEN, LANES), lambda b, h, r: (b, 0, h)),
                  pl.BlockSpec((1, PAST_LEN, LANES), lambda b, h, r: (b, 0, h)),
                  pl.BlockSpec((NAT_WH, 1, GRID_W, NAT_BAND), lambda b, h, r: (0, h, 0, 0))],
        out_specs=pl.BlockSpec((qblk, LANES), lambda b, h, r: (b * (NAT_ROWS // NAT_RB) + r, h)),
        out_shape=jax.ShapeDtypeStruct((N_SAMPLE, GROUP_W), bf16),
        compiler_params=_cparams(("parallel", "parallel", "arbitrary")),
        name="nat_attention",
    )(u, u, u, kc, vc, _nat_bias_table(rpb))


def _pool_kernel(seq, u_ref, w_ref, s_ref, o_ref):
    grp = pl.program_id(1)
    for gi, win in enumerate(POOL_WINDOWS):
        @pl.when(grp == gi)
        def _(win=win):
            u = u_ref[...]
            t = lax.broadcasted_iota(jnp.int32, u.shape, 0)
            acc = jnp.zeros_like(u)
            for d in range(-(win // 2), win // 2):
                shifted = u if d == 0 else pltpu.roll(u, (-d) % seq, 0)
                acc += jnp.where((t + d >= 0) & (t + d < seq), shifted, 0.0)
            cnt = (jnp.minimum(t + win // 2, seq) - jnp.maximum(t - win // 2, 0)).astype(f32)
            p = acc / cnt - u
            y = jnp.dot(p.astype(bf16), w_ref[0].astype(bf16), preferred_element_type=f32) * s_ref[...]
            o_ref[...] = y.astype(o_ref.dtype)


def pool_mixer(u, row0, n_seq, seq, w, s):
    return pl.pallas_call(
        functools.partial(_pool_kernel, seq),
        grid=(n_seq, len(POOL_WINDOWS)),
        in_specs=[pl.BlockSpec((seq, POOL_GW), lambda b, g: (row0 // seq + b, g)),
                  pl.BlockSpec((1, POOL_GW, POOL_GW), lambda b, g: (g, 0, 0)),
                  pl.BlockSpec((1, POOL_GW), lambda b, g: (0, g))],
        out_specs=pl.BlockSpec((seq, POOL_GW), lambda b, g: (b, g)),
        out_shape=jax.ShapeDtypeStruct((n_seq * seq, GROUP_W), bf16),
        compiler_params=_cparams(("parallel", "parallel")),
        name="pool_mixer",
    )(u, w, s.reshape(1, GROUP_W))


GLA_QK = GLA_HEADS * GLA_DK


def _gla_kernel(reverse, final, n_chunks, *refs):
    if final:
        (q_ref, k_ref, v_ref, gd_ref, wup_ref, bup_ref, s0_ref, op_ref, gg_ref, gn_ref,
         o_ref, s_ref, st_ref) = refs
    else:
        q_ref, k_ref, v_ref, gd_ref, wup_ref, bup_ref, s0_ref, o_ref, s_ref, st_ref = refs
    j = pl.program_id(1)
    C = GLA_CHUNK

    @pl.when(j == 0)
    def _():
        for h in range(GLA_HEADS):
            s0 = s0_ref[0, h]
            z = jnp.zeros_like(s0)
            st_ref[h] = jnp.concatenate([s0, z] if h % 2 == 0 else [z, s0], axis=0).T

    ti = lax.broadcasted_iota(jnp.int32, (C, C), 0)
    tj = lax.broadcasted_iota(jnp.int32, (C, C), 1)
    tri = (ti <= tj) if reverse else (ti >= tj)
    tri_b = jnp.where(tri, 1.0, 0.0).astype(bf16)
    first_half = lax.broadcasted_iota(jnp.int32, (C, LANES), 1) < GLA_DK
    ref_row = C // 2 - 1 if reverse else C // 2
    end_row = 0 if reverse else C - 1
    nt = (((1,), (1,)), ((), ()))

    def chunk(ci, carry):
        c = n_chunks - 1 - ci if reverse else ci
        rows = pl.ds(pl.multiple_of(c * C, C), C)
        logit = jnp.dot(gd_ref[rows, :].astype(bf16), wup_ref[...], preferred_element_type=f32) + bup_ref[...]
        la = (jnp.minimum(logit, 0.0) - jnp.log1p(jnp.exp(-jnp.abs(logit)))) * (1.0 / GLA_GATE_NORM)
        hi = la.astype(bf16)
        r1 = la - hi.astype(f32)
        mid = r1.astype(bf16)
        lo = (r1 - mid.astype(f32)).astype(bf16)
        b = (jnp.dot(tri_b, hi, preferred_element_type=f32) + jnp.dot(tri_b, mid, preferred_element_type=f32)
             + jnp.dot(tri_b, lo, preferred_element_type=f32))
        bref = b[ref_row:ref_row + 1, :]
        bl = b[end_row:end_row + 1, :]
        q = q_ref[rows, :] * GLA_DK ** -0.5
        k = k_ref[rows, :]
        qs = q * jnp.exp(b - bref)
        ks = (k * jnp.exp(bref - b)).astype(bf16)
        qe = (q * jnp.exp(b)).astype(bf16)
        kd = k * jnp.exp(bl - b)
        ebl = jnp.exp(bl)
        for h in range(GLA_HEADS):
            pair = slice((h // 2) * LANES, (h // 2 + 1) * LANES)
            mine = first_half if h % 2 == 0 else jnp.logical_not(first_half)
            cols = slice(h * GLA_DV, (h + 1) * GLA_DV)
            a = lax.dot_general(jnp.where(mine, qs[:, pair], 0.0).astype(bf16), ks[:, pair], nt,
                                preferred_element_type=f32)
            a = jnp.where(tri, a, 0.0).astype(bf16)
            vh = v_ref[rows, cols]
            st = st_ref[h]
            o = (jnp.dot(a, vh.astype(bf16), preferred_element_type=f32)
                 + lax.dot_general(qe[:, pair], st.astype(bf16), nt, preferred_element_type=f32))
            st_ref[h] = st * ebl[:, pair] + jnp.dot(vh.T.astype(bf16),
                                                    jnp.where(mine, kd[:, pair], 0.0).astype(bf16),
                                                    preferred_element_type=f32)
            if final:
                o = o + op_ref[rows, cols]
                o = o * lax.rsqrt(jnp.mean(o * o, axis=-1, keepdims=True) + EPS) * gn_ref[...]
                gate = gg_ref[rows, cols]
                o = o * (gate * jax.nn.sigmoid(gate))
            o_ref[rows, cols] = o.astype(o_ref.dtype)
        return carry

    lax.fori_loop(0, n_chunks, chunk, 0)

    @pl.when(j == pl.num_programs(1) - 1)
    def _():
        for h in range(GLA_HEADS):
            half = (h % 2) * GLA_DK
            s_ref[0, h] = st_ref[h].T[half:half + GLA_DK, :]


def _gla_pass(u, row0, n_seq, seq, tb, z, w_up, b_up, s0, o_prev=None, g_norm=None):
    reverse = z == 1
    final = o_prev is not None
    nblk = seq // tb
    rb0 = row0 // tb
    wup = jnp.zeros((LANES, GLA_QK), f32).at[z * GLA_RANK:(z + 1) * GLA_RANK].set(w_up[z]).astype(bf16)

    def blk(b, j):
        return b * nblk + (nblk - 1 - j if reverse else j)

    def ucol(width, c0):
        return pl.BlockSpec((tb, width), lambda b, j: (rb0 + blk(b, j), c0 // width))

    in_specs = [ucol(GLA_QK, C_GQ), ucol(GLA_QK, C_GK), ucol(GROUP_W, C_GV), ucol(LANES, C_GD),
                pl.BlockSpec((LANES, GLA_QK), lambda b, j: (0, 0)),
                pl.BlockSpec((1, GLA_QK), lambda b, j: (0, 0)),
                pl.BlockSpec((1, GLA_HEADS, GLA_DK, GLA_DV), lambda b, j: (b, 0, 0, 0))]
    args = [u, u, u, u, wup, b_up[z].reshape(1, GLA_QK), s0]
    if final:
        in_specs += [pl.BlockSpec((tb, GROUP_W), lambda b, j: (blk(b, j), 0)), ucol(GROUP_W, C_GG),
                     pl.BlockSpec((1, GLA_DV), lambda b, j: (0, 0))]
        args += [o_prev, u, g_norm.reshape(1, GLA_DV)]
    return pl.pallas_call(
        functools.partial(_gla_kernel, reverse, final, tb // GLA_CHUNK),
        grid=(n_seq, nblk),
        in_specs=in_specs,
        out_specs=[pl.BlockSpec((tb, GROUP_W), lambda b, j: (blk(b, j), 0)),
                   pl.BlockSpec((1, GLA_HEADS, GLA_DK, GLA_DV), lambda b, j: (b, 0, 0, 0))],
        out_shape=[jax.ShapeDtypeStruct((n_seq * seq, GROUP_W), bf16 if final else f32),
                   jax.ShapeDtypeStruct((n_seq, GLA_HEADS, GLA_DK, GLA_DV), f32)],
        scratch_shapes=[pltpu.VMEM((GLA_HEADS, GLA_DV, LANES), f32)],
        compiler_params=_cparams(("parallel", "arbitrary")),
        name="gla_backward" if reverse else "gla_forward",
    )(*args)


def gla_mixer(u, row0, n_seq, seq, tb, w_up, b_up, g_norm, s0_f, s0_b):
    o_f, s_f = _gla_pass(u, row0, n_seq, seq, tb, 0, w_up, b_up, s0_f)
    o, s_b = _gla_pass(u, row0, n_seq, seq, tb, 1, w_up, b_up, s0_b, o_prev=o_f, g_norm=g_norm)
    return o, jnp.stack([s_f, s_b], axis=1)


def kernel(x_prompt, x_sample, cache_diff_k, cache_diff_v, cache_nat_k, cache_nat_v, state_gla, c, c_ctx, w_ada, b_ada, norm1, w_in, pool_w, pool_scale, gla_w_up, gla_b_up, gla_norm, diff_lambda, diff_norm, nat_rpb, w_out, norm2, router_group_w, router_group_b, router_expert_w, router_expert_b, expert_w_gate, expert_w_up, expert_w_down, norm_final):
    Bp, Lp, Bs, Ls = BATCH, SEQ, DEC_BATCH, DEC_SEQ
    x = jnp.concatenate([x_prompt.reshape(N_PROMPT, D_MODEL), x_sample.reshape(N_SAMPLE, D_MODEL)], axis=0)
    cvec = jnp.zeros((MOD_ROWS, D_MODEL), f32).at[0].set(c_ctx).at[1:1 + DEC_BATCH].set(c)
    mod_all = ada_modulation(cvec, w_ada, b_ada).reshape(DEPTH, MOD_ROWS, 6, D_MODEL)

    new_dk, new_dv, new_nk, new_nv, new_gs = [], [], [], [], []
    for l in range(DEPTH):
        mod = mod_all[l]
        lam_init = 0.8 - 0.6 * math.exp(-0.3 * l)
        lv = diff_lambda[l].astype(f32)
        lam = jnp.exp(jnp.sum(lv[0] * lv[1])) - jnp.exp(jnp.sum(lv[2] * lv[3])) + lam_init

        w_l = w_in[l]
        w_perm = jnp.concatenate(
            [w_l[:, :ORIG_GD], w_l[:, ORIG_GD + 2 * GLA_RANK:], w_l[:, ORIG_GD:ORIG_GD + 2 * GLA_RANK],
             jnp.zeros((D_MODEL, D_IN_PAD - C_GD - 2 * GLA_RANK), f32)], axis=1).astype(bf16)
        u = input_projection(x, norm1[l], mod, w_perm)

        u_seq = u.reshape(N_TOK // Lp, Lp, D_IN_PAD)
        up = u[:N_PROMPT]

        zero = jnp.zeros((Bp, GLA_HEADS, GLA_DK, GLA_DV), f32)
        o_gla_p, gs = gla_mixer(u, 0, Bp, Lp, Lp, gla_w_up[l], gla_b_up[l], gla_norm[l], zero, zero)
        q2_p, kb_p = diff_prep(u, 0, N_PROMPT, rope=False)
        o_diff_p = flash_attention(q2_p, kb_p.reshape(Bp, Lp, GROUP_W), u_seq, n_batch=Bp, lq=Lp, tq=Lp, tk=Lp,
                                   qcol=0, kcol=0, vcol=C_DV // LANES, lam=lam, g=diff_norm[l],
                                   out_scale=1.0 - lam_init)
        o_nat_p = flash_attention(u.reshape(1, N_TOK, D_IN_PAD), u_seq, u_seq, n_batch=Bp, lq=Lp, tq=Lp, tk=Lp,
                                  qcol=C_NQ // LANES, kcol=C_NK // LANES, vcol=C_NV // LANES,
                                  scale=NAT_HD ** -0.5)
        o_pool_p = pool_mixer(u, 0, Bp, Lp, pool_w[l], pool_scale[l])
        new_dk.append(up[:, C_DK:C_DK + GROUP_W].reshape(Bp, Lp, DIFF_HEADS, 2 * DIFF_HD))
        new_dv.append(up[:, C_DV:C_DV + GROUP_W].reshape(Bp, Lp, DIFF_HEADS, DIFF_VD))
        new_nk.append(up[:, C_NK:C_NK + GROUP_W].reshape(Bp, Lp, NAT_HEADS, NAT_HD))
        new_nv.append(up[:, C_NV:C_NV + GROUP_W].reshape(Bp, Lp, NAT_HEADS, NAT_HD))
        new_gs.append(gs)

        st = state_gla[:, l].astype(f32)
        o_gla_s, _ = gla_mixer(u, N_PROMPT, Bs, Ls, 512, gla_w_up[l], gla_b_up[l], gla_norm[l],
                               st[:, 0], st[:, 1])
        q2_s, kb_s = diff_prep(u, N_PROMPT, N_SAMPLE, rope=True)
        k_all = jnp.concatenate([kb_s.reshape(Bs, Ls, GROUP_W),
                                 cache_diff_k[:, l].reshape(Bs, PAST_LEN, GROUP_W).astype(bf16)], axis=1)
        v_all = jnp.concatenate([u[N_PROMPT:, C_DV:C_DV + GROUP_W].astype(bf16).reshape(Bs, Ls, GROUP_W),
                                 cache_diff_v[:, l].reshape(Bs, PAST_LEN, GROUP_W).astype(bf16)], axis=1)
        o_diff_s = flash_attention(q2_s, k_all, jnp.swapaxes(v_all, 1, 2), n_batch=Bs, lq=Ls, tq=2048, tk=1536,
                                   qcol=0, kcol=0, vcol=0, v_transposed=True,
                                   lam=lam, g=diff_norm[l], out_scale=1.0 - lam_init)
        o_nat_s = nat_attention(u, N_PROMPT, cache_nat_k[:, l].reshape(Bs, PAST_LEN, GROUP_W),
                                cache_nat_v[:, l].reshape(Bs, PAST_LEN, GROUP_W), nat_rpb[l])
        o_pool_s = pool_mixer(u, N_PROMPT, Bs, Ls, pool_w[l], pool_scale[l])

        x = output_projection([o_pool_p, o_gla_p, o_diff_p, o_nat_p], [o_pool_s, o_gla_s, o_diff_s, o_nat_s],
                              w_out[l].astype(bf16), x, mod)

        wr = jnp.zeros((D_MODEL, LANES), f32)
        wr = wr.at[:, :MOE_GROUPS].set(router_group_w[l]).at[:, MOE_GROUPS:MOE_GROUPS + MOE_EXPERTS].set(
            router_expert_w[l])
        br = jnp.zeros((1, LANES), f32)
        br = br.at[0, :MOE_GROUPS].set(router_group_b[l]).at[0, MOE_GROUPS:MOE_GROUPS + MOE_EXPERTS].set(
            router_expert_b[l])
        h2, route, counts = moe_router(x, norm2[l], mod, wr, br)
        pos, slot_token, tile_expert, n_used = route_layout(route, counts)
        ys = expert_ffn(l, tile_expert, n_used, slot_token, h2, expert_w_gate, expert_w_up, expert_w_down)
        y0 = ys.at[pos[:, 0]].get(mode="promise_in_bounds")
        y1 = ys.at[pos[:, 1]].get(mode="promise_in_bounds")
        x = moe_combine(x, y0, y1, route, mod, norm_final if l == DEPTH - 1 else None)

    y_prompt = x[:N_PROMPT].reshape(Bp, Lp, D_MODEL)
    y_sample = x[N_PROMPT:].reshape(Bs, Ls, D_MODEL)
    return (y_prompt, y_sample, jnp.stack(new_dk, axis=1), jnp.stack(new_dv, axis=1),
            jnp.stack(new_nk, axis=1), jnp.stack(new_nv, axis=1), jnp.stack(new_gs, axis=1))
```

```python
import functools
import math

import jax
import jax.numpy as jnp
import numpy as np
from jax import lax
from jax.experimental import pallas as pl
from jax.experimental.pallas import tpu as pltpu

f32 = jnp.float32
bf16 = jnp.bfloat16

D_MODEL = 4096
BATCH = 32
SEQ = 256
DEPTH = 2
DEC_BATCH = 4
DEC_SEQ = 4096
PAST_LEN = 512
GRID_W = 64
GROUP_W = D_MODEL // 4
POOL_WINDOWS = (2, 4, 8, 16)
POOL_GW = GROUP_W // 4
GLA_HEADS = 8
GLA_DV = GROUP_W // GLA_HEADS
GLA_DK = GLA_DV // 2
GLA_RANK = 16
GLA_GATE_NORM = 16.0
GLA_CHUNK = 64
DIFF_HEADS = 8
DIFF_VD = GROUP_W // DIFF_HEADS
DIFF_HD = DIFF_VD // 2
ROPE_BASE = 10000.0
NAT_HEADS = 8
NAT_HD = GROUP_W // NAT_HEADS
NAT_WH = 8
NAT_WW = 16
MOE_GROUPS = 4
MOE_PER_GROUP = 8
MOE_EXPERTS = MOE_GROUPS * MOE_PER_GROUP
MOE_TOPK = 2
MOE_FF = 512
Q_BLOCK = 128
EPS = 1e-6

N_PROMPT = BATCH * SEQ
N_SAMPLE = DEC_BATCH * DEC_SEQ
N_TOK = N_PROMPT + N_SAMPLE
LANES = 128
MOD_ROWS = 8

C_POOL, C_GQ, C_GK, C_GV, C_GG = 0, 1024, 1536, 2048, 3072
C_DQ, C_DK, C_DV, C_NQ, C_NK, C_NV, C_GD = 4096, 5120, 6144, 7168, 8192, 9216, 10240
D_IN_PAD = 10752
ORIG_GD = 3 * GROUP_W + 2 * GLA_HEADS * GLA_DK

TM = 512
TN_IN = 768
TN_OUT = 1024
TM_E = 256
N_ASSIGN = N_TOK * MOE_TOPK
N_ETILES = N_ASSIGN // TM_E + MOE_EXPERTS
VMEM_LIMIT = 56 * 1024 * 1024


def _cparams(sem):
    return pltpu.CompilerParams(dimension_semantics=sem, vmem_limit_bytes=VMEM_LIMIT)


def _mod_row(i):
    npt = N_PROMPT // TM
    return jnp.where(i < npt, 0, 1 + (i - npt) // (DEC_SEQ // TM))


def _ada_kernel(c_ref, w_ref, b_ref, o_ref):
    @pl.when(pl.program_id(2) == 0)
    def _():
        o_ref[0] = jnp.broadcast_to(b_ref[0], o_ref.shape[1:])

    c = c_ref[...]
    a = (c * jax.nn.sigmoid(c)).astype(bf16)
    o_ref[0] += jnp.dot(a, w_ref[0].astype(bf16), preferred_element_type=f32)


def ada_modulation(cvec, w_ada, b_ada):
    tk, tn = 2048, 1024
    n6 = 6 * D_MODEL
    return pl.pallas_call(
        _ada_kernel,
        grid=(DEPTH, n6 // tn, D_MODEL // tk),
        in_specs=[pl.BlockSpec((MOD_ROWS, tk), lambda l, j, k: (0, k)),
                  pl.BlockSpec((1, tk, tn), lambda l, j, k: (l, k, j)),
                  pl.BlockSpec((1, 1, tn), lambda l, j, k: (l, 0, j))],
        out_specs=pl.BlockSpec((1, MOD_ROWS, tn), lambda l, j, k: (l, 0, j)),
        out_shape=jax.ShapeDtypeStruct((DEPTH, MOD_ROWS, n6), f32),
        compiler_params=_cparams(("parallel", "parallel", "arbitrary")),
        name="ada_modulation",
    )(cvec, w_ada, b_ada.reshape(DEPTH, 1, n6))


def _modulated_norm(x, g, shift, scale):
    y = x * lax.rsqrt(jnp.mean(x * x, axis=-1, keepdims=True) + EPS) * g
    return y * (1.0 + scale) + shift


def _win_kernel(x_ref, g_ref, mod_ref, w_ref, o_ref, h_ref):
    @pl.when(pl.program_id(1) == 0)
    def _():
        h = _modulated_norm(x_ref[...], g_ref[...], mod_ref[0, 0:1, :], mod_ref[0, 1:2, :])
        h_ref[...] = h.astype(bf16)

    o_ref[...] = jnp.dot(h_ref[...], w_ref[...], preferred_element_type=f32)


def input_projection(x, g_norm, mod, w):
    return pl.pallas_call(
        _win_kernel,
        grid=(N_TOK // TM, D_IN_PAD // TN_IN),
        in_specs=[pl.BlockSpec((TM, D_MODEL), lambda i, j: (i, 0)),
                  pl.BlockSpec((1, D_MODEL), lambda i, j: (0, 0)),
                  pl.BlockSpec((1, 6, D_MODEL), lambda i, j: (_mod_row(i), 0, 0)),
                  pl.BlockSpec((D_MODEL, TN_IN), lambda i, j: (0, j))],
        out_specs=pl.BlockSpec((TM, TN_IN), lambda i, j: (i, j)),
        out_shape=jax.ShapeDtypeStruct((N_TOK, D_IN_PAD), f32),
        scratch_shapes=[pltpu.VMEM((TM, D_MODEL), bf16)],
        compiler_params=_cparams(("parallel", "arbitrary")),
        name="input_projection",
    )(x, g_norm.reshape(1, D_MODEL), mod, w)


N_MIX = 4


def _wout_kernel(*refs):
    prompt_parts, sample_parts = refs[:N_MIX], refs[N_MIX:2 * N_MIX]
    w_refs = refs[2 * N_MIX:3 * N_MIX]
    x_ref, mod_ref, o_ref = refs[3 * N_MIX:]
    is_prompt = pl.program_id(0) < N_PROMPT // TM

    def project(parts):
        acc = jnp.dot(parts[0][...], w_refs[0][...], preferred_element_type=f32)
        for a, w in zip(parts[1:], w_refs[1:]):
            acc += jnp.dot(a[...], w[...], preferred_element_type=f32)
        o_ref[...] = x_ref[...] + mod_ref[0, 2:3, :] * acc

    @pl.when(is_prompt)
    def _():
        project(prompt_parts)

    @pl.when(jnp.logical_not(is_prompt))
    def _():
        project(sample_parts)


def output_projection(prompt_parts, sample_parts, w, x, mod):
    npt = N_PROMPT // TM
    p_specs = [pl.BlockSpec((TM, GROUP_W), lambda i, j: (jnp.minimum(i, npt - 1), 0)) for _ in range(N_MIX)]
    s_specs = [pl.BlockSpec((TM, GROUP_W), lambda i, j: (jnp.maximum(i - npt, 0), 0)) for _ in range(N_MIX)]
    w_specs = [pl.BlockSpec((GROUP_W, TN_OUT), functools.partial(lambda i, j, c: (c, j), c=c))
               for c in range(N_MIX)]
    return pl.pallas_call(
        _wout_kernel,
        grid=(N_TOK // TM, D_MODEL // TN_OUT),
        in_specs=p_specs + s_specs + w_specs + [
            pl.BlockSpec((TM, TN_OUT), lambda i, j: (i, j)),
            pl.BlockSpec((1, 6, TN_OUT), lambda i, j: (_mod_row(i), 0, j))],
        out_specs=pl.BlockSpec((TM, TN_OUT), lambda i, j: (i, j)),
        out_shape=jax.ShapeDtypeStruct((N_TOK, D_MODEL), f32),
        compiler_params=_cparams(("parallel", "arbitrary")),
        name="output_projection",
    )(*prompt_parts, *sample_parts, *([w] * N_MIX), x, mod)


def _split_bf16(a):
    hi = a.astype(bf16)
    lo = (a - hi.astype(f32)).astype(bf16)
    return hi, lo


def _router_kernel(x_ref, g_ref, mod_ref, wr_ref, br_ref, h_ref, r_ref, cnt_ref):
    h = _modulated_norm(x_ref[...], g_ref[...], mod_ref[0, 3:4, :], mod_ref[0, 4:5, :])
    h_ref[...] = h
    h_hi, h_lo = _split_bf16(h)
    w_hi, w_lo = _split_bf16(wr_ref[...])
    lg = (jnp.dot(h_hi, w_hi, preferred_element_type=f32)
          + jnp.dot(h_lo, w_hi, preferred_element_type=f32)
          + jnp.dot(h_hi, w_lo, preferred_element_type=f32)) + br_ref[...]
    lane = lax.broadcasted_iota(jnp.int32, lg.shape, 1).astype(f32)
    ninf = -jnp.inf

    def first_max(v):
        m = jnp.max(v, axis=-1, keepdims=True)
        idx = jnp.min(jnp.where(v == m, lane, float(LANES)), axis=-1, keepdims=True)
        return m, idx

    gmask = lane < MOE_GROUPS
    mg, gi = first_max(jnp.where(gmask, lg, ninf))
    p_top = 1.0 / jnp.sum(jnp.where(gmask, jnp.exp(lg - mg), 0.0), axis=-1, keepdims=True)
    lo = MOE_GROUPS + gi * MOE_PER_GROUP
    le = jnp.where((lane >= lo) & (lane < lo + MOE_PER_GROUP), lg, ninf)
    v1, i1 = first_max(le)
    v2, i2 = first_max(jnp.where(lane == i1, ninf, le))
    t = jnp.exp(v2 - v1)
    w1 = p_top / (1.0 + t)
    w2 = w1 * t

    @pl.when(pl.program_id(0) == 0)
    def _():
        cnt_ref[...] = jnp.zeros(cnt_ref.shape, f32)

    tm = lg.shape[0]
    chosen = jnp.where((lane == i1) | (lane == i2), 1.0, 0.0)
    earlier = (lax.broadcasted_iota(jnp.int32, (tm, tm), 1) < lax.broadcasted_iota(jnp.int32, (tm, tm), 0))
    before = jnp.dot(jnp.where(earlier, 1.0, 0.0).astype(bf16), chosen.astype(bf16),
                     preferred_element_type=f32) + cnt_ref[0:1, :]
    rank1 = jnp.sum(jnp.where(lane == i1, before, 0.0), axis=-1, keepdims=True)
    rank2 = jnp.sum(jnp.where(lane == i2, before, 0.0), axis=-1, keepdims=True)
    cnt_ref[...] = cnt_ref[...] + jnp.sum(chosen, axis=0, keepdims=True)

    slab = jnp.zeros(lg.shape, f32)
    for pos, val in enumerate((i1 - MOE_GROUPS, i2 - MOE_GROUPS, w1, w2, rank1, rank2)):
        slab = jnp.where(lane == float(pos), val, slab)
    r_ref[...] = slab


def moe_router(x, g_norm, mod, wr, br):
    tm = 256
    return pl.pallas_call(
        _router_kernel,
        grid=(N_TOK // tm,),
        in_specs=[pl.BlockSpec((tm, D_MODEL), lambda i: (i, 0)),
                  pl.BlockSpec((1, D_MODEL), lambda i: (0, 0)),
                  pl.BlockSpec((1, 6, D_MODEL), lambda i: (_mod_row(i // (TM // tm)), 0, 0)),
                  pl.BlockSpec((D_MODEL, LANES), lambda i: (0, 0)),
                  pl.BlockSpec((1, LANES), lambda i: (0, 0))],
        out_specs=[pl.BlockSpec((tm, D_MODEL), lambda i: (i, 0)),
                   pl.BlockSpec((tm, LANES), lambda i: (i, 0)),
                   pl.BlockSpec((MOD_ROWS, LANES), lambda i: (0, 0))],
        out_shape=[jax.ShapeDtypeStruct((N_TOK, D_MODEL), f32),
                   jax.ShapeDtypeStruct((N_TOK, LANES), f32),
                   jax.ShapeDtypeStruct((MOD_ROWS, LANES), f32)],
        compiler_params=_cparams(("arbitrary",)),
        name="moe_router",
    )(x, g_norm.reshape(1, D_MODEL), mod, wr, br)


def _ffn_kernel(te_ref, nu_ref, tok_ref, h_hbm, wg_ref, wu_ref, wd_ref, o_ref, xbuf, sem, wg_b, wu_b, wd_b):
    t = pl.program_id(0)
    n_used = nu_ref[0]

    def row_copy(tile, r):
        half = tile % 2
        return pltpu.make_async_copy(h_hbm.at[pl.ds(tok_ref[tile * TM_E + r], 1)],
                                     xbuf.at[half, pl.ds(r, 1)], sem.at[half])

    def for_rows(tile, fn):
        def body(r, carry):
            fn(row_copy(tile, r))
            return carry
        lax.fori_loop(0, TM_E, body, 0, unroll=32)

    @pl.when(t == 0)
    def _():
        for_rows(t, lambda cp: cp.start())

    @pl.when(t + 1 < n_used)
    def _():
        for_rows(t + 1, lambda cp: cp.start())

    @pl.when(t < n_used)
    def _():
        @pl.when((t == 0) | (te_ref[t] != te_ref[jnp.maximum(t - 1, 0)]))
        def _():
            wg_b[...] = wg_ref[0, 0].astype(bf16)
            wu_b[...] = wu_ref[0, 0].astype(bf16)
            wd_b[...] = wd_ref[0, 0].astype(bf16)

        for_rows(t, lambda cp: cp.wait())
        x = xbuf[t % 2].astype(bf16)
        a = jnp.dot(x, wg_b[...], preferred_element_type=f32)
        u = jnp.dot(x, wu_b[...], preferred_element_type=f32)
        mid = (a * jax.nn.sigmoid(a) * u).astype(bf16)
        o_ref[...] = jnp.dot(mid, wd_b[...], preferred_element_type=f32).astype(o_ref.dtype)

    @pl.when(t >= nu_ref[0])
    def _():
        o_ref[...] = jnp.zeros(o_ref.shape, o_ref.dtype)


def expert_ffn(layer, tile_expert, n_used, slot_token, h, wg, wu, wd):
    def w_spec(r, c):
        return pl.BlockSpec((1, 1, r, c), lambda t, te, nu, tok: (layer, te[t], 0, 0),
                            pipeline_mode=pl.Buffered(1))

    return pl.pallas_call(
        _ffn_kernel,
        grid_spec=pltpu.PrefetchScalarGridSpec(
            num_scalar_prefetch=3,
            grid=(N_ETILES,),
            in_specs=[pl.BlockSpec(memory_space=pl.ANY),
                      w_spec(D_MODEL, MOE_FF), w_spec(D_MODEL, MOE_FF), w_spec(MOE_FF, D_MODEL)],
            out_specs=pl.BlockSpec((TM_E, D_MODEL), lambda t, te, nu, tok: (t, 0)),
            scratch_shapes=[pltpu.VMEM((2, TM_E, D_MODEL), f32), pltpu.SemaphoreType.DMA((2,)),
                            pltpu.VMEM((D_MODEL, MOE_FF), bf16), pltpu.VMEM((D_MODEL, MOE_FF), bf16),
                            pltpu.VMEM((MOE_FF, D_MODEL), bf16)]),
        out_shape=jax.ShapeDtypeStruct((N_ETILES * TM_E, D_MODEL), bf16),
        compiler_params=_cparams(("arbitrary",)),
        name="expert_ffn",
    )(tile_expert, n_used, slot_token, h, wg, wu, wd)


def route_layout(route, counts):
    e = route[:, :MOE_TOPK].astype(jnp.int32)
    rank = route[:, 4:4 + MOE_TOPK].astype(jnp.int32)
    cnt = counts[0, MOE_GROUPS:MOE_GROUPS + MOE_EXPERTS].astype(jnp.int32)
    tiles_e = (cnt + TM_E - 1) // TM_E
    tile_end = jnp.cumsum(tiles_e)
    tile_start = tile_end - tiles_e
    pos = tile_start[e] * TM_E + rank
    slot_token = jnp.zeros((N_ETILES * TM_E,), jnp.int32).at[pos.reshape(-1)].set(
        jnp.arange(N_ASSIGN, dtype=jnp.int32) // MOE_TOPK)
    n_used = tile_end[-1]
    t = jnp.minimum(jnp.arange(N_ETILES, dtype=jnp.int32), n_used - 1)
    tile_expert = jnp.sum((t[:, None] >= tile_end[None, :]).astype(jnp.int32), axis=1)
    tile_expert = jnp.minimum(tile_expert, MOE_EXPERTS - 1)
    return pos, slot_token, tile_expert, n_used.reshape(1).astype(jnp.int32)


def _combine_kernel(x_ref, y0_ref, y1_ref, r_ref, mod_ref, o_ref):
    y = r_ref[:, 2:3] * y0_ref[...] + r_ref[:, 3:4] * y1_ref[...]
    o_ref[...] = x_ref[...] + mod_ref[0, 5:6, :] * y


def _combine_norm_kernel(x_ref, y0_ref, y1_ref, r_ref, mod_ref, g_ref, o_ref):
    y = r_ref[:, 2:3] * y0_ref[...] + r_ref[:, 3:4] * y1_ref[...]
    x = x_ref[...] + mod_ref[0, 5:6, :] * y
    o_ref[...] = x * lax.rsqrt(jnp.mean(x * x, axis=-1, keepdims=True) + EPS) * g_ref[...]


def moe_combine(x, y0, y1, route, mod, g_final=None):
    tm = 256
    row = pl.BlockSpec((tm, D_MODEL), lambda i: (i, 0))
    in_specs = [row, row, row, pl.BlockSpec((tm, LANES), lambda i: (i, 0)),
                pl.BlockSpec((1, 6, D_MODEL), lambda i: (_mod_row(i // (TM // tm)), 0, 0))]
    args = [x, y0, y1, route, mod]
    body = _combine_kernel
    if g_final is not None:
        in_specs.append(pl.BlockSpec((1, D_MODEL), lambda i: (0, 0)))
        args.append(g_final.reshape(1, D_MODEL))
        body = _combine_norm_kernel
    return pl.pallas_call(
        body,
        grid=(N_TOK // tm,),
        in_specs=in_specs,
        out_specs=row,
        out_shape=jax.ShapeDtypeStruct((N_TOK, D_MODEL), f32),
        compiler_params=_cparams(("parallel",)),
        name="moe_combine",
    )(*args)


def _diff_prep_kernel(rope, q_ref, k_ref, cos_ref, sin_ref, q2_ref, kb_ref):
    lane = lax.broadcasted_iota(jnp.int32, q_ref.shape, 1)

    def rot(x):
        if not rope:
            return x
        partner = jnp.where((lane & 31) >= 16, pltpu.roll(x, 16, 1), pltpu.roll(x, LANES - 16, 1))
        return x * cos_ref[...] + partner * sin_ref[...]

    q = rot(q_ref[...]) * DIFF_HD ** -0.5
    first = lane < DIFF_HD
    q2_ref[0] = jnp.where(first, q, 0.0).astype(bf16)
    q2_ref[1] = jnp.where(first, 0.0, q).astype(bf16)
    kb_ref[...] = rot(k_ref[...]).astype(bf16)


def _rope_tables():
    t = jnp.arange(DEC_SEQ)
    row = (t // GRID_W).astype(f32)
    col = (t % GRID_W).astype(f32)
    nf = DIFF_HD // 4
    inv = ROPE_BASE ** (-jnp.arange(nf, dtype=f32) / nf)
    lane = np.arange(LANES)
    pos = jnp.where(((lane // 32) % 2 == 0)[None, :], row[:, None], col[:, None])
    ang = pos * inv[lane % nf][None, :]
    sign = np.where(lane % 32 >= nf, 1.0, -1.0).astype(np.float32)
    return jnp.cos(ang), jnp.sin(ang) * sign[None, :]


def diff_prep(u, row0, n_rows, rope):
    t = 1024
    r0 = row0 // t
    nseq = DEC_SEQ // t
    cos, sin = _rope_tables() if rope else (jnp.zeros((t, LANES), f32), jnp.zeros((t, LANES), f32))
    tab = pl.BlockSpec((t, LANES), (lambda i, h: (i % nseq, 0)) if rope else (lambda i, h: (0, 0)))
    return pl.pallas_call(
        functools.partial(_diff_prep_kernel, rope),
        grid=(n_rows // t, DIFF_HEADS),
        in_specs=[pl.BlockSpec((t, LANES), lambda i, h: (r0 + i, C_DQ // LANES + h)),
                  pl.BlockSpec((t, LANES), lambda i, h: (r0 + i, C_DK // LANES + h)),
                  tab, tab],
        out_specs=[pl.BlockSpec((2, t, LANES), lambda i, h: (0, i, h)),
                   pl.BlockSpec((t, LANES), lambda i, h: (i, h))],
        out_shape=[jax.ShapeDtypeStruct((2, n_rows, GROUP_W), bf16),
                   jax.ShapeDtypeStruct((n_rows, GROUP_W), bf16)],
        compiler_params=_cparams(("parallel", "parallel")),
        name="diff_prep",
    )(u, u, cos, sin)


def _flash_kernel(n_stack, tq, sub, scale, out_scale, v_transposed, q_ref, k_ref, v_ref, lam_ref, g_ref, o_ref,
                  m_ref, l_ref, acc_ref):
    kj = pl.program_id(3)

    @pl.when(kj == 0)
    def _():
        m_ref[...] = jnp.full(m_ref.shape, -jnp.inf, f32)
        l_ref[...] = jnp.zeros(l_ref.shape, f32)
        acc_ref[...] = jnp.zeros(acc_ref.shape, f32)

    q = q_ref[...].reshape(n_stack * tq, LANES).astype(bf16)
    tk = k_ref.shape[1]
    m, l, acc = m_ref[...], l_ref[...], acc_ref[...]
    for c in range(tk // sub):
        keys = slice(c * sub, (c + 1) * sub)
        s = lax.dot_general(k_ref[0, keys, :].astype(bf16), q, (((1,), (1,)), ((), ())),
                            preferred_element_type=f32)
        if scale != 1.0:
            s = s * scale
        m_new = jnp.maximum(m, jnp.max(s, axis=0, keepdims=True))
        alpha = jnp.exp(m - m_new)
        p = jnp.exp(s - m_new)
        l = alpha * l + jnp.sum(p, axis=0, keepdims=True)
        vt = v_ref[0, :, keys].astype(bf16) if v_transposed else v_ref[0, keys, :].T.astype(bf16)
        acc = alpha * acc + jnp.dot(vt, p.astype(bf16), preferred_element_type=f32)
        m = m_new
    m_ref[...], l_ref[...], acc_ref[...] = m, l, acc

    @pl.when(kj == pl.num_programs(3) - 1)
    def _():
        o = acc_ref[...] / l_ref[...]
        if n_stack == 2:
            o = o[:, :tq] - lam_ref[0:1, 0:1] * o[:, tq:]
            o = o * lax.rsqrt(jnp.mean(o * o, axis=0, keepdims=True) + EPS)
            o = o.T * (g_ref[...] * out_scale)
        else:
            o = o.T
        o_ref[...] = o.astype(o_ref.dtype)


def flash_attention(q, k, v, *, n_batch, lq, tq, tk, qcol, kcol, vcol, sub=512, v_transposed=False, qrow0=0, kb0=0,
                    scale=1.0, lam=None, g=None, out_scale=1.0):
    n_stack = q.shape[0]
    lk = k.shape[1]
    nq = lq // tq
    lam = jnp.zeros((1, LANES), f32) if lam is None else jnp.full((1, LANES), lam, f32)
    g = jnp.ones((1, LANES), f32) if g is None else g.reshape(1, LANES).astype(f32)
    rows = n_stack * tq
    if v_transposed:
        v_spec = pl.BlockSpec((1, LANES, tk), lambda b, h, i, j: (kb0 + b, vcol + h, j))
    else:
        v_spec = pl.BlockSpec((1, tk, LANES), lambda b, h, i, j: (kb0 + b, j, vcol + h))
    return pl.pallas_call(
        functools.partial(_flash_kernel, n_stack, tq, min(sub, tk), scale, out_scale, v_transposed),
        grid=(n_batch, DIFF_HEADS, nq, lk // tk),
        in_specs=[pl.BlockSpec((n_stack, tq, LANES), lambda b, h, i, j: (0, qrow0 + b * nq + i, qcol + h)),
                  pl.BlockSpec((1, tk, LANES), lambda b, h, i, j: (kb0 + b, j, kcol + h)),
                  v_spec,
                  pl.BlockSpec((1, LANES), lambda b, h, i, j: (0, 0)),
                  pl.BlockSpec((1, LANES), lambda b, h, i, j: (0, 0))],
        out_specs=pl.BlockSpec((tq, LANES), lambda b, h, i, j: (b * nq + i, h)),
        out_shape=jax.ShapeDtypeStruct((n_batch * lq, GROUP_W), bf16),
        scratch_shapes=[pltpu.VMEM((1, rows), f32), pltpu.VMEM((1, rows), f32),
                        pltpu.VMEM((LANES, rows), f32)],
        compiler_params=_cparams(("parallel", "parallel", "parallel", "arbitrary")),
        name="flash_attention",
    )(q, k, v, lam, g)


NAT_ROWS = DEC_SEQ // GRID_W
NAT_RB = 8
NAT_BAND = NAT_WH * GRID_W
NEG_BIG = -1e30


def _nat_kernel(q_ref, k_ref, v_ref, kc_ref, vc_ref, bias_ref, o_ref):
    rb = pl.program_id(2)
    scale = NAT_HD ** -0.5
    kc = kc_ref[0].astype(bf16)
    vc = vc_ref[0].astype(bf16)
    nt = (((1,), (1,)), ((), ()))
    for i in range(NAT_RB):
        r = rb * NAT_RB + i
        rs = jnp.clip(r - NAT_WH // 2, 0, NAT_ROWS - NAT_WH)
        k0 = pl.multiple_of(rs * GRID_W, GRID_W)
        kband = k_ref[pl.ds(k0, NAT_BAND), :].astype(bf16)
        vband = v_ref[pl.ds(k0, NAT_BAND), :].astype(bf16)
        q = q_ref[i * GRID_W:(i + 1) * GRID_W, :].astype(bf16)
        sb = lax.dot_general(q, kband, nt, preferred_element_type=f32) * scale + bias_ref[rs - r + NAT_WH - 1, 0]
        sc = lax.dot_general(q, kc, nt, preferred_element_type=f32) * scale
        m = jnp.maximum(jnp.max(sb, axis=-1, keepdims=True), jnp.max(sc, axis=-1, keepdims=True))
        pb = jnp.exp(sb - m)
        pc = jnp.exp(sc - m)
        l = jnp.sum(pb, axis=-1, keepdims=True) + jnp.sum(pc, axis=-1, keepdims=True)
        o = (jnp.dot(pb.astype(bf16), vband, preferred_element_type=f32)
             + jnp.dot(pc.astype(bf16), vc, preferred_element_type=f32)) / l
        o_ref[i * GRID_W:(i + 1) * GRID_W, :] = o.astype(o_ref.dtype)


def _nat_bias_table(rpb):
    colv = np.arange(GRID_W)
    cstart = np.clip(colv - NAT_WW // 2, 0, GRID_W - NAT_WW)
    col_mask = (colv[None, :] >= cstart[:, None]) & (colv[None, :] < cstart[:, None] + NAT_WW)
    col_idx = np.clip(colv[None, :] - colv[:, None] + NAT_WW - 1, 0, 2 * NAT_WW - 2)
    rpb_c = rpb.astype(f32)[:, :, col_idx]
    row_idx = np.arange(NAT_WH)[:, None] + np.arange(NAT_WH)[None, :]
    tab = rpb_c[:, row_idx]
    tab = jnp.where(col_mask[None, None, None], tab, NEG_BIG)
    return tab.transpose(1, 0, 3, 2, 4).reshape(NAT_WH, NAT_HEADS, GRID_W, NAT_BAND)


def nat_attention(u, row0, kc, vc, rpb):
    qblk = NAT_RB * GRID_W
    q0 = row0 // qblk
    b0 = row0 // DEC_SEQ
    return pl.pallas_call(
        _nat_kernel,
        grid=(DEC_BATCH, NAT_HEADS, NAT_ROWS // NAT_RB),
        in_specs=[pl.BlockSpec((qblk, LANES), lambda b, h, r: (q0 + b * (NAT_ROWS // NAT_RB) + r, C_NQ // LANES + h)),
                  pl.BlockSpec((DEC_SEQ, LANES), lambda b, h, r: (b0 + b, C_NK // LANES + h)),
                  pl.BlockSpec((DEC_SEQ, LANES), lambda b, h, r: (b0 + b, C_NV // LANES + h)),
                  pl.BlockSpec((1, PAST_LEN, LANES), lambda b, h, r: (b, 0, h)),
                  pl.BlockSpec((1, PAST_LEN, LANES), lambda b, h, r: (b, 0, h)),
                  pl.BlockSpec((NAT_WH, 1, GRID_W, NAT_BAND), lambda b, h, r: (0, h, 0, 0))],
        out_specs=pl.BlockSpec((qblk, LANES), lambda b, h, r: (b * (NAT_ROWS // NAT_RB) + r, h)),
        out_shape=jax.ShapeDtypeStruct((N_SAMPLE, GROUP_W), bf16),
        compiler_params=_cparams(("parallel", "parallel", "arbitrary")),
        name="nat_attention",
    )(u, u, u, kc, vc, _nat_bias_table(rpb))


def _pool_kernel(seq, u_ref, w_ref, s_ref, o_ref):
    grp = pl.program_id(1)
    for gi, win in enumerate(POOL_WINDOWS):
        @pl.when(grp == gi)
        def _(win=win):
            u = u_ref[...]
            t = lax.broadcasted_iota(jnp.int32, u.shape, 0)
            acc = jnp.zeros_like(u)
            for d in range(-(win // 2), win // 2):
                shifted = u if d == 0 else pltpu.roll(u, (-d) % seq, 0)
                acc += jnp.where((t + d >= 0) & (t + d < seq), shifted, 0.0)
            cnt = (jnp.minimum(t + win // 2, seq) - jnp.maximum(t - win // 2, 0)).astype(f32)
            p = acc / cnt - u
            y = jnp.dot(p.astype(bf16), w_ref[0].astype(bf16), preferred_element_type=f32) * s_ref[...]
            o_ref[...] = y.astype(o_ref.dtype)


def pool_mixer(u, row0, n_seq, seq, w, s):
    return pl.pallas_call(
        functools.partial(_pool_kernel, seq),
        grid=(n_seq, len(POOL_WINDOWS)),
        in_specs=[pl.BlockSpec((seq, POOL_GW), lambda b, g: (row0 // seq + b, g)),
                  pl.BlockSpec((1, POOL_GW, POOL_GW), lambda b, g: (g, 0, 0)),
                  pl.BlockSpec((1, POOL_GW), lambda b, g: (0, g))],
        out_specs=pl.BlockSpec((seq, POOL_GW), lambda b, g: (b, g)),
        out_shape=jax.ShapeDtypeStruct((n_seq * seq, GROUP_W), bf16),
        compiler_params=_cparams(("parallel", "parallel")),
        name="pool_mixer",
    )(u, w, s.reshape(1, GROUP_W))


GLA_QK = GLA_HEADS * GLA_DK


def _gla_kernel(reverse, final, n_chunks, *refs):
    if final:
        (q_ref, k_ref, v_ref, gd_ref, wup_ref, bup_ref, s0_ref, op_ref, gg_ref, gn_ref,
         o_ref, s_ref, st_ref) = refs
    else:
        q_ref, k_ref, v_ref, gd_ref, wup_ref, bup_ref, s0_ref, o_ref, s_ref, st_ref = refs
    j = pl.program_id(1)
    C = GLA_CHUNK

    @pl.when(j == 0)
    def _():
        for h in range(GLA_HEADS):
            s0 = s0_ref[0, h]
            z = jnp.zeros_like(s0)
            st_ref[h] = jnp.concatenate([s0, z] if h % 2 == 0 else [z, s0], axis=0).T

    ti = lax.broadcasted_iota(jnp.int32, (C, C), 0)
    tj = lax.broadcasted_iota(jnp.int32, (C, C), 1)
    tri = (ti <= tj) if reverse else (ti >= tj)
    tri_b = jnp.where(tri, 1.0, 0.0).astype(bf16)
    first_half = lax.broadcasted_iota(jnp.int32, (C, LANES), 1) < GLA_DK
    ref_row = C // 2 - 1 if reverse else C // 2
    end_row = 0 if reverse else C - 1
    nt = (((1,), (1,)), ((), ()))

    def chunk(ci, carry):
        c = n_chunks - 1 - ci if reverse else ci
        rows = pl.ds(pl.multiple_of(c * C, C), C)
        logit = jnp.dot(gd_ref[rows, :].astype(bf16), wup_ref[...], preferred_element_type=f32) + bup_ref[...]
        la = (jnp.minimum(logit, 0.0) - jnp.log1p(jnp.exp(-jnp.abs(logit)))) * (1.0 / GLA_GATE_NORM)
        hi = la.astype(bf16)
        r1 = la - hi.astype(f32)
        mid = r1.astype(bf16)
        lo = (r1 - mid.astype(f32)).astype(bf16)
        b = (jnp.dot(tri_b, hi, preferred_element_type=f32) + jnp.dot(tri_b, mid, preferred_element_type=f32)
             + jnp.dot(tri_b, lo, preferred_element_type=f32))
        bref = b[ref_row:ref_row + 1, :]
        bl = b[end_row:end_row + 1, :]
        q = q_ref[rows, :] * GLA_DK ** -0.5
        k = k_ref[rows, :]
        qs = q * jnp.exp(b - bref)
        ks = (k * jnp.exp(bref - b)).astype(bf16)
        qe = (q * jnp.exp(b)).astype(bf16)
        kd = k * jnp.exp(bl - b)
        ebl = jnp.exp(bl)
        for h in range(GLA_HEADS):
            pair = slice((h // 2) * LANES, (h // 2 + 1) * LANES)
            mine = first_half if h % 2 == 0 else jnp.logical_not(first_half)
            cols = slice(h * GLA_DV, (h + 1) * GLA_DV)
            a = lax.dot_general(jnp.where(mine, qs[:, pair], 0.0).astype(bf16), ks[:, pair], nt,
                                preferred_element_type=f32)
            a = jnp.where(tri, a, 0.0).astype(bf16)
            vh = v_ref[rows, cols]
            st = st_ref[h]
            o = (jnp.dot(a, vh.astype(bf16), preferred_element_type=f32)
                 + lax.dot_general(qe[:, pair], st.astype(bf16), nt, preferred_element_type=f32))
            st_ref[h] = st * ebl[:, pair] + jnp.dot(vh.T.astype(bf16),
                                                    jnp.where(mine, kd[:, pair], 0.0).astype(bf16),
                                                    preferred_element_type=f32)
            if final:
                o = o + op_ref[rows, cols]
                o = o * lax.rsqrt(jnp.mean(o * o, axis=-1, keepdims=True) + EPS) * gn_ref[...]
                gate = gg_ref[rows, cols]
                o = o * (gate * jax.nn.sigmoid(gate))
            o_ref[rows, cols] = o.astype(o_ref.dtype)
        return carry

    lax.fori_loop(0, n_chunks, chunk, 0)

    @pl.when(j == pl.num_programs(1) - 1)
    def _():
        for h in range(GLA_HEADS):
            half = (h % 2) * GLA_DK
            s_ref[0, h] = st_ref[h].T[half:half + GLA_DK, :]


def _gla_pass(u, row0, n_seq, seq, tb, z, w_up, b_up, s0, o_prev=None, g_norm=None):
    reverse = z == 1
    final = o_prev is not None
    nblk = seq // tb
    rb0 = row0 // tb
    wup = jnp.zeros((LANES, GLA_QK), f32).at[z * GLA_RANK:(z + 1) * GLA_RANK].set(w_up[z]).astype(bf16)

    def blk(b, j):
        return b * nblk + (nblk - 1 - j if reverse else j)

    def ucol(width, c0):
        return pl.BlockSpec((tb, width), lambda b, j: (rb0 + blk(b, j), c0 // width))

    in_specs = [ucol(GLA_QK, C_GQ), ucol(GLA_QK, C_GK), ucol(GROUP_W, C_GV), ucol(LANES, C_GD),
                pl.BlockSpec((LANES, GLA_QK), lambda b, j: (0, 0)),
                pl.BlockSpec((1, GLA_QK), lambda b, j: (0, 0)),
                pl.BlockSpec((1, GLA_HEADS, GLA_DK, GLA_DV), lambda b, j: (b, 0, 0, 0))]
    args = [u, u, u, u, wup, b_up[z].reshape(1, GLA_QK), s0]
    if final:
        in_specs += [pl.BlockSpec((tb, GROUP_W), lambda b, j: (blk(b, j), 0)), ucol(GROUP_W, C_GG),
                     pl.BlockSpec((1, GLA_DV), lambda b, j: (0, 0))]
        args += [o_prev, u, g_norm.reshape(1, GLA_DV)]
    return pl.pallas_call(
        functools.partial(_gla_kernel, reverse, final, tb // GLA_CHUNK),
        grid=(n_seq, nblk),
        in_specs=in_specs,
        out_specs=[pl.BlockSpec((tb, GROUP_W), lambda b, j: (blk(b, j), 0)),
                   pl.BlockSpec((1, GLA_HEADS, GLA_DK, GLA_DV), lambda b, j: (b, 0, 0, 0))],
        out_shape=[jax.ShapeDtypeStruct((n_seq * seq, GROUP_W), bf16 if final else f32),
                   jax.ShapeDtypeStruct((n_seq, GLA_HEADS, GLA_DK, GLA_DV), f32)],
        scratch_shapes=[pltpu.VMEM((GLA_HEADS, GLA_DV, LANES), f32)],
        compiler_params=_cparams(("parallel", "arbitrary")),
        name="gla_backward" if reverse else "gla_forward",
    )(*args)


def gla_mixer(u, row0, n_seq, seq, tb, w_up, b_up, g_norm, s0_f, s0_b):
    o_f, s_f = _gla_pass(u, row0, n_seq, seq, tb, 0, w_up, b_up, s0_f)
    o, s_b = _gla_pass(u, row0, n_seq, seq, tb, 1, w_up, b_up, s0_b, o_prev=o_f, g_norm=g_norm)
    return o, jnp.stack([s_f, s_b], axis=1)


def kernel(x_prompt, x_sample, cache_diff_k, cache_diff_v, cache_nat_k, cache_nat_v, state_gla, c, c_ctx, w_ada, b_ada, norm1, w_in, pool_w, pool_scale, gla_w_up, gla_b_up, gla_norm, diff_lambda, diff_norm, nat_rpb, w_out, norm2, router_group_w, router_group_b, router_expert_w, router_expert_b, expert_w_gate, expert_w_up, expert_w_down, norm_final):
    Bp, Lp, Bs, Ls = BATCH, SEQ, DEC_BATCH, DEC_SEQ
    x = jnp.concatenate([x_prompt.reshape(N_PROMPT, D_MODEL), x_sample.reshape(N_SAMPLE, D_MODEL)], axis=0)
    cvec = jnp.zeros((MOD_ROWS, D_MODEL), f32).at[0].set(c_ctx).at[1:1 + DEC_BATCH].set(c)
    mod_all = ada_modulation(cvec, w_ada, b_ada).reshape(DEPTH, MOD_ROWS, 6, D_MODEL)

    new_dk, new_dv, new_nk, new_nv, new_gs = [], [], [], [], []
    for l in range(DEPTH):
        mod = mod_all[l]
        lam_init = 0.8 - 0.6 * math.exp(-0.3 * l)
        lv = diff_lambda[l].astype(f32)
        lam = jnp.exp(jnp.sum(lv[0] * lv[1])) - jnp.exp(jnp.sum(lv[2] * lv[3])) + lam_init

        w_l = w_in[l]
        w_perm = jnp.concatenate(
            [w_l[:, :ORIG_GD], w_l[:, ORIG_GD + 2 * GLA_RANK:], w_l[:, ORIG_GD:ORIG_GD + 2 * GLA_RANK],
             jnp.zeros((D_MODEL, D_IN_PAD - C_GD - 2 * GLA_RANK), f32)], axis=1).astype(bf16)
        u = input_projection(x, norm1[l], mod, w_perm)

        u_seq = u.reshape(N_TOK // Lp, Lp, D_IN_PAD)
        up = u[:N_PROMPT]

        zero = jnp.zeros((Bp, GLA_HEADS, GLA_DK, GLA_DV), f32)
        o_gla_p, gs = gla_mixer(u, 0, Bp, Lp, Lp, gla_w_up[l], gla_b_up[l], gla_norm[l], zero, zero)
        q2_p, kb_p = diff_prep(u, 0, N_PROMPT, rope=False)
        o_diff_p = flash_attention(q2_p, kb_p.reshape(Bp, Lp, GROUP_W), u_seq, n_batch=Bp, lq=Lp, tq=Lp, tk=Lp,
                                   qcol=0, kcol=0, vcol=C_DV // LANES, lam=lam, g=diff_norm[l],
                                   out_scale=1.0 - lam_init)
        o_nat_p = flash_attention(u.reshape(1, N_TOK, D_IN_PAD), u_seq, u_seq, n_batch=Bp, lq=Lp, tq=Lp, tk=Lp,
                                  qcol=C_NQ // LANES, kcol=C_NK // LANES, vcol=C_NV // LANES,
                                  scale=NAT_HD ** -0.5)
        o_pool_p = pool_mixer(u, 0, Bp, Lp, pool_w[l], pool_scale[l])
        new_dk.append(up[:, C_DK:C_DK + GROUP_W].reshape(Bp, Lp, DIFF_HEADS, 2 * DIFF_HD))
        new_dv.append(up[:, C_DV:C_DV + GROUP_W].reshape(Bp, Lp, DIFF_HEADS, DIFF_VD))
        new_nk.append(up[:, C_NK:C_NK + GROUP_W].reshape(Bp, Lp, NAT_HEADS, NAT_HD))
        new_nv.append(up[:, C_NV:C_NV + GROUP_W].reshape(Bp, Lp, NAT_HEADS, NAT_HD))
        new_gs.append(gs)

        st = state_gla[:, l].astype(f32)
        o_gla_s, _ = gla_mixer(u, N_PROMPT, Bs, Ls, 512, gla_w_up[l], gla_b_up[l], gla_norm[l],
                               st[:, 0], st[:, 1])
        q2_s, kb_s = diff_prep(u, N_PROMPT, N_SAMPLE, rope=True)
        k_all = jnp.concatenate([kb_s.reshape(Bs, Ls, GROUP_W),
                                 cache_diff_k[:, l].reshape(Bs, PAST_LEN, GROUP_W).astype(bf16)], axis=1)
        v_all = jnp.concatenate([u[N_PROMPT:, C_DV:C_DV + GROUP_W].astype(bf16).reshape(Bs, Ls, GROUP_W),
                                 cache_diff_v[:, l].reshape(Bs, PAST_LEN, GROUP_W).astype(bf16)], axis=1)
        o_diff_s = flash_attention(q2_s, k_all, jnp.swapaxes(v_all, 1, 2), n_batch=Bs, lq=Ls, tq=2048, tk=1536,
                                   qcol=0, kcol=0, vcol=0, v_transposed=True,
                                   lam=lam, g=diff_norm[l], out_scale=1.0 - lam_init)
        o_nat_s = nat_attention(u, N_PROMPT, cache_nat_k[:, l].reshape(Bs, PAST_LEN, GROUP_W),
                                cache_nat_v[:, l].reshape(Bs, PAST_LEN, GROUP_W), nat_rpb[l])
        o_pool_s = pool_mixer(u, N_PROMPT, Bs, Ls, pool_w[l], pool_scale[l])

        x = output_projection([o_pool_p, o_gla_p, o_diff_p, o_nat_p], [o_pool_s, o_gla_s, o_diff_s, o_nat_s],
                              w_out[l].astype(bf16), x, mod)

        wr = jnp.zeros((D_MODEL, LANES), f32)
        wr = wr.at[:, :MOE_GROUPS].set(router_group_w[l]).at[:, MOE_GROUPS:MOE_GROUPS + MOE_EXPERTS].set(
            router_expert_w[l])
        br = jnp.zeros((1, LANES), f32)
        br = br.at[0, :MOE_GROUPS].set(router_group_b[l]).at[0, MOE_GROUPS:MOE_GROUPS + MOE_EXPERTS].set(
            router_expert_b[l])
        h2, route, counts = moe_router(x, norm2[l], mod, wr, br)
        pos, slot_token, tile_expert, n_used = route_layout(route, counts)
        ys = expert_ffn(l, tile_expert, n_used, slot_token, h2, expert_w_gate, expert_w_up, expert_w_down)
        y0 = ys.at[pos[:, 0]].get(mode="promise_in_bounds")
        y1 = ys.at[pos[:, 1]].get(mode="promise_in_bounds")
        x = moe_combine(x, y0, y1, route, mod, norm_final if l == DEPTH - 1 else None)

    y_prompt = x[:N_PROMPT].reshape(Bp, Lp, D_MODEL)
    y_sample = x[N_PROMPT:].reshape(Bs, Ls, D_MODEL)
    return (y_prompt, y_sample, jnp.stack(new_dk, axis=1), jnp.stack(new_dv, axis=1),
            jnp.stack(new_nk, axis=1), jnp.stack(new_nv, axis=1), jnp.stack(new_gs, axis=1))
```

```python
import functools
import math

import jax
import jax.numpy as jnp
import numpy as np
from jax import lax
from jax.experimental import pallas as pl
from jax.experimental.pallas import tpu as pltpu

f32 = jnp.float32
bf16 = jnp.bfloat16

D_MODEL = 4096
BATCH = 32
SEQ = 256
DEPTH = 2
DEC_BATCH = 4
DEC_SEQ = 4096
PAST_LEN = 512
GRID_W = 64
GROUP_W = D_MODEL // 4
POOL_WINDOWS = (2, 4, 8, 16)
POOL_GW = GROUP_W // 4
GLA_HEADS = 8
GLA_DV = GROUP_W // GLA_HEADS
GLA_DK = GLA_DV // 2
GLA_RANK = 16
GLA_GATE_NORM = 16.0
GLA_CHUNK = 64
DIFF_HEADS = 8
DIFF_VD = GROUP_W // DIFF_HEADS
DIFF_HD = DIFF_VD // 2
ROPE_BASE = 10000.0
NAT_HEADS = 8
NAT_HD = GROUP_W // NAT_HEADS
NAT_WH = 8
NAT_WW = 16
MOE_GROUPS = 4
MOE_PER_GROUP = 8
MOE_EXPERTS = MOE_GROUPS * MOE_PER_GROUP
MOE_TOPK = 2
MOE_FF = 512
Q_BLOCK = 128
EPS = 1e-6

N_PROMPT = BATCH * SEQ
N_SAMPLE = DEC_BATCH * DEC_SEQ
N_TOK = N_PROMPT + N_SAMPLE
LANES = 128
MOD_ROWS = 8

C_POOL, C_GQ, C_GK, C_GV, C_GG = 0, 1024, 1536, 2048, 3072
C_DQ, C_DK, C_DV, C_NQ, C_NK, C_NV, C_GD = 4096, 5120, 6144, 7168, 8192, 9216, 10240
D_IN_PAD = 10752
ORIG_GD = 3 * GROUP_W + 2 * GLA_HEADS * GLA_DK

TM = 512
TN_IN = 768
TN_OUT = 1024
TM_E = 256
N_ASSIGN = N_TOK * MOE_TOPK
N_ETILES = N_ASSIGN // TM_E + MOE_EXPERTS
VMEM_LIMIT = 56 * 1024 * 1024


def _cparams(sem):
    return pltpu.CompilerParams(dimension_semantics=sem, vmem_limit_bytes=VMEM_LIMIT)


def _mod_row(i):
    npt = N_PROMPT // TM
    return jnp.where(i < npt, 0, 1 + (i - npt) // (DEC_SEQ // TM))


def _ada_kernel(c_ref, w_ref, b_ref, o_ref):
    @pl.when(pl.program_id(2) == 0)
    def _():
        o_ref[0] = jnp.broadcast_to(b_ref[0], o_ref.shape[1:])

    c = c_ref[...]
    a = (c * jax.nn.sigmoid(c)).astype(bf16)
    o_ref[0] += jnp.dot(a, w_ref[0].astype(bf16), preferred_element_type=f32)


def ada_modulation(cvec, w_ada, b_ada):
    tk, tn = 2048, 1024
    n6 = 6 * D_MODEL
    return pl.pallas_call(
        _ada_kernel,
        grid=(DEPTH, n6 // tn, D_MODEL // tk),
        in_specs=[pl.BlockSpec((MOD_ROWS, tk), lambda l, j, k: (0, k)),
                  pl.BlockSpec((1, tk, tn), lambda l, j, k: (l, k, j)),
                  pl.BlockSpec((1, 1, tn), lambda l, j, k: (l, 0, j))],
        out_specs=pl.BlockSpec((1, MOD_ROWS, tn), lambda l, j, k: (l, 0, j)),
        out_shape=jax.ShapeDtypeStruct((DEPTH, MOD_ROWS, n6), f32),
        compiler_params=_cparams(("parallel", "parallel", "arbitrary")),
        name="ada_modulation",
    )(cvec, w_ada, b_ada.reshape(DEPTH, 1, n6))


def _modulated_norm(x, g, shift, scale):
    y = x * lax.rsqrt(jnp.mean(x * x, axis=-1, keepdims=True) + EPS) * g
    return y * (1.0 + scale) + shift


def _win_kernel(x_ref, g_ref, mod_ref, w_ref, o_ref, h_ref):
    @pl.when(pl.program_id(1) == 0)
    def _():
        h = _modulated_norm(x_ref[...], g_ref[...], mod_ref[0, 0:1, :], mod_ref[0, 1:2, :])
        h_ref[...] = h.astype(bf16)

    o_ref[...] = jnp.dot(h_ref[...], w_ref[...], preferred_element_type=f32)


def input_projection(x, g_norm, mod, w):
    return pl.pallas_call(
        _win_kernel,
        grid=(N_TOK // TM, D_IN_PAD // TN_IN),
        in_specs=[pl.BlockSpec((TM, D_MODEL), lambda i, j: (i, 0)),
                  pl.BlockSpec((1, D_MODEL), lambda i, j: (0, 0)),
                  pl.BlockSpec((1, 6, D_MODEL), lambda i, j: (_mod_row(i), 0, 0)),
                  pl.BlockSpec((D_MODEL, TN_IN), lambda i, j: (0, j))],
        out_specs=pl.BlockSpec((TM, TN_IN), lambda i, j: (i, j)),
        out_shape=jax.ShapeDtypeStruct((N_TOK, D_IN_PAD), f32),
        scratch_shapes=[pltpu.VMEM((TM, D_MODEL), bf16)],
        compiler_params=_cparams(("parallel", "arbitrary")),
        name="input_projection",
    )(x, g_norm.reshape(1, D_MODEL), mod, w)


N_MIX = 4


def _wout_kernel(*refs):
    prompt_parts, sample_parts = refs[:N_MIX], refs[N_MIX:2 * N_MIX]
    w_refs = refs[2 * N_MIX:3 * N_MIX]
    x_ref, mod_ref, o_ref = refs[3 * N_MIX:]
    is_prompt = pl.program_id(0) < N_PROMPT // TM

    def project(parts):
        acc = jnp.dot(parts[0][...], w_refs[0][...], preferred_element_type=f32)
        for a, w in zip(parts[1:], w_refs[1:]):
            acc += jnp.dot(a[...], w[...], preferred_element_type=f32)
        o_ref[...] = x_ref[...] + mod_ref[0, 2:3, :] * acc

    @pl.when(is_prompt)
    def _():
        project(prompt_parts)

    @pl.when(jnp.logical_not(is_prompt))
    def _():
        project(sample_parts)


def output_projection(prompt_parts, sample_parts, w, x, mod):
    npt = N_PROMPT // TM
    p_specs = [pl.BlockSpec((TM, GROUP_W), lambda i, j: (jnp.minimum(i, npt - 1), 0)) for _ in range(N_MIX)]
    s_specs = [pl.BlockSpec((TM, GROUP_W), lambda i, j: (jnp.maximum(i - npt, 0), 0)) for _ in range(N_MIX)]
    w_specs = [pl.BlockSpec((GROUP_W, TN_OUT), functools.partial(lambda i, j, c: (c, j), c=c))
               for c in range(N_MIX)]
    return pl.pallas_call(
        _wout_kernel,
        grid=(N_TOK // TM, D_MODEL // TN_OUT),
        in_specs=p_specs + s_specs + w_specs + [
            pl.BlockSpec((TM, TN_OUT), lambda i, j: (i, j)),
            pl.BlockSpec((1, 6, TN_OUT), lambda i, j: (_mod_row(i), 0, j))],
        out_specs=pl.BlockSpec((TM, TN_OUT), lambda i, j: (i, j)),
        out_shape=jax.ShapeDtypeStruct((N_TOK, D_MODEL), f32),
        compiler_params=_cparams(("parallel", "arbitrary")),
        name="output_projection",
    )(*prompt_parts, *sample_parts, *([w] * N_MIX), x, mod)


def _split_bf16(a):
    hi = a.astype(bf16)
    lo = (a - hi.astype(f32)).astype(bf16)
    return hi, lo


def _router_kernel(x_ref, g_ref, mod_ref, wr_ref, br_ref, h_ref, r_ref, cnt_ref):
    h = _modulated_norm(x_ref[...], g_ref[...], mod_ref[0, 3:4, :], mod_ref[0, 4:5, :])
    h_ref[...] = h
    h_hi, h_lo = _split_bf16(h)
    w_hi, w_lo = _split_bf16(wr_ref[...])
    lg = (jnp.dot(h_hi, w_hi, preferred_element_type=f32)
          + jnp.dot(h_lo, w_hi, preferred_element_type=f32)
          + jnp.dot(h_hi, w_lo, preferred_element_type=f32)) + br_ref[...]
    lane = lax.broadcasted_iota(jnp.int32, lg.shape, 1).astype(f32)
    ninf = -jnp.inf

    def first_max(v):
        m = jnp.max(v, axis=-1, keepdims=True)
        idx = jnp.min(jnp.where(v == m, lane, float(LANES)), axis=-1, keepdims=True)
        return m, idx

    gmask = lane < MOE_GROUPS
    mg, gi = first_max(jnp.where(gmask, lg, ninf))
    p_top = 1.0 / jnp.sum(jnp.where(gmask, jnp.exp(lg - mg), 0.0), axis=-1, keepdims=True)
    lo = MOE_GROUPS + gi * MOE_PER_GROUP
    le = jnp.where((lane >= lo) & (lane < lo + MOE_PER_GROUP), lg, ninf)
    v1, i1 = first_max(le)
    v2, i2 = first_max(jnp.where(lane == i1, ninf, le))
    t = jnp.exp(v2 - v1)
    w1 = p_top / (1.0 + t)
    w2 = w1 * t

    @pl.when(pl.program_id(0) == 0)
    def _():
        cnt_ref[...] = jnp.zeros(cnt_ref.shape, f32)

    tm = lg.shape[0]
    chosen = jnp.where((lane == i1) | (lane == i2), 1.0, 0.0)
    earlier = (lax.broadcasted_iota(jnp.int32, (tm, tm), 1) < lax.broadcasted_iota(jnp.int32, (tm, tm), 0))
    before = jnp.dot(jnp.where(earlier, 1.0, 0.0).astype(bf16), chosen.astype(bf16),
                     preferred_element_type=f32) + cnt_ref[0:1, :]
    rank1 = jnp.sum(jnp.where(lane == i1, before, 0.0), axis=-1, keepdims=True)
    rank2 = jnp.sum(jnp.where(lane == i2, before, 0.0), axis=-1, keepdims=True)
    cnt_ref[...] = cnt_ref[...] + jnp.sum(chosen, axis=0, keepdims=True)

    slab = jnp.zeros(lg.shape, f32)
    for pos, val in enumerate((i1 - MOE_GROUPS, i2 - MOE_GROUPS, w1, w2, rank1, rank2)):
        slab = jnp.where(lane == float(pos), val, slab)
    r_ref[...] = slab


def moe_router(x, g_norm, mod, wr, br):
    tm = 256
    return pl.pallas_call(
        _router_kernel,
        grid=(N_TOK // tm,),
        in_specs=[pl.BlockSpec((tm, D_MODEL), lambda i: (i, 0)),
                  pl.BlockSpec((1, D_MODEL), lambda i: (0, 0)),
                  pl.BlockSpec((1, 6, D_MODEL), lambda i: (_mod_row(i // (TM // tm)), 0, 0)),
                  pl.BlockSpec((D_MODEL, LANES), lambda i: (0, 0)),
                  pl.BlockSpec((1, LANES), lambda i: (0, 0))],
        out_specs=[pl.BlockSpec((tm, D_MODEL), lambda i: (i, 0)),
                   pl.BlockSpec((tm, LANES), lambda i: (i, 0)),
                   pl.BlockSpec((MOD_ROWS, LANES), lambda i: (0, 0))],
        out_shape=[jax.ShapeDtypeStruct((N_TOK, D_MODEL), f32),
                   jax.ShapeDtypeStruct((N_TOK, LANES), f32),
                   jax.ShapeDtypeStruct((MOD_ROWS, LANES), f32)],
        compiler_params=_cparams(("arbitrary",)),
        name="moe_router",
    )(x, g_norm.reshape(1, D_MODEL), mod, wr, br)


def _ffn_kernel(te_ref, nu_ref, tok_ref, h_hbm, wg_ref, wu_ref, wd_ref, o_ref, xbuf, sem, wg_b, wu_b, wd_b):
    t = pl.program_id(0)
    n_used = nu_ref[0]

    def row_copy(tile, r):
        half = tile % 2
        return pltpu.make_async_copy(h_hbm.at[pl.ds(tok_ref[tile * TM_E + r], 1)],
                                     xbuf.at[half, pl.ds(r, 1)], sem.at[half])

    def for_rows(tile, fn):
        def body(r, carry):
            fn(row_copy(tile, r))
            return carry
        lax.fori_loop(0, TM_E, body, 0, unroll=8)

    @pl.when(t == 0)
    def _():
        for_rows(t, lambda cp: cp.start())

    @pl.when(t + 1 < n_used)
    def _():
        for_rows(t + 1, lambda cp: cp.start())

    @pl.when(t < n_used)
    def _():
        @pl.when((t == 0) | (te_ref[t] != te_ref[jnp.maximum(t - 1, 0)]))
        def _():
            wg_b[...] = wg_ref[0, 0].astype(bf16)
            wu_b[...] = wu_ref[0, 0].astype(bf16)
            wd_b[...] = wd_ref[0, 0].astype(bf16)

        for_rows(t, lambda cp: cp.wait())
        x = xbuf[t % 2].astype(bf16)
        a = jnp.dot(x, wg_b[...], preferred_element_type=f32)
        u = jnp.dot(x, wu_b[...], preferred_element_type=f32)
        mid = (a * jax.nn.sigmoid(a) * u).astype(bf16)
        o_ref[...] = jnp.dot(mid, wd_b[...], preferred_element_type=f32).astype(o_ref.dtype)

    @pl.when(t >= nu_ref[0])
    def _():
        o_ref[...] = jnp.zeros(o_ref.shape, o_ref.dtype)


def expert_ffn(layer, tile_expert, n_used, slot_token, h, wg, wu, wd):
    def w_spec(r, c):
        return pl.BlockSpec((1, 1, r, c), lambda t, te, nu, tok: (layer, te[t], 0, 0),
                            pipeline_mode=pl.Buffered(1))

    return pl.pallas_call(
        _ffn_kernel,
        grid_spec=pltpu.PrefetchScalarGridSpec(
            num_scalar_prefetch=3,
            grid=(N_ETILES,),
            in_specs=[pl.BlockSpec(memory_space=pl.ANY),
                      w_spec(D_MODEL, MOE_FF), w_spec(D_MODEL, MOE_FF), w_spec(MOE_FF, D_MODEL)],
            out_specs=pl.BlockSpec((TM_E, D_MODEL), lambda t, te, nu, tok: (t, 0)),
            scratch_shapes=[pltpu.VMEM((2, TM_E, D_MODEL), f32), pltpu.SemaphoreType.DMA((2,)),
                            pltpu.VMEM((D_MODEL, MOE_FF), bf16), pltpu.VMEM((D_MODEL, MOE_FF), bf16),
                            pltpu.VMEM((MOE_FF, D_MODEL), bf16)]),
        out_shape=jax.ShapeDtypeStruct((N_ETILES * TM_E, D_MODEL), bf16),
        compiler_params=_cparams(("arbitrary",)),
        name="expert_ffn",
    )(tile_expert, n_used, slot_token, h, wg, wu, wd)


def route_layout(route, counts):
    e = route[:, :MOE_TOPK].astype(jnp.int32)
    rank = route[:, 4:4 + MOE_TOPK].astype(jnp.int32)
    cnt = counts[0, MOE_GROUPS:MOE_GROUPS + MOE_EXPERTS].astype(jnp.int32)
    tiles_e = (cnt + TM_E - 1) // TM_E
    tile_end = jnp.cumsum(tiles_e)
    tile_start = tile_end - tiles_e
    pos = tile_start[e] * TM_E + rank
    slot_token = jnp.zeros((N_ETILES * TM_E,), jnp.int32).at[pos.reshape(-1)].set(
        jnp.arange(N_ASSIGN, dtype=jnp.int32) // MOE_TOPK)
    n_used = tile_end[-1]
    t = jnp.minimum(jnp.arange(N_ETILES, dtype=jnp.int32), n_used - 1)
    tile_expert = jnp.sum((t[:, None] >= tile_end[None, :]).astype(jnp.int32), axis=1)
    tile_expert = jnp.minimum(tile_expert, MOE_EXPERTS - 1)
    return pos, slot_token, tile_expert, n_used.reshape(1).astype(jnp.int32)


def _combine_kernel(x_ref, y0_ref, y1_ref, r_ref, mod_ref, o_ref):
    y = r_ref[:, 2:3] * y0_ref[...] + r_ref[:, 3:4] * y1_ref[...]
    o_ref[...] = x_ref[...] + mod_ref[0, 5:6, :] * y


def _combine_norm_kernel(n_prompt_tiles, x_ref, y0_ref, y1_ref, r_ref, mod_ref, g_ref, op_ref, os_ref):
    y = r_ref[:, 2:3] * y0_ref[...] + r_ref[:, 3:4] * y1_ref[...]
    x = x_ref[...] + mod_ref[0, 5:6, :] * y
    out = x * lax.rsqrt(jnp.mean(x * x, axis=-1, keepdims=True) + EPS) * g_ref[...]
    is_prompt = pl.program_id(0) < n_prompt_tiles

    @pl.when(is_prompt)
    def _():
        op_ref[...] = out

    @pl.when(jnp.logical_not(is_prompt))
    def _():
        os_ref[...] = out


def moe_combine(x, y0, y1, route, mod, g_final=None):
    tm = 256
    row = pl.BlockSpec((tm, D_MODEL), lambda i: (i, 0))
    in_specs = [row, row, row, pl.BlockSpec((tm, LANES), lambda i: (i, 0)),
                pl.BlockSpec((1, 6, D_MODEL), lambda i: (_mod_row(i // (TM // tm)), 0, 0))]
    args = [x, y0, y1, route, mod]
    if g_final is not None:
        npt = N_PROMPT // tm
        in_specs.append(pl.BlockSpec((1, D_MODEL), lambda i: (0, 0)))
        args.append(g_final.reshape(1, D_MODEL))
        return pl.pallas_call(
            functools.partial(_combine_norm_kernel, npt),
            grid=(N_TOK // tm,),
            in_specs=in_specs,
            out_specs=[pl.BlockSpec((tm, D_MODEL), lambda i: (jnp.minimum(i, npt - 1), 0)),
                       pl.BlockSpec((tm, D_MODEL), lambda i: (jnp.maximum(i - npt, 0), 0))],
            out_shape=[jax.ShapeDtypeStruct((N_PROMPT, D_MODEL), f32),
                       jax.ShapeDtypeStruct((N_SAMPLE, D_MODEL), f32)],
            compiler_params=_cparams(("arbitrary",)),
            name="moe_combine_final",
        )(*args)
    return pl.pallas_call(
        _combine_kernel,
        grid=(N_TOK // tm,),
        in_specs=in_specs,
        out_specs=row,
        out_shape=jax.ShapeDtypeStruct((N_TOK, D_MODEL), f32),
        compiler_params=_cparams(("parallel",)),
        name="moe_combine",
    )(*args)


def _diff_prep_kernel(rope, q_ref, k_ref, cos_ref, sin_ref, q2_ref, kb_ref):
    lane = lax.broadcasted_iota(jnp.int32, q_ref.shape, 1)

    def rot(x):
        if not rope:
            return x
        partner = jnp.where((lane & 31) >= 16, pltpu.roll(x, 16, 1), pltpu.roll(x, LANES - 16, 1))
        return x * cos_ref[...] + partner * sin_ref[...]

    q = rot(q_ref[...]) * DIFF_HD ** -0.5
    first = lane < DIFF_HD
    q2_ref[0] = jnp.where(first, q, 0.0).astype(bf16)
    q2_ref[1] = jnp.where(first, 0.0, q).astype(bf16)
    kb_ref[...] = rot(k_ref[...]).astype(bf16)


def _rope_tables():
    t = jnp.arange(DEC_SEQ)
    row = (t // GRID_W).astype(f32)
    col = (t % GRID_W).astype(f32)
    nf = DIFF_HD // 4
    inv = ROPE_BASE ** (-jnp.arange(nf, dtype=f32) / nf)
    lane = np.arange(LANES)
    pos = jnp.where(((lane // 32) % 2 == 0)[None, :], row[:, None], col[:, None])
    ang = pos * inv[lane % nf][None, :]
    sign = np.where(lane % 32 >= nf, 1.0, -1.0).astype(np.float32)
    return jnp.cos(ang), jnp.sin(ang) * sign[None, :]


def diff_prep(u, row0, n_rows, rope):
    t = 1024
    r0 = row0 // t
    nseq = DEC_SEQ // t
    cos, sin = _rope_tables() if rope else (jnp.zeros((t, LANES), f32), jnp.zeros((t, LANES), f32))
    tab = pl.BlockSpec((t, LANES), (lambda i, h: (i % nseq, 0)) if rope else (lambda i, h: (0, 0)))
    return pl.pallas_call(
        functools.partial(_diff_prep_kernel, rope),
        grid=(n_rows // t, DIFF_HEADS),
        in_specs=[pl.BlockSpec((t, LANES), lambda i, h: (r0 + i, C_DQ // LANES + h)),
                  pl.BlockSpec((t, LANES), lambda i, h: (r0 + i, C_DK // LANES + h)),
                  tab, tab],
        out_specs=[pl.BlockSpec((2, t, LANES), lambda i, h: (0, i, h)),
                   pl.BlockSpec((t, LANES), lambda i, h: (i, h))],
        out_shape=[jax.ShapeDtypeStruct((2, n_rows, GROUP_W), bf16),
                   jax.ShapeDtypeStruct((n_rows, GROUP_W), bf16)],
        compiler_params=_cparams(("parallel", "parallel")),
        name="diff_prep",
    )(u, u, cos, sin)


def _flash_kernel(n_stack, tq, sub, scale, out_scale, v_transposed, q_ref, k_ref, v_ref, lam_ref, g_ref, o_ref,
                  m_ref, l_ref, acc_ref):
    kj = pl.program_id(3)

    @pl.when(kj == 0)
    def _():
        m_ref[...] = jnp.full(m_ref.shape, -jnp.inf, f32)
        l_ref[...] = jnp.zeros(l_ref.shape, f32)
        acc_ref[...] = jnp.zeros(acc_ref.shape, f32)

    q = q_ref[...].reshape(n_stack * tq, LANES).astype(bf16)
    tk = k_ref.shape[1]
    m, l, acc = m_ref[...], l_ref[...], acc_ref[...]
    for c in range(tk // sub):
        keys = slice(c * sub, (c + 1) * sub)
        s = lax.dot_general(k_ref[0, keys, :].astype(bf16), q, (((1,), (1,)), ((), ())),
                            preferred_element_type=f32)
        if scale != 1.0:
            s = s * scale
        m_new = jnp.maximum(m, jnp.max(s, axis=0, keepdims=True))
        alpha = jnp.exp(m - m_new)
        p = jnp.exp(s - m_new)
        l = alpha * l + jnp.sum(p, axis=0, keepdims=True)
        vt = v_ref[0, :, keys].astype(bf16) if v_transposed else v_ref[0, keys, :].T.astype(bf16)
        acc = alpha * acc + jnp.dot(vt, p.astype(bf16), preferred_element_type=f32)
        m = m_new
    m_ref[...], l_ref[...], acc_ref[...] = m, l, acc

    @pl.when(kj == pl.num_programs(3) - 1)
    def _():
        o = acc_ref[...] / l_ref[...]
        if n_stack == 2:
            o = o[:, :tq] - lam_ref[0:1, 0:1] * o[:, tq:]
            o = o * lax.rsqrt(jnp.mean(o * o, axis=0, keepdims=True) + EPS)
            o = o.T * (g_ref[...] * out_scale)
        else:
            o = o.T
        o_ref[...] = o.astype(o_ref.dtype)


def flash_attention(q, k, v, *, n_batch, lq, tq, tk, qcol, kcol, vcol, sub=512, v_transposed=False, qrow0=0, kb0=0,
                    scale=1.0, lam=None, g=None, out_scale=1.0):
    n_stack = q.shape[0]
    lk = k.shape[1]
    nq = lq // tq
    lam = jnp.zeros((1, LANES), f32) if lam is None else jnp.full((1, LANES), lam, f32)
    g = jnp.ones((1, LANES), f32) if g is None else g.reshape(1, LANES).astype(f32)
    rows = n_stack * tq
    if v_transposed:
        v_spec = pl.BlockSpec((1, LANES, tk), lambda b, h, i, j: (kb0 + b, vcol + h, j))
    else:
        v_spec = pl.BlockSpec((1, tk, LANES), lambda b, h, i, j: (kb0 + b, j, vcol + h))
    return pl.pallas_call(
        functools.partial(_flash_kernel, n_stack, tq, min(sub, tk), scale, out_scale, v_transposed),
        grid=(n_batch, DIFF_HEADS, nq, lk // tk),
        in_specs=[pl.BlockSpec((n_stack, tq, LANES), lambda b, h, i, j: (0, qrow0 + b * nq + i, qcol + h)),
                  pl.BlockSpec((1, tk, LANES), lambda b, h, i, j: (kb0 + b, j, kcol + h)),
                  v_spec,
                  pl.BlockSpec((1, LANES), lambda b, h, i, j: (0, 0)),
                  pl.BlockSpec((1, LANES), lambda b, h, i, j: (0, 0))],
        out_specs=pl.BlockSpec((tq, LANES), lambda b, h, i, j: (b * nq + i, h)),
        out_shape=jax.ShapeDtypeStruct((n_batch * lq, GROUP_W), bf16),
        scratch_shapes=[pltpu.VMEM((1, rows), f32), pltpu.VMEM((1, rows), f32),
                        pltpu.VMEM((LANES, rows), f32)],
        compiler_params=_cparams(("parallel", "parallel", "parallel", "arbitrary")),
        name="flash_attention",
    )(q, k, v, lam, g)


NAT_ROWS = DEC_SEQ // GRID_W
NAT_RB = 8
NAT_BAND = NAT_WH * GRID_W
NEG_BIG = -1e30


def _nat_kernel(q_ref, k_ref, v_ref, kc_ref, vc_ref, bias_ref, o_ref):
    rb = pl.program_id(2)
    scale = NAT_HD ** -0.5
    kc = kc_ref[0].astype(bf16)
    vc = vc_ref[0].astype(bf16)
    nt = (((1,), (1,)), ((), ()))
    for i in range(NAT_RB):
        r = rb * NAT_RB + i
        rs = jnp.clip(r - NAT_WH // 2, 0, NAT_ROWS - NAT_WH)
        k0 = pl.multiple_of(rs * GRID_W, GRID_W)
        kband = k_ref[pl.ds(k0, NAT_BAND), :].astype(bf16)
        vband = v_ref[pl.ds(k0, NAT_BAND), :].astype(bf16)
        q = q_ref[i * GRID_W:(i + 1) * GRID_W, :].astype(bf16)
        sb = lax.dot_general(q, kband, nt, preferred_element_type=f32) * scale + bias_ref[rs - r + NAT_WH - 1, 0]
        sc = lax.dot_general(q, kc, nt, preferred_element_type=f32) * scale
        m = jnp.maximum(jnp.max(sb, axis=-1, keepdims=True), jnp.max(sc, axis=-1, keepdims=True))
        pb = jnp.exp(sb - m)
        pc = jnp.exp(sc - m)
        l = jnp.sum(pb, axis=-1, keepdims=True) + jnp.sum(pc, axis=-1, keepdims=True)
        o = (jnp.dot(pb.astype(bf16), vband, preferred_element_type=f32)
             + jnp.dot(pc.astype(bf16), vc, preferred_element_type=f32)) / l
        o_ref[i * GRID_W:(i + 1) * GRID_W, :] = o.astype(o_ref.dtype)


def _nat_bias_table(rpb):
    colv = np.arange(GRID_W)
    cstart = np.clip(colv - NAT_WW // 2, 0, GRID_W - NAT_WW)
    col_mask = (colv[None, :] >= cstart[:, None]) & (colv[None, :] < cstart[:, None] + NAT_WW)
    col_idx = np.clip(colv[None, :] - colv[:, None] + NAT_WW - 1, 0, 2 * NAT_WW - 2)
    rpb_c = rpb.astype(f32)[:, :, col_idx]
    row_idx = np.arange(NAT_WH)[:, None] + np.arange(NAT_WH)[None, :]
    tab = rpb_c[:, row_idx]
    tab = jnp.where(col_mask[None, None, None], tab, NEG_BIG)
    return tab.transpose(1, 0, 3, 2, 4).reshape(NAT_WH, NAT_HEADS, GRID_W, NAT_BAND)


def nat_attention(u, row0, kc, vc, rpb):
    qblk = NAT_RB * GRID_W
    q0 = row0 // qblk
    b0 = row0 // DEC_SEQ
    return pl.pallas_call(
        _nat_kernel,
        grid=(DEC_BATCH, NAT_HEADS, NAT_ROWS // NAT_RB),
        in_specs=[pl.BlockSpec((qblk, LANES), lambda b, h, r: (q0 + b * (NAT_ROWS // NAT_RB) + r, C_NQ // LANES + h)),
                  pl.BlockSpec((DEC_SEQ, LANES), lambda b, h, r: (b0 + b, C_NK // LANES + h)),
                  pl.BlockSpec((DEC_SEQ, LANES), lambda b, h, r: (b0 + b, C_NV // LANES + h)),
                  pl.BlockSpec((1, PAST_LEN, LANES), lambda b, h, r: (b, 0, h)),
                  pl.BlockSpec((1, PAST_LEN, LANES), lambda b, h, r: (b, 0, h)),
                  pl.BlockSpec((NAT_WH, 1, GRID_W, NAT_BAND), lambda b, h, r: (0, h, 0, 0))],
        out_specs=pl.BlockSpec((qblk, LANES), lambda b, h, r: (b * (NAT_ROWS // NAT_RB) + r, h)),
        out_shape=jax.ShapeDtypeStruct((N_SAMPLE, GROUP_W), bf16),
        compiler_params=_cparams(("parallel", "parallel", "arbitrary")),
        name="nat_attention",
    )(u, u, u, kc, vc, _nat_bias_table(rpb))


def _pool_kernel(seq, u_ref, w_ref, s_ref, o_ref):
    grp = pl.program_id(1)
    for gi, win in enumerate(POOL_WINDOWS):
        @pl.when(grp == gi)
        def _(win=win):
            u = u_ref[...]
            t = lax.broadcasted_iota(jnp.int32, u.shape, 0)
            acc = jnp.zeros_like(u)
            for d in range(-(win // 2), win // 2):
                shifted = u if d == 0 else pltpu.roll(u, (-d) % seq, 0)
                acc += jnp.where((t + d >= 0) & (t + d < seq), shifted, 0.0)
            cnt = (jnp.minimum(t + win // 2, seq) - jnp.maximum(t - win // 2, 0)).astype(f32)
            p = acc / cnt - u
            y = jnp.dot(p.astype(bf16), w_ref[0].astype(bf16), preferred_element_type=f32) * s_ref[...]
            o_ref[...] = y.astype(o_ref.dtype)


def pool_mixer(u, row0, n_seq, seq, w, s):
    return pl.pallas_call(
        functools.partial(_pool_kernel, seq),
        grid=(n_seq, len(POOL_WINDOWS)),
        in_specs=[pl.BlockSpec((seq, POOL_GW), lambda b, g: (row0 // seq + b, g)),
                  pl.BlockSpec((1, POOL_GW, POOL_GW), lambda b, g: (g, 0, 0)),
                  pl.BlockSpec((1, POOL_GW), lambda b, g: (0, g))],
        out_specs=pl.BlockSpec((seq, POOL_GW), lambda b, g: (b, g)),
        out_shape=jax.ShapeDtypeStruct((n_seq * seq, GROUP_W), bf16),
        compiler_params=_cparams(("parallel", "parallel")),
        name="pool_mixer",
    )(u, w, s.reshape(1, GROUP_W))


GLA_QK = GLA_HEADS * GLA_DK


def _gla_kernel(reverse, final, n_chunks, *refs):
    if final:
        (q_ref, k_ref, v_ref, gd_ref, wup_ref, bup_ref, s0_ref, op_ref, gg_ref, gn_ref,
         o_ref, s_ref, st_ref) = refs
    else:
        q_ref, k_ref, v_ref, gd_ref, wup_ref, bup_ref, s0_ref, o_ref, s_ref, st_ref = refs
    j = pl.program_id(1)
    C = GLA_CHUNK

    @pl.when(j == 0)
    def _():
        for h in range(GLA_HEADS):
            s0 = s0_ref[0, h]
            z = jnp.zeros_like(s0)
            st_ref[h] = jnp.concatenate([s0, z] if h % 2 == 0 else [z, s0], axis=0).T

    ti = lax.broadcasted_iota(jnp.int32, (C, C), 0)
    tj = lax.broadcasted_iota(jnp.int32, (C, C), 1)
    tri = (ti <= tj) if reverse else (ti >= tj)
    tri_b = jnp.where(tri, 1.0, 0.0).astype(bf16)
    first_half = lax.broadcasted_iota(jnp.int32, (C, LANES), 1) < GLA_DK
    ref_row = C // 2 - 1 if reverse else C // 2
    end_row = 0 if reverse else C - 1
    nt = (((1,), (1,)), ((), ()))

    def chunk(ci, carry):
        c = n_chunks - 1 - ci if reverse else ci
        rows = pl.ds(pl.multiple_of(c * C, C), C)
        logit = jnp.dot(gd_ref[rows, :].astype(bf16), wup_ref[...], preferred_element_type=f32) + bup_ref[...]
        la = (jnp.minimum(logit, 0.0) - jnp.log1p(jnp.exp(-jnp.abs(logit)))) * (1.0 / GLA_GATE_NORM)
        hi = la.astype(bf16)
        r1 = la - hi.astype(f32)
        mid = r1.astype(bf16)
        lo = (r1 - mid.astype(f32)).astype(bf16)
        b = (jnp.dot(tri_b, hi, preferred_element_type=f32) + jnp.dot(tri_b, mid, preferred_element_type=f32)
             + jnp.dot(tri_b, lo, preferred_element_type=f32))
        bref = b[ref_row:ref_row + 1, :]
        bl = b[end_row:end_row + 1, :]
        q = q_ref[rows, :] * GLA_DK ** -0.5
        k = k_ref[rows, :]
        qs = q * jnp.exp(b - bref)
        ks = (k * jnp.exp(bref - b)).astype(bf16)
        qe = (q * jnp.exp(b)).astype(bf16)
        kd = k * jnp.exp(bl - b)
        ebl = jnp.exp(bl)
        for h in range(GLA_HEADS):
            pair = slice((h // 2) * LANES, (h // 2 + 1) * LANES)
            mine = first_half if h % 2 == 0 else jnp.logical_not(first_half)
            cols = slice(h * GLA_DV, (h + 1) * GLA_DV)
            a = lax.dot_general(jnp.where(mine, qs[:, pair], 0.0).astype(bf16), ks[:, pair], nt,
                                preferred_element_type=f32)
            a = jnp.where(tri, a, 0.0).astype(bf16)
            vh = v_ref[rows, cols]
            st = st_ref[h]
            o = (jnp.dot(a, vh.astype(bf16), preferred_element_type=f32)
                 + lax.dot_general(qe[:, pair], st.astype(bf16), nt, preferred_element_type=f32))
            st_ref[h] = st * ebl[:, pair] + jnp.dot(vh.T.astype(bf16),
                                                    jnp.where(mine, kd[:, pair], 0.0).astype(bf16),
                                                    preferred_element_type=f32)
            if final:
                o = o + op_ref[rows, cols]
                o = o * lax.rsqrt(jnp.mean(o * o, axis=-1, keepdims=True) + EPS) * gn_ref[...]
                gate = gg_ref[rows, cols]
                o = o * (gate * jax.nn.sigmoid(gate))
            o_ref[rows, cols] = o.astype(o_ref.dtype)
        return carry

    lax.fori_loop(0, n_chunks, chunk, 0)

    @pl.when(j == pl.num_programs(1) - 1)
    def _():
        for h in range(GLA_HEADS):
            half = (h % 2) * GLA_DK
            s_ref[0, h] = st_ref[h].T[half:half + GLA_DK, :]


def _gla_pass(u, row0, n_seq, seq, tb, z, w_up, b_up, s0, o_prev=None, g_norm=None):
    reverse = z == 1
    final = o_prev is not None
    nblk = seq // tb
    rb0 = row0 // tb
    wup = jnp.zeros((LANES, GLA_QK), f32).at[z * GLA_RANK:(z + 1) * GLA_RANK].set(w_up[z]).astype(bf16)

    def blk(b, j):
        return b * nblk + (nblk - 1 - j if reverse else j)

    def ucol(width, c0):
        return pl.BlockSpec((tb, width), lambda b, j: (rb0 + blk(b, j), c0 // width))

    in_specs = [ucol(GLA_QK, C_GQ), ucol(GLA_QK, C_GK), ucol(GROUP_W, C_GV), ucol(LANES, C_GD),
                pl.BlockSpec((LANES, GLA_QK), lambda b, j: (0, 0)),
                pl.BlockSpec((1, GLA_QK), lambda b, j: (0, 0)),
                pl.BlockSpec((1, GLA_HEADS, GLA_DK, GLA_DV), lambda b, j: (b, 0, 0, 0))]
    args = [u, u, u, u, wup, b_up[z].reshape(1, GLA_QK), s0]
    if final:
        in_specs += [pl.BlockSpec((tb, GROUP_W), lambda b, j: (blk(b, j), 0)), ucol(GROUP_W, C_GG),
                     pl.BlockSpec((1, GLA_DV), lambda b, j: (0, 0))]
        args += [o_prev, u, g_norm.reshape(1, GLA_DV)]
    return pl.pallas_call(
        functools.partial(_gla_kernel, reverse, final, tb // GLA_CHUNK),
        grid=(n_seq, nblk),
        in_specs=in_specs,
        out_specs=[pl.BlockSpec((tb, GROUP_W), lambda b, j: (blk(b, j), 0)),
                   pl.BlockSpec((1, GLA_HEADS, GLA_DK, GLA_DV), lambda b, j: (b, 0, 0, 0))],
        out_shape=[jax.ShapeDtypeStruct((n_seq * seq, GROUP_W), bf16 if final else f32),
                   jax.ShapeDtypeStruct((n_seq, GLA_HEADS, GLA_DK, GLA_DV), f32)],
        scratch_shapes=[pltpu.VMEM((GLA_HEADS, GLA_DV, LANES), f32)],
        compiler_params=_cparams(("parallel", "arbitrary")),
        name="gla_backward" if reverse else "gla_forward",
    )(*args)


def gla_mixer(u, row0, n_seq, seq, tb, w_up, b_up, g_norm, s0_f, s0_b):
    o_f, s_f = _gla_pass(u, row0, n_seq, seq, tb, 0, w_up, b_up, s0_f)
    o, s_b = _gla_pass(u, row0, n_seq, seq, tb, 1, w_up, b_up, s0_b, o_prev=o_f, g_norm=g_norm)
    return o, jnp.stack([s_f, s_b], axis=1)


def kernel(x_prompt, x_sample, cache_diff_k, cache_diff_v, cache_nat_k, cache_nat_v, state_gla, c, c_ctx, w_ada, b_ada, norm1, w_in, pool_w, pool_scale, gla_w_up, gla_b_up, gla_norm, diff_lambda, diff_norm, nat_rpb, w_out, norm2, router_group_w, router_group_b, router_expert_w, router_expert_b, expert_w_gate, expert_w_up, expert_w_down, norm_final):
    Bp, Lp, Bs, Ls = BATCH, SEQ, DEC_BATCH, DEC_SEQ
    x = jnp.concatenate([x_prompt.reshape(N_PROMPT, D_MODEL), x_sample.reshape(N_SAMPLE, D_MODEL)], axis=0)
    cvec = jnp.zeros((MOD_ROWS, D_MODEL), f32).at[0].set(c_ctx).at[1:1 + DEC_BATCH].set(c)
    mod_all = ada_modulation(cvec, w_ada, b_ada).reshape(DEPTH, MOD_ROWS, 6, D_MODEL)

    new_dk, new_dv, new_nk, new_nv, new_gs = [], [], [], [], []
    for l in range(DEPTH):
        mod = mod_all[l]
        lam_init = 0.8 - 0.6 * math.exp(-0.3 * l)
        lv = diff_lambda[l].astype(f32)
        lam = jnp.exp(jnp.sum(lv[0] * lv[1])) - jnp.exp(jnp.sum(lv[2] * lv[3])) + lam_init

        w_l = w_in[l]
        w_perm = jnp.concatenate(
            [w_l[:, :ORIG_GD], w_l[:, ORIG_GD + 2 * GLA_RANK:], w_l[:, ORIG_GD:ORIG_GD + 2 * GLA_RANK],
             jnp.zeros((D_MODEL, D_IN_PAD - C_GD - 2 * GLA_RANK), f32)], axis=1).astype(bf16)
        u = input_projection(x, norm1[l], mod, w_perm)

        u_seq = u.reshape(N_TOK // Lp, Lp, D_IN_PAD)
        up = u[:N_PROMPT]

        zero = jnp.zeros((Bp, GLA_HEADS, GLA_DK, GLA_DV), f32)
        o_gla_p, gs = gla_mixer(u, 0, Bp, Lp, Lp, gla_w_up[l], gla_b_up[l], gla_norm[l], zero, zero)
        q2_p, kb_p = diff_prep(u, 0, N_PROMPT, rope=False)
        o_diff_p = flash_attention(q2_p, kb_p.reshape(Bp, Lp, GROUP_W), u_seq, n_batch=Bp, lq=Lp, tq=Lp, tk=Lp,
                                   qcol=0, kcol=0, vcol=C_DV // LANES, lam=lam, g=diff_norm[l],
                                   out_scale=1.0 - lam_init)
        o_nat_p = flash_attention(u.reshape(1, N_TOK, D_IN_PAD), u_seq, u_seq, n_batch=Bp, lq=Lp, tq=Lp, tk=Lp,
                                  qcol=C_NQ // LANES, kcol=C_NK // LANES, vcol=C_NV // LANES,
                                  scale=NAT_HD ** -0.5)
        o_pool_p = pool_mixer(u, 0, Bp, Lp, pool_w[l], pool_scale[l])
        new_dk.append(up[:, C_DK:C_DK + GROUP_W].reshape(Bp, Lp, DIFF_HEADS, 2 * DIFF_HD))
        new_dv.append(up[:, C_DV:C_DV + GROUP_W].reshape(Bp, Lp, DIFF_HEADS, DIFF_VD))
        new_nk.append(up[:, C_NK:C_NK + GROUP_W].reshape(Bp, Lp, NAT_HEADS, NAT_HD))
        new_nv.append(up[:, C_NV:C_NV + GROUP_W].reshape(Bp, Lp, NAT_HEADS, NAT_HD))
        new_gs.append(gs)

        st = state_gla[:, l].astype(f32)
        o_gla_s, _ = gla_mixer(u, N_PROMPT, Bs, Ls, 512, gla_w_up[l], gla_b_up[l], gla_norm[l],
                               st[:, 0], st[:, 1])
        q2_s, kb_s = diff_prep(u, N_PROMPT, N_SAMPLE, rope=True)
        k_all = jnp.concatenate([kb_s.reshape(Bs, Ls, GROUP_W),
                                 cache_diff_k[:, l].reshape(Bs, PAST_LEN, GROUP_W).astype(bf16)], axis=1)
        v_all = jnp.concatenate([u[N_PROMPT:, C_DV:C_DV + GROUP_W].astype(bf16).reshape(Bs, Ls, GROUP_W),
                                 cache_diff_v[:, l].reshape(Bs, PAST_LEN, GROUP_W).astype(bf16)], axis=1)
        o_diff_s = flash_attention(q2_s, k_all, jnp.swapaxes(v_all, 1, 2), n_batch=Bs, lq=Ls, tq=2048, tk=1536,
                                   qcol=0, kcol=0, vcol=0, v_transposed=True,
                                   lam=lam, g=diff_norm[l], out_scale=1.0 - lam_init)
        o_nat_s = nat_attention(u, N_PROMPT, cache_nat_k[:, l].reshape(Bs, PAST_LEN, GROUP_W),
                                cache_nat_v[:, l].reshape(Bs, PAST_LEN, GROUP_W), nat_rpb[l])
        o_pool_s = pool_mixer(u, N_PROMPT, Bs, Ls, pool_w[l], pool_scale[l])

        x = output_projection([o_pool_p, o_gla_p, o_diff_p, o_nat_p], [o_pool_s, o_gla_s, o_diff_s, o_nat_s],
                              w_out[l].astype(bf16), x, mod)

        wr = jnp.zeros((D_MODEL, LANES), f32)
        wr = wr.at[:, :MOE_GROUPS].set(router_group_w[l]).at[:, MOE_GROUPS:MOE_GROUPS + MOE_EXPERTS].set(
            router_expert_w[l])
        br = jnp.zeros((1, LANES), f32)
        br = br.at[0, :MOE_GROUPS].set(router_group_b[l]).at[0, MOE_GROUPS:MOE_GROUPS + MOE_EXPERTS].set(
            router_expert_b[l])
        h2, route, counts = moe_router(x, norm2[l], mod, wr, br)
        pos, slot_token, tile_expert, n_used = route_layout(route, counts)
        ys = expert_ffn(l, tile_expert, n_used, slot_token, h2, expert_w_gate, expert_w_up, expert_w_down)
        y0 = ys.at[pos[:, 0]].get(mode="promise_in_bounds")
        y1 = ys.at[pos[:, 1]].get(mode="promise_in_bounds")
        x = moe_combine(x, y0, y1, route, mod, norm_final if l == DEPTH - 1 else None)

    y_prompt = x[0].reshape(Bp, Lp, D_MODEL)
    y_sample = x[1].reshape(Bs, Ls, D_MODEL)
    return (y_prompt, y_sample, jnp.stack(new_dk, axis=1), jnp.stack(new_dv, axis=1),
            jnp.stack(new_nk, axis=1), jnp.stack(new_nv, axis=1), jnp.stack(new_gs, axis=1))
```
